```python
import jax, jax.numpy as jnp
from jax import lax
import numpy as np

D_MODEL = 2048
BATCH = 2
SEQ = 4096
DEPTH = 2
DEC_BATCH = 128
DEC_SEQ = 1
PAST_LEN = 8192
PAGE_SIZE = 128

HEAD_DIM = 64
N_Q_HEADS = 16
N_KV_HEADS = 4
GROUP = N_Q_HEADS // N_KV_HEADS
ATTN_WIDTH = N_Q_HEADS * HEAD_DIM
KV_WIDTH = N_KV_HEADS * HEAD_DIM
WINDOW = 128
ATTN_BLOCK = WINDOW
ROPE_THETA = 10000.0
POOL_WIDTH = D_MODEL // 4
POOL_WINDOWS = (2, 4, 8, 16)
POOL_GROUPS = len(POOL_WINDOWS)
POOL_GROUP_WIDTH = POOL_WIDTH // POOL_GROUPS
POOL_PAD = max(POOL_WINDOWS) - 1
RWKV_HEAD_DIM = 64
RWKV_WIDTH = D_MODEL // 4
RWKV_HEADS = RWKV_WIDTH // RWKV_HEAD_DIM
DECAY_LORA = 96
ICLR_LORA = 96
GATE_LORA = 256
VRES_LORA = 32
RWKV_SPLITS = (RWKV_WIDTH, RWKV_WIDTH, RWKV_WIDTH, DECAY_LORA, ICLR_LORA, GATE_LORA)
RWKV_PROJ = sum(RWKV_SPLITS)
N_BRANCH = 3
IN_SPLITS = (ATTN_WIDTH, KV_WIDTH, KV_WIDTH, POOL_WIDTH, RWKV_PROJ, N_BRANCH * D_MODEL)
IN_WIDTH = sum(IN_SPLITS)
D_FF = ((8 * D_MODEL + 3 * 256 - 1) // (3 * 256)) * 256
RMS_EPS = 1e-6
LNX_EPS = 64e-5
NEG_INF = -1e30

kernel_name = 'hybrid_swa_pool_rwkv7_adaln_step'


def _split(z, sizes):
    idx = np.cumsum(sizes)[:-1].tolist()
    return jnp.split(z, idx, axis=-1)


def _rms_norm(x, g):
    xf = x.astype(jnp.float32)
    return xf * lax.rsqrt(jnp.mean(xf * xf, axis=-1, keepdims=True) + RMS_EPS) * g.astype(jnp.float32)


def _mod_norm(x, g, shift, scale):
    y = _rms_norm(x, g) * (1.0 + scale.astype(jnp.float32)) + shift.astype(jnp.float32)
    return y.astype(x.dtype)


def _rope(x, pos):
    half = HEAD_DIM // 2
    inv = ROPE_THETA ** (-jnp.arange(half, dtype=jnp.float32) * 2.0 / HEAD_DIM)
    ang = pos.astype(jnp.float32)[:, None] * inv[None, :]
    cos = jnp.cos(ang)[None, :, None, :]
    sin = jnp.sin(ang)[None, :, None, :]
    xf = x.astype(jnp.float32)
    x1, x2 = xf[..., :half], xf[..., half:]
    return jnp.concatenate([x1 * cos - x2 * sin, x2 * cos + x1 * sin], axis=-1).astype(x.dtype)


def _sink_softmax(s, mask, sink):
    sink = sink.astype(jnp.float32).reshape(N_KV_HEADS, GROUP, 1, 1)
    s = jnp.where(mask, s, NEG_INF)
    m = jnp.maximum(jnp.max(s, axis=-1, keepdims=True), sink)
    p = jnp.exp(s - m)
    return p / (jnp.sum(p, axis=-1, keepdims=True) + jnp.exp(sink - m))


def _attn_prompt(q, k, v, sink):
    B, T = q.shape[:2]
    nb = T // ATTN_BLOCK
    qb = q.reshape(B, nb, ATTN_BLOCK, N_KV_HEADS, GROUP, HEAD_DIM)

    def with_prev(t):
        tb = t.reshape(B, nb, ATTN_BLOCK, N_KV_HEADS, HEAD_DIM)
        prev = jnp.concatenate([jnp.zeros_like(tb[:, :1]), tb[:, :-1]], axis=1)
        return jnp.concatenate([prev, tb], axis=2)

    kw, vw = with_prev(k), with_prev(v)
    s = jnp.einsum('bnqhgd,bnkhd->bnhgqk', qb, kw).astype(jnp.float32) * (HEAD_DIM ** -0.5)
    blk = jnp.arange(nb)[:, None] * ATTN_BLOCK
    qpos = blk + jnp.arange(ATTN_BLOCK)[None, :]
    kpos = blk - ATTN_BLOCK + jnp.arange(2 * ATTN_BLOCK)[None, :]
    rel = qpos[:, :, None] - kpos[:, None, :]
    mask = (rel >= 0) & (rel <= WINDOW) & (kpos[:, None, :] >= 0)
    p = _sink_softmax(s, mask[None, :, None, None], sink)
    o = jnp.einsum('bnhgqk,bnkhd->bnqhgd', p, vw.astype(jnp.float32))
    return o.reshape(B, T, ATTN_WIDTH).astype(q.dtype)


def _attn_sample(q, k, v, k_buf, v_buf, q_pos, sink):
    B, S = q.shape[:2]
    L = k_buf.shape[1]
    k_all = jnp.concatenate([k_buf.astype(k.dtype), k], axis=1)
    v_all = jnp.concatenate([v_buf.astype(v.dtype), v], axis=1)
    kpos = jnp.concatenate([q_pos[0] - L + jnp.arange(L), q_pos])
    qg = q.reshape(B, S, N_KV_HEADS, GROUP, HEAD_DIM)
    s = jnp.einsum('bqhgd,bkhd->bhgqk', qg, k_all).astype(jnp.float32) * (HEAD_DIM ** -0.5)
    rel = q_pos[:, None] - kpos[None, :]
    mask = (rel >= 0) & (rel <= WINDOW)
    p = _sink_softmax(s, mask, sink)
    o = jnp.einsum('bhgqk,bkhd->bqhgd', p, v_all.astype(jnp.float32)).reshape(B, S, ATTN_WIDTH)
    return o.astype(q.dtype), k_all[:, -L:], v_all[:, -L:]


def _pool(u, u_prev, pos, pool_w, pool_scale):
    B, T, _ = u.shape
    u_ext = jnp.concatenate([u_prev.astype(u.dtype), u], axis=1)
    uf = u_ext.astype(jnp.float32)
    cs = jnp.concatenate([jnp.zeros((B, 1, POOL_WIDTH), jnp.float32), jnp.cumsum(uf, axis=1)], axis=1)
    end = cs[:, POOL_PAD + 1:]
    means = []
    for gi, w in enumerate(POOL_WINDOWS):
        lo, hi = gi * POOL_GROUP_WIDTH, (gi + 1) * POOL_GROUP_WIDTH
        start = cs[:, POOL_PAD + 1 - w:POOL_PAD + 1 - w + T, lo:hi]
        cnt = jnp.minimum(pos + 1, w).astype(jnp.float32)[None, :, None]
        means.append((end[..., lo:hi] - start) / cnt)
    d = jnp.concatenate(means, axis=-1) - uf[:, POOL_PAD:]
    z = jnp.einsum('btgc,gcd->btgd', d.reshape(B, T, POOL_GROUPS, POOL_GROUP_WIDTH), pool_w.astype(jnp.float32))
    z = z.reshape(B, T, POOL_WIDTH) * pool_scale.astype(jnp.float32)
    return z.astype(u.dtype), u_ext[:, -POOL_PAD:]


def _heads(t):
    return t.reshape(t.shape[:-1] + (RWKV_HEADS, RWKV_HEAD_DIM))


def _wkv_step(S, inp):
    r_t, w_t, k_t, v_t, a_t, b_t = inp
    sa = jnp.einsum('bhij,bhj->bhi', S, a_t)
    S = S * w_t[:, :, None, :] + sa[..., None] * b_t[:, :, None, :] + v_t[..., None] * k_t[:, :, None, :]
    return S, jnp.einsum('bhij,bhj->bhi', S, r_t)


def _rwkv(pr, shift_prev, wkv0, lp, vres, v_first):
    B, T, _ = pr.shape
    f32 = jnp.float32
    prev = jnp.concatenate([shift_prev[:, None, :].astype(pr.dtype), pr[:, :-1]], axis=1)
    xm = pr + (prev - pr) * lp['rwkv_mu']
    r, k, v, dw, da, dg = _split(xm, RWKV_SPLITS)
    w_log = -jax.nn.softplus(-(lp['rwkv_w0'] + jnp.tanh(dw) @ lp['rwkv_w2']).astype(f32)) - 0.5
    decay = jnp.exp(-jnp.exp(w_log))
    a = jax.nn.sigmoid((lp['rwkv_a0'] + da @ lp['rwkv_a2']).astype(f32))
    g = (jax.nn.sigmoid(dg) @ lp['rwkv_g2']).astype(f32)
    if vres is None:
        v_first = v
    else:
        vw1, vw2, vb = vres
        v = v + (v_first - v) * jax.nn.sigmoid(vb + (v @ vw1) @ vw2)
    r, k, v = r.astype(f32), k.astype(f32), v.astype(f32)
    kk = _heads(k * lp['rwkv_k_k'].astype(f32))
    kk = kk / jnp.maximum(jnp.sqrt(jnp.sum(kk * kk, axis=-1, keepdims=True)), 1e-12)
    k = k * (1.0 + (a - 1.0) * lp['rwkv_k_a'].astype(f32))
    rh, kh, vh, ah, wh = _heads(r), _heads(k), _heads(v), _heads(a), _heads(decay)
    seq = tuple(jnp.moveaxis(t, 1, 0) for t in (rh, wh, kh, vh, -kk, kk * ah))
    S, y = lax.scan(_wkv_step, wkv0.astype(f32), seq)
    y = jnp.moveaxis(y, 0, 1)
    mu = jnp.mean(y, axis=-1, keepdims=True)
    var = jnp.mean(jnp.square(y - mu), axis=-1, keepdims=True)
    y = ((y - mu) * lax.rsqrt(var + LNX_EPS)).reshape(B, T, RWKV_WIDTH)
    y = y * lp['rwkv_lnx_g'].astype(f32) + lp['rwkv_lnx_b'].astype(f32)
    bonus = jnp.sum(rh * kh * lp['rwkv_r_k'].astype(f32), axis=-1, keepdims=True) * vh
    y = (y + bonus.reshape(B, T, RWKV_WIDTH)) * g
    return y.astype(pr.dtype), v_first, pr[:, -1], S


def _layer(x, c, pos, lp, vres, v_first, past):
    B, T, _ = x.shape
    mod = jax.nn.silu(c) @ lp['ada_w'] + lp['ada_b']
    sh1, sc1, gt1, sh2, sc2, gt2 = jnp.split(mod[:, None, :], 6, axis=-1)
    h = _mod_norm(x, lp['norm1_g'], sh1, sc1)
    q, k, v, u, pr, gl = _split(h @ lp['w_in'], IN_SPLITS)
    q = _rope(q.reshape(B, T, N_Q_HEADS, HEAD_DIM), pos)
    k = _rope(k.reshape(B, T, N_KV_HEADS, HEAD_DIM), pos)
    v = v.reshape(B, T, N_KV_HEADS, HEAD_DIM)
    if past is None:
        o_attn = _attn_prompt(q, k, v, lp['attn_sink'])
        k_new, v_new = k[:, -WINDOW:], v[:, -WINDOW:]
        pool_prev = jnp.zeros((B, POOL_PAD, POOL_WIDTH), x.dtype)
        shift_prev = jnp.zeros((B, RWKV_PROJ), x.dtype)
        wkv0 = jnp.zeros((B, RWKV_HEADS, RWKV_HEAD_DIM, RWKV_HEAD_DIM), jnp.float32)
    else:
        k_buf, v_buf, pool_prev, shift_prev, wkv0 = past
        o_attn, k_new, v_new = _attn_sample(q, k, v, k_buf, v_buf, pos, lp['attn_sink'])
    z_pool, pool_new = _pool(u, pool_prev, pos, lp['pool_w'], lp['pool_scale'])
    y_rwkv, v_first, shift_new, wkv_new = _rwkv(pr, shift_prev, wkv0, lp, vres, v_first)
    g_attn, g_pool, g_rwkv = jnp.split(jax.nn.sigmoid(gl), N_BRANCH, axis=-1)
    merged = (g_attn * (o_attn @ lp['w_attn_out'])
              + g_pool * (z_pool @ lp['w_pool_out'])
              + g_rwkv * (y_rwkv @ lp['w_rwkv_out']))
    x = x + (1.0 + gt1) * (merged @ lp['w_o'])
    h2 = _mod_norm(x, lp['norm2_g'], sh2, sc2)
    ffn = (jax.nn.silu(h2 @ lp['w_ffn_gate']) * (h2 @ lp['w_ffn_up'])) @ lp['w_ffn_down']
    x = x + (1.0 + gt2) * ffn
    return x, v_first, (k_new, v_new, pool_new, shift_new, wkv_new)


def _trunk(x, c, pos, layers, vres, pasts, final_g):
    v_first = None
    new = []
    for l in range(DEPTH):
        x, v_first, st = _layer(x, c, pos, layers[l], vres[l], v_first, pasts[l])
        new.append(st)
    y = _rms_norm(x, final_g).astype(x.dtype)
    return y, [jnp.stack([s[i] for s in new]) for i in range(5)]


def setup_inputs(seed: int = 0) -> dict:
    key = jax.random.key(seed)
    ks = iter(jax.random.split(key, 48))
    f32 = jnp.float32

    def nrm(shape, scale):
        return jax.random.normal(next(ks), shape, f32) * scale

    def gain(shape, s=0.02):
        return 1.0 + nrm(shape, s)

    L = DEPTH
    win_len = min(WINDOW, PAST_LEN)
    return {
        'x_prompt': nrm((BATCH, SEQ, D_MODEL), 1.0),
        'x_sample': nrm((DEC_BATCH, DEC_SEQ, D_MODEL), 1.0),
        'c_prompt': nrm((BATCH, D_MODEL), 1.0),
        'c_sample': nrm((DEC_BATCH, D_MODEL), 1.0),
        'cache_k_win': nrm((L, DEC_BATCH, win_len, N_KV_HEADS, HEAD_DIM), 1.0),
        'cache_v_win': nrm((L, DEC_BATCH, win_len, N_KV_HEADS, HEAD_DIM), 1.0),
        'state_pool': nrm((L, DEC_BATCH, POOL_PAD, POOL_WIDTH), 1.0),
        'state_shift': nrm((L, DEC_BATCH, RWKV_PROJ), 1.0),
        'state_wkv': nrm((L, DEC_BATCH, RWKV_HEADS, RWKV_HEAD_DIM, RWKV_HEAD_DIM), 0.3),
        'ada_w': nrm((L, D_MODEL, 6 * D_MODEL), 0.5 * D_MODEL ** -0.5),
        'ada_b': nrm((L, 6 * D_MODEL), 0.02),
        'norm1_g': gain((L, D_MODEL)),
        'norm2_g': gain((L, D_MODEL)),
        'w_in': nrm((L, D_MODEL, IN_WIDTH), D_MODEL ** -0.5),
        'attn_sink': nrm((L, N_Q_HEADS), 1.0),
        'w_attn_out': nrm((L, ATTN_WIDTH, D_MODEL), ATTN_WIDTH ** -0.5),
        'pool_w': nrm((L, POOL_GROUPS, POOL_GROUP_WIDTH, POOL_GROUP_WIDTH), POOL_GROUP_WIDTH ** -0.5),
        'pool_scale': gain((L, POOL_WIDTH), 0.1),
        'w_pool_out': nrm((L, POOL_WIDTH, D_MODEL), POOL_WIDTH ** -0.5),
        'rwkv_mu': jax.random.uniform(next(ks), (L, RWKV_PROJ), f32),
        'rwkv_w0': -0.5 + nrm((L, RWKV_WIDTH), 0.5),
        'rwkv_w2': nrm((L, DECAY_LORA, RWKV_WIDTH), 0.5 * DECAY_LORA ** -0.5),
        'rwkv_a0': nrm((L, RWKV_WIDTH), 0.1),
        'rwkv_a2': nrm((L, ICLR_LORA, RWKV_WIDTH), 0.5 * ICLR_LORA ** -0.5),
        'rwkv_g2': nrm((L, GATE_LORA, RWKV_WIDTH), GATE_LORA ** -0.5),
        'rwkv_k_k': 0.85 + nrm((L, RWKV_WIDTH), 0.05),
        'rwkv_k_a': gain((L, RWKV_WIDTH), 0.05),
        'rwkv_r_k': nrm((L, RWKV_HEADS, RWKV_HEAD_DIM), 0.1),
        'rwkv_lnx_g': gain((L, RWKV_WIDTH)),
        'rwkv_lnx_b': nrm((L, RWKV_WIDTH), 0.01),
        'w_rwkv_out': nrm((L, RWKV_WIDTH, D_MODEL), RWKV_WIDTH ** -0.5),
        'vres_w1': nrm((L - 1, RWKV_WIDTH, VRES_LORA), RWKV_WIDTH ** -0.5),
        'vres_w2': nrm((L - 1, VRES_LORA, RWKV_WIDTH), VRES_LORA ** -0.5),
        'vres_b': nrm((L - 1, RWKV_WIDTH), 0.1),
        'w_o': nrm((L, D_MODEL, D_MODEL), D_MODEL ** -0.5),
        'w_ffn_gate': nrm((L, D_MODEL, D_FF), D_MODEL ** -0.5),
        'w_ffn_up': nrm((L, D_MODEL, D_FF), D_MODEL ** -0.5),
        'w_ffn_down': nrm((L, D_FF, D_MODEL), D_FF ** -0.5),
        'final_norm_g': gain((D_MODEL,)),
    }


def reference(x_prompt, x_sample, c_prompt, c_sample, cache_k_win, cache_v_win, state_pool, state_shift, state_wkv,
              ada_w, ada_b, norm1_g, norm2_g, w_in, attn_sink, w_attn_out, pool_w, pool_scale, w_pool_out,
              rwkv_mu, rwkv_w0, rwkv_w2, rwkv_a0, rwkv_a2, rwkv_g2, rwkv_k_k, rwkv_k_a, rwkv_r_k,
              rwkv_lnx_g, rwkv_lnx_b, w_rwkv_out, vres_w1, vres_w2, vres_b, w_o,
              w_ffn_gate, w_ffn_up, w_ffn_down, final_norm_g):
    layers = [dict(ada_w=ada_w[l], ada_b=ada_b[l], norm1_g=norm1_g[l], norm2_g=norm2_g[l], w_in=w_in[l],
                   attn_sink=attn_sink[l], w_attn_out=w_attn_out[l], pool_w=pool_w[l], pool_scale=pool_scale[l],
                   w_pool_out=w_pool_out[l], rwkv_mu=rwkv_mu[l], rwkv_w0=rwkv_w0[l], rwkv_w2=rwkv_w2[l],
                   rwkv_a0=rwkv_a0[l], rwkv_a2=rwkv_a2[l], rwkv_g2=rwkv_g2[l], rwkv_k_k=rwkv_k_k[l],
                   rwkv_k_a=rwkv_k_a[l], rwkv_r_k=rwkv_r_k[l], rwkv_lnx_g=rwkv_lnx_g[l], rwkv_lnx_b=rwkv_lnx_b[l],
                   w_rwkv_out=w_rwkv_out[l], w_o=w_o[l], w_ffn_gate=w_ffn_gate[l], w_ffn_up=w_ffn_up[l],
                   w_ffn_down=w_ffn_down[l])
              for l in range(DEPTH)]
    vres = [None] + [(vres_w1[l - 1], vres_w2[l - 1], vres_b[l - 1]) for l in range(1, DEPTH)]
    pasts = [(cache_k_win[l], cache_v_win[l], state_pool[l], state_shift[l], state_wkv[l]) for l in range(DEPTH)]
    pos_p = jnp.arange(x_prompt.shape[1])
    pos_s = PAST_LEN + jnp.arange(x_sample.shape[1])
    y_prompt, st_p = _trunk(x_prompt, c_prompt, pos_p, layers, vres, [None] * DEPTH, final_norm_g)
    y_sample, st_s = _trunk(x_sample, c_sample, pos_s, layers, vres, pasts, final_norm_g)
    k_win_p, v_win_p, pool_p, shift_p, wkv_p = st_p
    k_win_s, v_win_s, pool_s, shift_s, wkv_s = st_s
    return (y_prompt, y_sample, k_win_p, v_win_p, pool_p, shift_p, wkv_p, k_win_s, v_win_s, pool_s, shift_s, wkv_s)
```

```python
import functools

import numpy as np
import jax
import jax.numpy as jnp
from jax import lax
from jax.experimental import pallas as pl
from jax.experimental.pallas import tpu as pltpu

F32 = jnp.float32
BF16 = jnp.bfloat16

D_MODEL = 2048
HEAD_DIM = 64
N_Q_HEADS = 16
N_KV_HEADS = 4
GROUP = N_Q_HEADS // N_KV_HEADS
ATTN_WIDTH = N_Q_HEADS * HEAD_DIM
KV_WIDTH = N_KV_HEADS * HEAD_DIM
WINDOW = 128
ROPE_THETA = 10000.0
POOL_WIDTH = 512
POOL_WINDOWS = (2, 4, 8, 16)
POOL_GROUP_WIDTH = 128
POOL_PAD = 15
RWKV_HEAD_DIM = 64
RWKV_WIDTH = 512
RWKV_HEADS = 8
DECAY_LORA = 96
ICLR_LORA = 96
GATE_LORA = 256
VRES_LORA = 32
RWKV_PROJ = 3 * RWKV_WIDTH + DECAY_LORA + ICLR_LORA + GATE_LORA
RWKV_PROJ_PAD = 2048
D_FF = 5632
PAST_LEN = 8192
RMS_EPS = 1e-6
LNX_EPS = 64e-5
NEG_INF = -1e30

Q_OFF = 0
K_OFF = ATTN_WIDTH
V_OFF = K_OFF + KV_WIDTH
U_OFF = V_OFF + KV_WIDTH
PR_OFF = U_OFF + POOL_WIDTH
GL_OFF = PR_OFF + RWKV_PROJ_PAD
IN_PAD = GL_OFF + 3 * D_MODEL
ROPE_END = V_OFF

LANES = 128
SUBLANES = 8
VMEM_LIMIT = 56 * 2**20


def _cparams(*sem):
    return pltpu.CompilerParams(dimension_semantics=sem, vmem_limit_bytes=VMEM_LIMIT)


def _row_tile(t, pref):
    tm = min(t, pref)
    while t % tm:
        tm -= SUBLANES
    return tm


def _sigmoid(x):
    return 1.0 / (1.0 + jnp.exp(-x))


def _mod_norm(x, g, sh, sc):
    ms = jnp.mean(x * x, axis=-1, keepdims=True)
    return (x * lax.rsqrt(ms + RMS_EPS) * g) * (1.0 + sc) + sh


def _bdot(a, b):
    return jnp.dot(a.astype(BF16), b.astype(BF16), preferred_element_type=F32)


def _seg_sum(x, ones_bd):
    hi = x.astype(BF16)
    lo = (x - hi.astype(F32)).astype(BF16)
    return (jnp.dot(hi, ones_bd, preferred_element_type=F32)
            + jnp.dot(lo, ones_bd, preferred_element_type=F32))


def _mod_spec(mod, tm, chunk):
    per_row = mod.shape[1] != 1
    rows = tm if per_row else 1
    return pl.BlockSpec((None, rows, D_MODEL), lambda g, i, j=0: (g, i if per_row else 0, chunk))


def _ada_kernel(c_ref, w_ref, b_ref, o_ref):
    c = c_ref[...]
    o_ref[...] = _bdot(c * _sigmoid(c), w_ref[...]) + b_ref[...]


def _ada(c_all, ada_w, ada_b):
    depth, _, width = ada_w.shape
    rows = c_all.shape[0]
    tn = 1024
    return pl.pallas_call(
        _ada_kernel,
        grid=(depth, width // tn),
        in_specs=[pl.BlockSpec((rows, D_MODEL), lambda l, j: (0, 0)),
                  pl.BlockSpec((None, D_MODEL, tn), lambda l, j: (l, 0, j)),
                  pl.BlockSpec((None, 1, tn), lambda l, j: (l, 0, j))],
        out_specs=pl.BlockSpec((None, rows, tn), lambda l, j: (l, 0, j)),
        out_shape=jax.ShapeDtypeStruct((depth, rows, width), F32),
        compiler_params=_cparams("parallel", "parallel"),
        name="ada",
    )(c_all, ada_w, ada_b.reshape(depth, 1, width))


IN_TN = 512


def _in_proj_kernel(x_ref, g_ref, sh_ref, sc_ref, cos_ref, sin_ref, w_ref, o_ref, h_ref):
    j = pl.program_id(2)

    @pl.when(j == 0)
    def _():
        h_ref[...] = _mod_norm(x_ref[...], g_ref[...], sh_ref[...], sc_ref[...]).astype(BF16)

    acc = jnp.dot(h_ref[...], w_ref[...].astype(BF16), preferred_element_type=F32)

    @pl.when(j * IN_TN < ROPE_END)
    def _():
        reps = IN_TN // LANES
        cos = jnp.concatenate([cos_ref[...]] * reps, axis=1)
        sin = jnp.concatenate([sin_ref[...]] * reps, axis=1)
        lane = lax.broadcasted_iota(jnp.int32, acc.shape, 1)
        first_half = (lane % HEAD_DIM) < (HEAD_DIM // 2)
        rot = jnp.where(first_half, pltpu.roll(acc, IN_TN - HEAD_DIM // 2, 1),
                        pltpu.roll(acc, HEAD_DIM // 2, 1))
        roped = acc * cos + rot * sin
        o_ref[...] = jnp.where(j * IN_TN + lane < ROPE_END, roped, acc)

    @pl.when(j * IN_TN >= ROPE_END)
    def _():
        o_ref[...] = acc


def _in_proj(x, g1, mod, w_pad, cos, sin, tm):
    groups, t, _ = x.shape
    return pl.pallas_call(
        _in_proj_kernel,
        grid=(groups, t // tm, IN_PAD // IN_TN),
        in_specs=[pl.BlockSpec((None, tm, D_MODEL), lambda g, i, j: (g, i, 0)),
                  pl.BlockSpec((1, D_MODEL), lambda g, i, j: (0, 0)),
                  _mod_spec(mod, tm, 0), _mod_spec(mod, tm, 1),
                  pl.BlockSpec((tm, LANES), lambda g, i, j: (i, 0)),
                  pl.BlockSpec((tm, LANES), lambda g, i, j: (i, 0)),
                  pl.BlockSpec((D_MODEL, IN_TN), lambda g, i, j: (0, j))],
        out_specs=pl.BlockSpec((None, tm, IN_TN), lambda g, i, j: (g, i, j)),
        out_shape=jax.ShapeDtypeStruct((groups, t, IN_PAD), F32),
        scratch_shapes=[pltpu.VMEM((tm, D_MODEL), BF16)],
        compiler_params=_cparams("parallel", "parallel", "arbitrary"),
        name="in_proj",
    )(x, g1, mod, mod, cos, sin, w_pad)


def _rope_tables(pos):
    half = HEAD_DIM // 2
    inv = ROPE_THETA ** (-jnp.arange(half, dtype=F32) * 2.0 / HEAD_DIM)
    ang = pos.astype(F32)[:, None] * inv[None, :]
    cos, sin = jnp.cos(ang), jnp.sin(ang)
    reps = LANES // HEAD_DIM
    cos_t = jnp.concatenate([cos, cos] * reps, axis=1)
    sin_t = jnp.concatenate([-sin, sin] * reps, axis=1)
    return cos_t, sin_t


def _attn_prompt_kernel(sink_ref, q_ref, k_ref, v_ref, kp_ref, vp_ref, o_ref):
    n = pl.program_id(1)
    blk = q_ref.shape[0]
    q = q_ref[...]
    k2 = jnp.concatenate([kp_ref[...], k_ref[...]], axis=0).astype(BF16)
    v2 = jnp.concatenate([vp_ref[...], v_ref[...]], axis=0).astype(BF16)
    qi = lax.broadcasted_iota(jnp.int32, (blk, 2 * blk), 0)
    kc = lax.broadcasted_iota(jnp.int32, (blk, 2 * blk), 1)
    rel = qi - kc + blk
    first_key = jnp.where(n > 0, 0, blk)
    mask = (rel >= 0) & (rel <= WINDOW) & (kc >= first_key)
    outs = []
    for hd in range(N_Q_HEADS):
        h = hd // GROUP
        kh = k2[:, h * HEAD_DIM:(h + 1) * HEAD_DIM]
        vh = v2[:, h * HEAD_DIM:(h + 1) * HEAD_DIM]
        qh = q[:, hd * HEAD_DIM:(hd + 1) * HEAD_DIM].astype(BF16)
        s = lax.dot_general(qh, kh, (((1,), (1,)), ((), ())), preferred_element_type=F32) * (HEAD_DIM ** -0.5)
        s = jnp.where(mask, s, NEG_INF)
        sk = sink_ref[hd]
        m = jnp.maximum(jnp.max(s, axis=1, keepdims=True), sk)
        p = jnp.exp(s - m)
        den = jnp.sum(p, axis=1, keepdims=True) + jnp.exp(sk - m)
        outs.append(jnp.dot(p.astype(BF16), vh, preferred_element_type=F32) / den)
    o_ref[...] = jnp.concatenate(outs, axis=1).astype(o_ref.dtype)


def _attn_prompt(z, sink):
    groups, t, _ = z.shape
    blk = WINDOW
    kcol, vcol = K_OFF // KV_WIDTH, V_OFF // KV_WIDTH
    prev = lambda col: (lambda g, n: (g, jnp.maximum(n - 1, 0), col))
    return pl.pallas_call(
        _attn_prompt_kernel,
        grid=(groups, t // blk),
        in_specs=[pl.BlockSpec(memory_space=pltpu.SMEM),
                  pl.BlockSpec((None, blk, ATTN_WIDTH), lambda g, n: (g, n, 0)),
                  pl.BlockSpec((None, blk, KV_WIDTH), lambda g, n: (g, n, kcol)),
                  pl.BlockSpec((None, blk, KV_WIDTH), lambda g, n: (g, n, vcol)),
                  pl.BlockSpec((None, blk, KV_WIDTH), prev(kcol)),
                  pl.BlockSpec((None, blk, KV_WIDTH), prev(vcol))],
        out_specs=pl.BlockSpec((None, blk, ATTN_WIDTH), lambda g, n: (g, n, 0)),
        out_shape=jax.ShapeDtypeStruct((groups, t, ATTN_WIDTH), BF16),
        compiler_params=_cparams("parallel", "parallel"),
        name="attn_prompt",
    )(sink, z, z, z, z, z)


def _attn_sample_kernel(sink_ref, q_ref, kn_ref, vn_ref, kb_ref, vb_ref, o_ref):
    bb = q_ref.shape[0]
    row = lax.broadcasted_iota(jnp.int32, (N_Q_HEADS, KV_WIDTH), 0)
    lane = lax.broadcasted_iota(jnp.int32, (N_Q_HEADS, KV_WIDTH), 1)
    own = (row // GROUP) == (lane // HEAD_DIM)
    sink = sink_ref[...]
    scale = HEAD_DIM ** -0.5

    def body(b, carry):
        q = q_ref[b]
        qbd = jnp.where(own, jnp.concatenate([q] * N_KV_HEADS, axis=1), 0.0)
        s = lax.dot_general(qbd.astype(BF16), kb_ref[b].astype(BF16), (((1,), (1,)), ((), ())),
                            preferred_element_type=F32) * scale
        s_new = jnp.sum(qbd * kn_ref[b], axis=1, keepdims=True) * scale
        m = jnp.maximum(jnp.maximum(jnp.max(s, axis=1, keepdims=True), s_new), sink)
        p = jnp.exp(s - m)
        p_new = jnp.exp(s_new - m)
        den = jnp.sum(p, axis=1, keepdims=True) + p_new + jnp.exp(sink - m)
        o = jnp.dot(p.astype(BF16), vb_ref[b].astype(BF16), preferred_element_type=F32) + p_new * vn_ref[b]
        o = jnp.where(own, o, 0.0)
        o64 = o[:, 0:HEAD_DIM]
        for h in range(1, N_KV_HEADS):
            o64 = o64 + o[:, h * HEAD_DIM:(h + 1) * HEAD_DIM]
        o_ref[b] = o64 / den
        return carry

    lax.fori_loop(0, bb, body, 0)


def _attn_sample(q3, k_new, v_new, k_buf, v_buf, sink):
    nb, win = k_buf.shape[0], k_buf.shape[1]
    bb = _row_tile(nb, 16)
    return pl.pallas_call(
        _attn_sample_kernel,
        grid=(nb // bb,),
        in_specs=[pl.BlockSpec((N_Q_HEADS, 1), lambda i: (0, 0)),
                  pl.BlockSpec((bb, N_Q_HEADS, HEAD_DIM), lambda i: (i, 0, 0)),
                  pl.BlockSpec((bb, 1, KV_WIDTH), lambda i: (i, 0, 0)),
                  pl.BlockSpec((bb, 1, KV_WIDTH), lambda i: (i, 0, 0)),
                  pl.BlockSpec((bb, win, KV_WIDTH), lambda i: (i, 0, 0)),
                  pl.BlockSpec((bb, win, KV_WIDTH), lambda i: (i, 0, 0))],
        out_specs=pl.BlockSpec((bb, N_Q_HEADS, HEAD_DIM), lambda i: (i, 0, 0)),
        out_shape=jax.ShapeDtypeStruct((nb, N_Q_HEADS, HEAD_DIM), F32),
        compiler_params=_cparams("parallel"),
        name="attn_sample",
    )(sink.reshape(N_Q_HEADS, 1), q3, k_new, v_new, k_buf, v_buf)


POOL_HALO = 16


def _pool_prompt_kernel(u_ref, halo_ref, w_ref, sc_ref, o_ref, ext_ref):
    i = pl.program_id(1)
    tm = u_ref.shape[0]
    ext_ref[0:POOL_HALO, :] = jnp.where(i > 0, halo_ref[...], 0.0)
    ext_ref[POOL_HALO:POOL_HALO + tm, :] = u_ref[...]
    pos = i * tm + lax.broadcasted_iota(jnp.int32, (tm, 1), 0)
    for gi, w in enumerate(POOL_WINDOWS):
        cols = slice(gi * POOL_GROUP_WIDTH, (gi + 1) * POOL_GROUP_WIDTH)
        u = ext_ref[POOL_HALO:POOL_HALO + tm, cols]
        acc = u
        for back in range(1, w):
            acc = acc + ext_ref[POOL_HALO - back:POOL_HALO - back + tm, cols]
        cnt = jnp.minimum(pos + 1, w).astype(F32)
        d = acc / cnt - u
        o_ref[:, cols] = _bdot(d, w_ref[gi]) * sc_ref[:, cols]


def _pool_prompt(z, pool_w, pool_scale, tm):
    groups, t, _ = z.shape
    ucol = U_OFF // POOL_WIDTH
    per = tm // POOL_HALO
    return pl.pallas_call(
        _pool_prompt_kernel,
        grid=(groups, t // tm),
        in_specs=[pl.BlockSpec((None, tm, POOL_WIDTH), lambda g, i: (g, i, ucol)),
                  pl.BlockSpec((None, POOL_HALO, POOL_WIDTH), lambda g, i: (g, jnp.maximum(i * per - 1, 0), ucol)),
                  pl.BlockSpec((len(POOL_WINDOWS), POOL_GROUP_WIDTH, POOL_GROUP_WIDTH), lambda g, i: (0, 0, 0)),
                  pl.BlockSpec((1, POOL_WIDTH), lambda g, i: (0, 0))],
        out_specs=pl.BlockSpec((None, tm, POOL_WIDTH), lambda g, i: (g, i, 0)),
        out_shape=jax.ShapeDtypeStruct((groups, t, POOL_WIDTH), F32),
        scratch_shapes=[pltpu.VMEM((tm + POOL_HALO, POOL_WIDTH), F32)],
        compiler_params=_cparams("parallel", "parallel"),
        name="pool_prompt",
    )(z, z, pool_w, pool_scale)


def _pool_sample_kernel(u_ref, st_ref, w_ref, sc_ref, o_ref):
    for gi, w in enumerate(POOL_WINDOWS):
        cols = slice(gi * POOL_GROUP_WIDTH, (gi + 1) * POOL_GROUP_WIDTH)
        u = u_ref[:, cols]
        acc = u
        for back in range(1, w):
            acc = acc + st_ref[POOL_PAD - back, :, cols]
        cnt = float(min(PAST_LEN + 1, w))
        d = acc / cnt - u
        o_ref[:, cols] = _bdot(d, w_ref[gi]) * sc_ref[:, cols]


def _pool_sample(z, state_t, pool_w, pool_scale):
    rows = z.shape[1]
    ucol = U_OFF // POOL_WIDTH
    return pl.pallas_call(
        _pool_sample_kernel,
        grid=(1,),
        in_specs=[pl.BlockSpec((None, rows, POOL_WIDTH), lambda i: (0, 0, ucol)),
                  pl.BlockSpec((POOL_PAD, rows, POOL_WIDTH), lambda i: (0, 0, 0)),
                  pl.BlockSpec((len(POOL_WINDOWS), POOL_GROUP_WIDTH, POOL_GROUP_WIDTH), lambda i: (0, 0, 0)),
                  pl.BlockSpec((1, POOL_WIDTH), lambda i: (0, 0))],
        out_specs=pl.BlockSpec((None, rows, POOL_WIDTH), lambda i: (0, 0, 0)),
        out_shape=jax.ShapeDtypeStruct((1, rows, POOL_WIDTH), F32),
        compiler_params=_cparams("arbitrary"),
        name="pool_sample",
    )(z, state_t, pool_w, pool_scale)


PREP_HALO = 8


def _rwkv_prep_kernel(*refs, halo, has_vres):
    it = iter(refs)
    pr_ref, prev_ref = next(it), next(it)
    vf_ref = next(it) if has_vres else None
    mu_ref, w0_ref, a0_ref, kk_ref, ka_ref, rk_ref = (next(it) for _ in range(6))
    w2_ref, a2_ref, g2_ref, ones_ref = (next(it) for _ in range(4))
    if has_vres:
        vw1_ref, vw2_ref, vb_ref = next(it), next(it), next(it)
    r_o, w_o, k_o, v_o, a_o, b_o, g_o, bonus_o = (next(it) for _ in range(8))
    ext_ref = next(it) if halo else None

    pr = pr_ref[...]
    tm = pr.shape[0]
    if halo:
        i = pl.program_id(1)
        ext_ref[0:PREP_HALO, :] = jnp.where(i > 0, prev_ref[...], 0.0)
        ext_ref[PREP_HALO:PREP_HALO + tm, :] = pr
        prev = ext_ref[PREP_HALO - 1:PREP_HALO - 1 + tm, :]
    else:
        prev = prev_ref[...]
    xm = pr + (prev - pr) * mu_ref[...]
    wd = RWKV_WIDTH
    r, k, v, lora_in = xm[:, 0:wd], xm[:, wd:2 * wd], xm[:, 2 * wd:3 * wd], xm[:, 3 * wd:4 * wd]
    ones_bd = ones_ref[...]
    lw = _bdot(jnp.tanh(lora_in), w2_ref[...])
    la = _bdot(lora_in, a2_ref[...])
    g = _bdot(_sigmoid(lora_in), g2_ref[...])
    y = -(w0_ref[...] + lw)
    softplus = jnp.maximum(y, 0.0) + jnp.log(1.0 + jnp.exp(-jnp.abs(y)))
    decay = jnp.exp(-jnp.exp(-softplus - 0.5))
    a = _sigmoid(a0_ref[...] + la)
    if has_vres:
        t2 = _bdot(_bdot(v, vw1_ref[...]), vw2_ref[...])
        v = v + (vf_ref[...] - v) * _sigmoid(vb_ref[...] + t2)
    kk = k * kk_ref[...]
    kk = kk / jnp.maximum(jnp.sqrt(_seg_sum(kk * kk, ones_bd)), 1e-12)
    k = k * (1.0 + (a - 1.0) * ka_ref[...])
    r_o[...] = r
    w_o[...] = decay
    k_o[...] = k
    v_o[...] = v
    a_o[...] = -kk
    b_o[...] = kk * a
    g_o[...] = g
    bonus_o[...] = _seg_sum(r * k * rk_ref[...], ones_bd) * v


def _rwkv_prep(z, prev, v_first, rp, tm):
    groups, t, _ = z.shape
    halo = prev is None
    has_vres = v_first is not None
    prcol = PR_OFF // RWKV_PROJ_PAD
    wd = RWKV_WIDTH
    tok = lambda width: pl.BlockSpec((None, tm, width), lambda g, i: (g, i, 0))
    vec = lambda width: pl.BlockSpec((1, width), lambda g, i: (0, 0))
    mat = lambda a, b: pl.BlockSpec((a, b), lambda g, i: (0, 0))
    args = [z]
    in_specs = [pl.BlockSpec((None, tm, RWKV_PROJ_PAD), lambda g, i: (g, i, prcol))]
    if halo:
        per = tm // PREP_HALO
        args.append(z)
        in_specs.append(pl.BlockSpec((None, PREP_HALO, RWKV_PROJ_PAD),
                                     lambda g, i: (g, jnp.maximum(i * per - 1, 0), prcol)))
    else:
        args.append(prev)
        in_specs.append(tok(RWKV_PROJ_PAD))
    if has_vres:
        args.append(v_first)
        in_specs.append(tok(wd))
    args += [rp["mu"], rp["w0"], rp["a0"], rp["k_k"], rp["k_a"], rp["r_k"],
             rp["w2"], rp["a2"], rp["g2"], rp["ones_bd"]]
    in_specs += [vec(RWKV_PROJ_PAD)] + [vec(wd)] * 5 + [mat(wd, wd)] * 4
    if has_vres:
        args += [rp["vw1"], rp["vw2"], rp["vb"]]
        in_specs += [mat(wd, LANES), mat(LANES, wd), vec(wd)]
    return pl.pallas_call(
        functools.partial(_rwkv_prep_kernel, halo=halo, has_vres=has_vres),
        grid=(groups, t // tm),
        in_specs=in_specs,
        out_specs=[tok(wd)] * 8,
        out_shape=[jax.ShapeDtypeStruct((groups, t, wd), F32)] * 8,
        scratch_shapes=[pltpu.VMEM((tm + PREP_HALO, RWKV_PROJ_PAD), F32)] if halo else [],
        compiler_params=_cparams("parallel", "parallel"),
        name="rwkv_prep",
    )(*args)


def _wkv_scan_kernel(r_ref, w_ref, k_ref, v_ref, a_ref, b_ref, y_ref, s_ref):
    nb, tc, _ = r_ref.shape
    pairs = RWKV_HEADS // 2
    n = RWKV_HEAD_DIM

    @pl.when(pl.program_id(0) == 0)
    def _():
        s_ref[...] = jnp.zeros(s_ref.shape, F32)

    row = lax.broadcasted_iota(jnp.int32, (n, 2 * n), 0)
    lane = lax.broadcasted_iota(jnp.int32, (n, 2 * n), 1)
    first = lane < n
    diag_lo = first & (lane == row)
    diag_hi = (~first) & (lane - n == row)
    first_row = lax.broadcasted_iota(jnp.int32, (1, 2 * n), 1) < n

    def seg_rows(x):
        lo = jnp.where(first_row, x, 0.0)
        return lo, x - lo

    def steps(t8, carry):
        base = pl.multiple_of(t8 * SUBLANES, SUBLANES)
        rows = pl.ds(base, SUBLANES)
        for bi in range(nb):
            for p in range(pairs):
                sl = slice(p * 2 * n, (p + 1) * 2 * n)
                r8, w8, k8 = r_ref[bi, rows, sl], w_ref[bi, rows, sl], k_ref[bi, rows, sl]
                v8, a8, b8 = v_ref[bi, rows, sl], a_ref[bi, rows, sl], b_ref[bi, rows, sl]
                s = s_ref[bi * pairs + p]
                y8 = []
                for u in range(SUBLANES):
                    one = slice(u, u + 1)
                    a_lo, a_hi = seg_rows(a8[one])
                    sa = jnp.where(first, jnp.sum(s * a_lo, axis=1, keepdims=True),
                                   jnp.sum(s * a_hi, axis=1, keepdims=True))
                    vcol = jnp.where(first, jnp.sum(jnp.where(diag_lo, v8[one], 0.0), axis=1, keepdims=True),
                                     jnp.sum(jnp.where(diag_hi, v8[one], 0.0), axis=1, keepdims=True))
                    s = s * w8[one] + sa * b8[one] + vcol * k8[one]
                    r_lo, r_hi = seg_rows(r8[one])
                    ycol = jnp.where(first, jnp.sum(s * r_lo, axis=1, keepdims=True),
                                     jnp.sum(s * r_hi, axis=1, keepdims=True))
                    y8.append(jnp.sum(jnp.where(diag_lo | diag_hi, ycol, 0.0), axis=0, keepdims=True))
                s_ref[bi * pairs + p] = s
                y_ref[bi, rows, sl] = jnp.concatenate(y8, axis=0)
        return carry

    lax.fori_loop(0, tc // SUBLANES, steps, 0)


def _wkv_scan(r, w, k, v, a, b, tc):
    nb, t, wd = r.shape
    pairs = RWKV_HEADS // 2
    tok = pl.BlockSpec((nb, tc, wd), lambda i: (0, i, 0))
    return pl.pallas_call(
        _wkv_scan_kernel,
        grid=(t // tc,),
        in_specs=[tok] * 6,
        out_specs=[tok, pl.BlockSpec((nb * pairs, RWKV_HEAD_DIM, 2 * RWKV_HEAD_DIM), lambda i: (0, 0, 0))],
        out_shape=[jax.ShapeDtypeStruct((nb, t, wd), F32),
                   jax.ShapeDtypeStruct((nb * pairs, RWKV_HEAD_DIM, 2 * RWKV_HEAD_DIM), F32)],
        compiler_params=_cparams("arbitrary"),
        name="wkv_scan",
    )(r, w, k, v, a, b)


def _wkv_step_kernel(r_ref, w_ref, k_ref, v_ref, a_ref, b_ref, s_ref, y_ref, so_ref):
    heads = s_ref.shape[0]
    n = RWKV_HEAD_DIM
    diag = lax.broadcasted_iota(jnp.int32, (n, n), 0) == lax.broadcasted_iota(jnp.int32, (n, n), 1)

    def body(h8, carry):
        base = pl.multiple_of(h8 * SUBLANES, SUBLANES)
        rows = pl.ds(base, SUBLANES)
        r8, w8, k8, v8, a8, b8 = (ref[rows, :] for ref in (r_ref, w_ref, k_ref, v_ref, a_ref, b_ref))
        y8 = []
        for u in range(SUBLANES):
            one = slice(u, u + 1)
            s = s_ref[base + u]
            sa = jnp.sum(s * a8[one], axis=1, keepdims=True)
            vcol = jnp.sum(jnp.where(diag, v8[one], 0.0), axis=1, keepdims=True)
            s = s * w8[one] + sa * b8[one] + vcol * k8[one]
            so_ref[base + u] = s
            ycol = jnp.sum(s * r8[one], axis=1, keepdims=True)
            y8.append(jnp.sum(jnp.where(diag, ycol, 0.0), axis=0, keepdims=True))
        y_ref[rows, :] = jnp.concatenate(y8, axis=0)
        return carry

    lax.fori_loop(0, heads // SUBLANES, body, 0)


def _wkv_step(r, w, k, v, a, b, state):
    heads = state.shape[0]
    hb = _row_tile(heads, 128)
    n = RWKV_HEAD_DIM
    vec = pl.BlockSpec((hb, n), lambda i: (i, 0))
    st = pl.BlockSpec((hb, n, n), lambda i: (i, 0, 0))
    return pl.pallas_call(
        _wkv_step_kernel,
        grid=(heads // hb,),
        in_specs=[vec] * 6 + [st],
        out_specs=[vec, st],
        out_shape=[jax.ShapeDtypeStruct((heads, n), F32), jax.ShapeDtypeStruct((heads, n, n), F32)],
        compiler_params=_cparams("parallel"),
        name="wkv_step",
    )(r, w, k, v, a, b, state)


def _rwkv_post_kernel(y_ref, bonus_ref, g_ref, lg_ref, lb_ref, ones_ref, o_ref):
    y = y_ref[...]
    ones_bd = ones_ref[...]
    inv = 1.0 / RWKV_HEAD_DIM
    d = y - _seg_sum(y, ones_bd) * inv
    var = _seg_sum(d * d, ones_bd) * inv
    yn = d * lax.rsqrt(var + LNX_EPS) * lg_ref[...] + lb_ref[...]
    o_ref[...] = ((yn + bonus_ref[...]) * g_ref[...]).astype(o_ref.dtype)


def _rwkv_post(y, bonus, g, rp, tm):
    groups, t, wd = y.shape
    tok = pl.BlockSpec((None, tm, wd), lambda gi, i: (gi, i, 0))
    vec = pl.BlockSpec((1, wd), lambda gi, i: (0, 0))
    return pl.pallas_call(
        _rwkv_post_kernel,
        grid=(groups, t // tm),
        in_specs=[tok, tok, tok, vec, vec, pl.BlockSpec((wd, wd), lambda gi, i: (0, 0))],
        out_specs=tok,
        out_shape=jax.ShapeDtypeStruct((groups, t, wd), BF16),
        compiler_params=_cparams("parallel", "parallel"),
        name="rwkv_post",
    )(y, bonus, g, rp["lnx_g"], rp["lnx_b"], rp["ones_bd"])


MERGE_TN = 512


def _merge_kernel(oa_ref, zp_ref, yr_ref, ga_ref, gp_ref, gr_ref, wa_ref, wp_ref, wr_ref, o_ref):
    merged = (_sigmoid(ga_ref[...]) * _bdot(oa_ref[...], wa_ref[...])
              + _sigmoid(gp_ref[...]) * _bdot(zp_ref[...], wp_ref[...])
              + _sigmoid(gr_ref[...]) * _bdot(yr_ref[...], wr_ref[...]))
    o_ref[...] = merged.astype(o_ref.dtype)


def _merge(o_attn, z_pool, y_rwkv, z, w_attn_out, w_pool_out, w_rwkv_out, tm):
    groups, t, _ = z.shape
    tn = MERGE_TN
    gate = lambda br: pl.BlockSpec((None, tm, tn), lambda g, i, j: (g, i, (GL_OFF + br * D_MODEL) // tn + j))
    tok = lambda width: pl.BlockSpec((None, tm, width), lambda g, i, j: (g, i, 0))
    wt = lambda rows: pl.BlockSpec((rows, tn), lambda g, i, j: (0, j))
    return pl.pallas_call(
        _merge_kernel,
        grid=(groups, t // tm, D_MODEL // tn),
        in_specs=[tok(ATTN_WIDTH), tok(POOL_WIDTH), tok(RWKV_WIDTH), gate(0), gate(1), gate(2),
                  wt(ATTN_WIDTH), wt(POOL_WIDTH), wt(RWKV_WIDTH)],
        out_specs=pl.BlockSpec((None, tm, tn), lambda g, i, j: (g, i, j)),
        out_shape=jax.ShapeDtypeStruct((groups, t, D_MODEL), BF16),
        compiler_params=_cparams("parallel", "parallel", "arbitrary"),
        name="merge",
    )(o_attn, z_pool, y_rwkv, z, z, z, w_attn_out, w_pool_out, w_rwkv_out)


def _proj_residual_kernel(a_ref, w_ref, x_ref, gt_ref, o_ref):
    o_ref[...] = x_ref[...] + (1.0 + gt_ref[...]) * _bdot(a_ref[...], w_ref[...])


def _proj_residual(a, w, x, mod, chunk, tm, tn):
    groups, t, kdim = a.shape
    per_row = mod.shape[1] != 1
    gate = pl.BlockSpec((None, tm if per_row else 1, tn),
                        lambda g, i, j: (g, i if per_row else 0, chunk * (D_MODEL // tn) + j))
    return pl.pallas_call(
        _proj_residual_kernel,
        grid=(groups, t // tm, D_MODEL // tn),
        in_specs=[pl.BlockSpec((None, tm, kdim), lambda g, i, j: (g, i, 0)),
                  pl.BlockSpec((kdim, tn), lambda g, i, j: (0, j)),
                  pl.BlockSpec((None, tm, tn), lambda g, i, j: (g, i, j)),
                  gate],
        out_specs=pl.BlockSpec((None, tm, tn), lambda g, i, j: (g, i, j)),
        out_shape=jax.ShapeDtypeStruct((groups, t, D_MODEL), F32),
        compiler_params=_cparams("parallel", "parallel", "arbitrary"),
        name="proj_residual",
    )(a, w, x, mod)


FFN_TN = 256


def _ffn_up_kernel(x_ref, g_ref, sh_ref, sc_ref, wg_ref, wu_ref, o_ref, h_ref):
    @pl.when(pl.program_id(2) == 0)
    def _():
        h_ref[...] = _mod_norm(x_ref[...], g_ref[...], sh_ref[...], sc_ref[...]).astype(BF16)

    h = h_ref[...]
    gate = jnp.dot(h, wg_ref[...].astype(BF16), preferred_element_type=F32)
    up = jnp.dot(h, wu_ref[...].astype(BF16), preferred_element_type=F32)
    o_ref[...] = (gate * _sigmoid(gate) * up).astype(o_ref.dtype)


def _ffn_up(x, g2, mod, w_gate, w_up, tm):
    groups, t, _ = x.shape
    tn = FFN_TN
    wt = pl.BlockSpec((D_MODEL, tn), lambda g, i, j: (0, j))
    return pl.pallas_call(
        _ffn_up_kernel,
        grid=(groups, t // tm, D_FF // tn),
        in_specs=[pl.BlockSpec((None, tm, D_MODEL), lambda g, i, j: (g, i, 0)),
                  pl.BlockSpec((1, D_MODEL), lambda g, i, j: (0, 0)),
                  _mod_spec(mod, tm, 3), _mod_spec(mod, tm, 4), wt, wt],
        out_specs=pl.BlockSpec((None, tm, tn), lambda g, i, j: (g, i, j)),
        out_shape=jax.ShapeDtypeStruct((groups, t, D_FF), BF16),
        scratch_shapes=[pltpu.VMEM((tm, D_MODEL), BF16)],
        compiler_params=_cparams("parallel", "parallel", "arbitrary"),
        name="ffn_up",
    )(x, g2, mod, mod, w_gate, w_up)


def _final_norm_kernel(x_ref, g_ref, o_ref):
    x = x_ref[...]
    ms = jnp.mean(x * x, axis=-1, keepdims=True)
    o_ref[...] = x * lax.rsqrt(ms + RMS_EPS) * g_ref[...]


def _final_norm(x, g, tm):
    groups, t, _ = x.shape
    tok = pl.BlockSpec((None, tm, D_MODEL), lambda gi, i: (gi, i, 0))
    return pl.pallas_call(
        _final_norm_kernel,
        grid=(groups, t // tm),
        in_specs=[tok, pl.BlockSpec((1, D_MODEL), lambda gi, i: (0, 0))],
        out_specs=tok,
        out_shape=jax.ShapeDtypeStruct(x.shape, F32),
        compiler_params=_cparams("parallel", "parallel"),
        name="final_norm",
    )(x, g)


def _pad_cols(a, width):
    return jnp.pad(a, [(0, 0)] * (a.ndim - 1) + [(0, width - a.shape[-1])])


def _rwkv_params(l, rwkv_mu, rwkv_w0, rwkv_w2, rwkv_a0, rwkv_a2, rwkv_g2, rwkv_k_k, rwkv_k_a, rwkv_r_k,
                 rwkv_lnx_g, rwkv_lnx_b, vres_w1, vres_w2, vres_b):
    wd = RWKV_WIDTH
    row = lambda a: a.reshape(1, -1)
    lora_rows = lambda w, off: jnp.zeros((wd, wd), F32).at[off:off + w.shape[0]].set(w).astype(BF16)
    seg = np.arange(wd) // RWKV_HEAD_DIM
    rp = dict(
        mu=_pad_cols(row(rwkv_mu[l]), RWKV_PROJ_PAD), w0=row(rwkv_w0[l]), a0=row(rwkv_a0[l]),
        k_k=row(rwkv_k_k[l]), k_a=row(rwkv_k_a[l]), r_k=row(rwkv_r_k[l]),
        w2=lora_rows(rwkv_w2[l], 0), a2=lora_rows(rwkv_a2[l], DECAY_LORA),
        g2=lora_rows(rwkv_g2[l], DECAY_LORA + ICLR_LORA),
        ones_bd=jnp.asarray(seg[:, None] == seg[None, :], BF16),
        lnx_g=row(rwkv_lnx_g[l]), lnx_b=row(rwkv_lnx_b[l]))
    if l > 0:
        rp.update(vw1=_pad_cols(vres_w1[l - 1], LANES).astype(BF16),
                  vw2=jnp.pad(vres_w2[l - 1], ((0, LANES - VRES_LORA), (0, 0))).astype(BF16),
                  vb=row(vres_b[l - 1]))
    return rp


def _layer_common(x, mod, z, o_attn, z_pool, y, bonus, g, lw, rp, tm):
    y_rwkv = _rwkv_post(y, bonus, g, rp, _row_tile(x.shape[1], 512))
    merged = _merge(o_attn, z_pool, y_rwkv, z, lw["w_attn_out"], lw["w_pool_out"], lw["w_rwkv_out"], tm)
    x = _proj_residual(merged, lw["w_o"], x, mod, 2, tm, 512)
    hidden = _ffn_up(x, lw["norm2_g"], mod, lw["w_ffn_gate"], lw["w_ffn_up"], tm)
    return _proj_residual(hidden, lw["w_ffn_down"], x, mod, 5, tm, 256)


def kernel(x_prompt, x_sample, c_prompt, c_sample, cache_k_win, cache_v_win, state_pool, state_shift, state_wkv, ada_w, ada_b, norm1_g, norm2_g, w_in, attn_sink, w_attn_out, pool_w, pool_scale, w_pool_out, rwkv_mu, rwkv_w0, rwkv_w2, rwkv_a0, rwkv_a2, rwkv_g2, rwkv_k_k, rwkv_k_a, rwkv_r_k, rwkv_lnx_g, rwkv_lnx_b, w_rwkv_out, vres_w1, vres_w2, vres_b, w_o, w_ffn_gate, w_ffn_up, w_ffn_down, final_norm_g):
    depth = ada_w.shape[0]
    nb, t, _ = x_prompt.shape
    ns = x_sample.shape[0]
    assert x_sample.shape[1] == 1 and t % WINDOW == 0
    win = cache_k_win.shape[2]
    assert win <= WINDOW
    heads, n = RWKV_HEADS, RWKV_HEAD_DIM

    pad_rows = -nb % SUBLANES
    c_all = jnp.concatenate([c_prompt, jnp.zeros((pad_rows, D_MODEL), F32), c_sample], axis=0)
    mod_all = _ada(c_all, ada_w, ada_b)

    cos_p, sin_p = _rope_tables(jnp.arange(t))
    cos_s, sin_s = _rope_tables(jnp.full((ns,), PAST_LEN))

    xp = x_prompt
    xs = x_sample.reshape(1, ns, D_MODEL)
    tm_p = _row_tile(t, 1024)
    tm_s = _row_tile(ns, 1024)
    vf_p = vf_s = None
    st_p, st_s = [], []
    for l in range(depth):
        lw = dict(norm2_g=norm2_g[l].reshape(1, -1), w_attn_out=w_attn_out[l], w_pool_out=w_pool_out[l],
                  w_rwkv_out=w_rwkv_out[l], w_o=w_o[l], w_ffn_gate=w_ffn_gate[l], w_ffn_up=w_ffn_up[l],
                  w_ffn_down=w_ffn_down[l])
        rp = _rwkv_params(l, rwkv_mu, rwkv_w0, rwkv_w2, rwkv_a0, rwkv_a2, rwkv_g2, rwkv_k_k, rwkv_k_a, rwkv_r_k,
                          rwkv_lnx_g, rwkv_lnx_b, vres_w1, vres_w2, vres_b)
        g1 = norm1_g[l].reshape(1, -1)
        w_pad = jnp.concatenate([w_in[l][:, :PR_OFF + RWKV_PROJ],
                                 jnp.zeros((D_MODEL, RWKV_PROJ_PAD - RWKV_PROJ), F32),
                                 w_in[l][:, PR_OFF + RWKV_PROJ:]], axis=1)
        mod_p = mod_all[l, :nb].reshape(nb, 1, -1)
        mod_s = mod_all[l, nb + pad_rows:].reshape(1, ns, -1)

        z = _in_proj(xp, g1, mod_p, w_pad, cos_p, sin_p, tm_p)
        o_attn = _attn_prompt(z, attn_sink[l])
        z_pool = _pool_prompt(z, pool_w[l], pool_scale[l].reshape(1, -1), _row_tile(t, 512))
        r, w, k, v, a, b, g, bonus = _rwkv_prep(z, None, vf_p, rp, _row_tile(t, 256))
        if l == 0:
            vf_p = v
        y, s_pair = _wkv_scan(r, w, k, v, a, b, _row_tile(t, 256))
        wkv_new = s_pair.reshape(nb, heads // 2, n, 2, n).transpose(0, 1, 3, 2, 4).reshape(nb, heads, n, n)
        st_p.append((z[:, t - WINDOW:, K_OFF:V_OFF].reshape(nb, WINDOW, N_KV_HEADS, HEAD_DIM),
                     z[:, t - WINDOW:, V_OFF:U_OFF].reshape(nb, WINDOW, N_KV_HEADS, HEAD_DIM),
                     z[:, t - POOL_PAD:, U_OFF:PR_OFF],
                     z[:, t - 1, PR_OFF:PR_OFF + RWKV_PROJ],
                     wkv_new))
        xp = _layer_common(xp, mod_p, z, o_attn, z_pool, y, bonus, g, lw, rp, tm_p)

        z = _in_proj(xs, g1, mod_s, w_pad, cos_s, sin_s, tm_s)
        k_new, v_new = z[0, :, K_OFF:V_OFF], z[0, :, V_OFF:U_OFF]
        k_buf = cache_k_win[l].reshape(ns, win, KV_WIDTH)
        v_buf = cache_v_win[l].reshape(ns, win, KV_WIDTH)
        o3 = _attn_sample(z[0, :, :ATTN_WIDTH].reshape(ns, N_Q_HEADS, HEAD_DIM), k_new.reshape(ns, 1, KV_WIDTH),
                          v_new.reshape(ns, 1, KV_WIDTH), k_buf, v_buf, attn_sink[l])
        o_attn = o3.reshape(1, ns, ATTN_WIDTH).astype(BF16)
        z_pool = _pool_sample(z, state_pool[l].transpose(1, 0, 2), pool_w[l], pool_scale[l].reshape(1, -1))
        prev = _pad_cols(state_shift[l], RWKV_PROJ_PAD).reshape(1, ns, RWKV_PROJ_PAD)
        r, w, k, v, a, b, g, bonus = _rwkv_prep(z, prev, vf_s, rp, tm_s)
        if l == 0:
            vf_s = v
        per_head = lambda arr: arr.reshape(ns * heads, n)
        y, wkv_new = _wkv_step(*(per_head(arr) for arr in (r, w, k, v, a, b)),
                               state_wkv[l].reshape(ns * heads, n, n))
        u_new = z[0, :, None, U_OFF:PR_OFF]
        st_s.append((jnp.concatenate([k_buf[:, 1:], k_new[:, None]], axis=1).reshape(ns, win, N_KV_HEADS, HEAD_DIM),
                     jnp.concatenate([v_buf[:, 1:], v_new[:, None]], axis=1).reshape(ns, win, N_KV_HEADS, HEAD_DIM),
                     jnp.concatenate([state_pool[l][:, 1:], u_new], axis=1),
                     z[0, :, PR_OFF:PR_OFF + RWKV_PROJ],
                     wkv_new.reshape(ns, heads, n, n)))
        xs = _layer_common(xs, mod_s, z, o_attn, z_pool, y.reshape(1, ns, RWKV_WIDTH), bonus, g, lw, rp, tm_s)

    g_fin = final_norm_g.reshape(1, -1)
    y_prompt = _final_norm(xp, g_fin, tm_p)
    y_sample = _final_norm(xs, g_fin, tm_s).reshape(ns, 1, D_MODEL)
    stack = lambda states, i: jnp.stack([s[i] for s in states])
    return (y_prompt, y_sample) + tuple(stack(st_p, i) for i in range(5)) + tuple(stack(st_s, i) for i in range(5))
```

```python
import functools

import numpy as np
import jax
import jax.numpy as jnp
from jax import lax
from jax.experimental import pallas as pl
from jax.experimental.pallas import tpu as pltpu

F32 = jnp.float32
BF16 = jnp.bfloat16

D_MODEL = 2048
HEAD_DIM = 64
N_Q_HEADS = 16
N_KV_HEADS = 4
GROUP = N_Q_HEADS // N_KV_HEADS
ATTN_WIDTH = N_Q_HEADS * HEAD_DIM
KV_WIDTH = N_KV_HEADS * HEAD_DIM
WINDOW = 128
ROPE_THETA = 10000.0
POOL_WIDTH = 512
POOL_WINDOWS = (2, 4, 8, 16)
POOL_GROUP_WIDTH = 128
POOL_PAD = 15
RWKV_HEAD_DIM = 64
RWKV_WIDTH = 512
RWKV_HEADS = 8
DECAY_LORA = 96
ICLR_LORA = 96
GATE_LORA = 256
VRES_LORA = 32
RWKV_PROJ = 3 * RWKV_WIDTH + DECAY_LORA + ICLR_LORA + GATE_LORA
RWKV_PROJ_PAD = 2048
D_FF = 5632
PAST_LEN = 8192
RMS_EPS = 1e-6
LNX_EPS = 64e-5
NEG_INF = -1e30

Q_OFF = 0
K_OFF = ATTN_WIDTH
V_OFF = K_OFF + KV_WIDTH
U_OFF = V_OFF + KV_WIDTH
PR_OFF = U_OFF + POOL_WIDTH
GL_OFF = PR_OFF + RWKV_PROJ_PAD
IN_PAD = GL_OFF + 3 * D_MODEL
ROPE_END = V_OFF

LANES = 128
SUBLANES = 8
VMEM_LIMIT = 56 * 2**20


def _cparams(*sem):
    return pltpu.CompilerParams(dimension_semantics=sem, vmem_limit_bytes=VMEM_LIMIT)


def _row_tile(t, pref):
    tm = min(t, pref)
    while t % tm:
        tm -= SUBLANES
    return tm


def _sigmoid(x):
    return 1.0 / (1.0 + jnp.exp(-x))


def _mod_norm(x, g, sh, sc):
    ms = jnp.mean(x * x, axis=-1, keepdims=True)
    return (x * lax.rsqrt(ms + RMS_EPS) * g) * (1.0 + sc) + sh


def _bdot(a, b):
    return jnp.dot(a.astype(BF16), b.astype(BF16), preferred_element_type=F32)


def _seg_sum(x, ones_bd):
    hi = x.astype(BF16)
    lo = (x - hi.astype(F32)).astype(BF16)
    return (jnp.dot(hi, ones_bd, preferred_element_type=F32)
            + jnp.dot(lo, ones_bd, preferred_element_type=F32))


def _mod_spec(mod, tm, chunk):
    per_row = mod.shape[1] != 1
    rows = tm if per_row else 1
    return pl.BlockSpec((None, rows, D_MODEL), lambda g, i, j=0: (g, i if per_row else 0, chunk))


def _ada_kernel(c_ref, w_ref, b_ref, o_ref):
    c = c_ref[...]
    o_ref[...] = _bdot(c * _sigmoid(c), w_ref[...]) + b_ref[...]


def _ada(c_all, ada_w, ada_b):
    depth, _, width = ada_w.shape
    rows = c_all.shape[0]
    tn = 1024
    return pl.pallas_call(
        _ada_kernel,
        grid=(depth, width // tn),
        in_specs=[pl.BlockSpec((rows, D_MODEL), lambda l, j: (0, 0)),
                  pl.BlockSpec((None, D_MODEL, tn), lambda l, j: (l, 0, j)),
                  pl.BlockSpec((None, 1, tn), lambda l, j: (l, 0, j))],
        out_specs=pl.BlockSpec((None, rows, tn), lambda l, j: (l, 0, j)),
        out_shape=jax.ShapeDtypeStruct((depth, rows, width), F32),
        compiler_params=_cparams("parallel", "parallel"),
        name="ada",
    )(c_all, ada_w, ada_b.reshape(depth, 1, width))


IN_TN = 512


def _in_proj_kernel(x_ref, g_ref, sh_ref, sc_ref, cos_ref, sin_ref, w_ref, o_ref, h_ref):
    j = pl.program_id(2)

    @pl.when(j == 0)
    def _():
        h_ref[...] = _mod_norm(x_ref[...], g_ref[...], sh_ref[...], sc_ref[...]).astype(BF16)

    acc = jnp.dot(h_ref[...], w_ref[...].astype(BF16), preferred_element_type=F32)

    @pl.when(j * IN_TN < ROPE_END)
    def _():
        reps = IN_TN // LANES
        cos = jnp.concatenate([cos_ref[...]] * reps, axis=1)
        sin = jnp.concatenate([sin_ref[...]] * reps, axis=1)
        lane = lax.broadcasted_iota(jnp.int32, acc.shape, 1)
        first_half = (lane % HEAD_DIM) < (HEAD_DIM // 2)
        rot = jnp.where(first_half, pltpu.roll(acc, IN_TN - HEAD_DIM // 2, 1),
                        pltpu.roll(acc, HEAD_DIM // 2, 1))
        roped = acc * cos + rot * sin
        o_ref[...] = jnp.where(j * IN_TN + lane < ROPE_END, roped, acc)

    @pl.when(j * IN_TN >= ROPE_END)
    def _():
        o_ref[...] = acc


def _in_proj(x, g1, mod, w_pad, cos, sin, tm):
    groups, t, _ = x.shape
    return pl.pallas_call(
        _in_proj_kernel,
        grid=(groups, t // tm, IN_PAD // IN_TN),
        in_specs=[pl.BlockSpec((None, tm, D_MODEL), lambda g, i, j: (g, i, 0)),
                  pl.BlockSpec((1, D_MODEL), lambda g, i, j: (0, 0)),
                  _mod_spec(mod, tm, 0), _mod_spec(mod, tm, 1),
                  pl.BlockSpec((tm, LANES), lambda g, i, j: (i, 0)),
                  pl.BlockSpec((tm, LANES), lambda g, i, j: (i, 0)),
                  pl.BlockSpec((D_MODEL, IN_TN), lambda g, i, j: (0, j))],
        out_specs=pl.BlockSpec((None, tm, IN_TN), lambda g, i, j: (g, i, j)),
        out_shape=jax.ShapeDtypeStruct((groups, t, IN_PAD), F32),
        scratch_shapes=[pltpu.VMEM((tm, D_MODEL), BF16)],
        compiler_params=_cparams("parallel", "parallel", "arbitrary"),
        name="in_proj",
    )(x, g1, mod, mod, cos, sin, w_pad)


def _rope_tables(pos):
    half = HEAD_DIM // 2
    inv = ROPE_THETA ** (-jnp.arange(half, dtype=F32) * 2.0 / HEAD_DIM)
    ang = pos.astype(F32)[:, None] * inv[None, :]
    cos, sin = jnp.cos(ang), jnp.sin(ang)
    reps = LANES // HEAD_DIM
    cos_t = jnp.concatenate([cos, cos] * reps, axis=1)
    sin_t = jnp.concatenate([-sin, sin] * reps, axis=1)
    return cos_t, sin_t


def _attn_prompt_kernel(sink_ref, q_ref, k_ref, v_ref, kp_ref, vp_ref, o_ref):
    n = pl.program_id(1)
    blk = q_ref.shape[0]
    q = q_ref[...]
    k2 = jnp.concatenate([kp_ref[...], k_ref[...]], axis=0).astype(BF16)
    v2 = jnp.concatenate([vp_ref[...], v_ref[...]], axis=0).astype(BF16)
    qi = lax.broadcasted_iota(jnp.int32, (blk, 2 * blk), 0)
    kc = lax.broadcasted_iota(jnp.int32, (blk, 2 * blk), 1)
    rel = qi - kc + blk
    first_key = jnp.where(n > 0, 0, blk)
    mask = (rel >= 0) & (rel <= WINDOW) & (kc >= first_key)
    outs = []
    for hd in range(N_Q_HEADS):
        h = hd // GROUP
        kh = k2[:, h * HEAD_DIM:(h + 1) * HEAD_DIM]
        vh = v2[:, h * HEAD_DIM:(h + 1) * HEAD_DIM]
        qh = q[:, hd * HEAD_DIM:(hd + 1) * HEAD_DIM].astype(BF16)
        s = lax.dot_general(qh, kh, (((1,), (1,)), ((), ())), preferred_element_type=F32) * (HEAD_DIM ** -0.5)
        s = jnp.where(mask, s, NEG_INF)
        sk = sink_ref[hd]
        m = jnp.maximum(jnp.max(s, axis=1, keepdims=True), sk)
        p = jnp.exp(s - m)
        den = jnp.sum(p, axis=1, keepdims=True) + jnp.exp(sk - m)
        outs.append(jnp.dot(p.astype(BF16), vh, preferred_element_type=F32) / den)
    o_ref[...] = jnp.concatenate(outs, axis=1).astype(o_ref.dtype)


def _attn_prompt(z, sink):
    groups, t, _ = z.shape
    blk = WINDOW
    kcol, vcol = K_OFF // KV_WIDTH, V_OFF // KV_WIDTH
    prev = lambda col: (lambda g, n: (g, jnp.maximum(n - 1, 0), col))
    return pl.pallas_call(
        _attn_prompt_kernel,
        grid=(groups, t // blk),
        in_specs=[pl.BlockSpec(memory_space=pltpu.SMEM),
                  pl.BlockSpec((None, blk, ATTN_WIDTH), lambda g, n: (g, n, 0)),
                  pl.BlockSpec((None, blk, KV_WIDTH), lambda g, n: (g, n, kcol)),
                  pl.BlockSpec((None, blk, KV_WIDTH), lambda g, n: (g, n, vcol)),
                  pl.BlockSpec((None, blk, KV_WIDTH), prev(kcol)),
                  pl.BlockSpec((None, blk, KV_WIDTH), prev(vcol))],
        out_specs=pl.BlockSpec((None, blk, ATTN_WIDTH), lambda g, n: (g, n, 0)),
        out_shape=jax.ShapeDtypeStruct((groups, t, ATTN_WIDTH), BF16),
        compiler_params=_cparams("parallel", "parallel"),
        name="attn_prompt",
    )(sink, z, z, z, z, z)


def _attn_sample_kernel(sink_ref, q_ref, kn_ref, vn_ref, kb_ref, vb_ref, o_ref):
    bb = q_ref.shape[0]
    row = lax.broadcasted_iota(jnp.int32, (N_Q_HEADS, KV_WIDTH), 0)
    lane = lax.broadcasted_iota(jnp.int32, (N_Q_HEADS, KV_WIDTH), 1)
    own = (row // GROUP) == (lane // HEAD_DIM)
    sink = sink_ref[...]
    scale = HEAD_DIM ** -0.5

    def body(b, carry):
        q = q_ref[b]
        qbd = jnp.where(own, jnp.concatenate([q] * N_KV_HEADS, axis=1), 0.0)
        s = lax.dot_general(qbd.astype(BF16), kb_ref[b].astype(BF16), (((1,), (1,)), ((), ())),
                            preferred_element_type=F32) * scale
        s_new = jnp.sum(qbd * kn_ref[b], axis=1, keepdims=True) * scale
        m = jnp.maximum(jnp.maximum(jnp.max(s, axis=1, keepdims=True), s_new), sink)
        p = jnp.exp(s - m)
        p_new = jnp.exp(s_new - m)
        den = jnp.sum(p, axis=1, keepdims=True) + p_new + jnp.exp(sink - m)
        o = jnp.dot(p.astype(BF16), vb_ref[b].astype(BF16), preferred_element_type=F32) + p_new * vn_ref[b]
        o = jnp.where(own, o, 0.0)
        o64 = o[:, 0:HEAD_DIM]
        for h in range(1, N_KV_HEADS):
            o64 = o64 + o[:, h * HEAD_DIM:(h + 1) * HEAD_DIM]
        o_ref[b] = o64 / den
        return carry

    lax.fori_loop(0, bb, body, 0)


def _attn_sample(q3, k_new, v_new, k_buf, v_buf, sink):
    nb, win = k_buf.shape[0], k_buf.shape[1]
    bb = _row_tile(nb, 16)
    return pl.pallas_call(
        _attn_sample_kernel,
        grid=(nb // bb,),
        in_specs=[pl.BlockSpec((N_Q_HEADS, 1), lambda i: (0, 0)),
                  pl.BlockSpec((bb, N_Q_HEADS, HEAD_DIM), lambda i: (i, 0, 0)),
                  pl.BlockSpec((bb, 1, KV_WIDTH), lambda i: (i, 0, 0)),
                  pl.BlockSpec((bb, 1, KV_WIDTH), lambda i: (i, 0, 0)),
                  pl.BlockSpec((bb, win, KV_WIDTH), lambda i: (i, 0, 0)),
                  pl.BlockSpec((bb, win, KV_WIDTH), lambda i: (i, 0, 0))],
        out_specs=pl.BlockSpec((bb, N_Q_HEADS, HEAD_DIM), lambda i: (i, 0, 0)),
        out_shape=jax.ShapeDtypeStruct((nb, N_Q_HEADS, HEAD_DIM), F32),
        compiler_params=_cparams("parallel"),
        name="attn_sample",
    )(sink.reshape(N_Q_HEADS, 1), q3, k_new, v_new, k_buf, v_buf)


POOL_HALO = 16


def _pool_prompt_kernel(u_ref, halo_ref, w_ref, sc_ref, o_ref, ext_ref):
    i = pl.program_id(1)
    tm = u_ref.shape[0]
    ext_ref[0:POOL_HALO, :] = jnp.where(i > 0, halo_ref[...], 0.0)
    ext_ref[POOL_HALO:POOL_HALO + tm, :] = u_ref[...]
    pos = i * tm + lax.broadcasted_iota(jnp.int32, (tm, 1), 0)
    for gi, w in enumerate(POOL_WINDOWS):
        cols = slice(gi * POOL_GROUP_WIDTH, (gi + 1) * POOL_GROUP_WIDTH)
        u = ext_ref[POOL_HALO:POOL_HALO + tm, cols]
        acc = u
        for back in range(1, w):
            acc = acc + ext_ref[POOL_HALO - back:POOL_HALO - back + tm, cols]
        cnt = jnp.minimum(pos + 1, w).astype(F32)
        d = acc / cnt - u
        o_ref[:, cols] = _bdot(d, w_ref[gi]) * sc_ref[:, cols]


def _pool_prompt(z, pool_w, pool_scale, tm):
    groups, t, _ = z.shape
    ucol = U_OFF // POOL_WIDTH
    per = tm // POOL_HALO
    return pl.pallas_call(
        _pool_prompt_kernel,
        grid=(groups, t // tm),
        in_specs=[pl.BlockSpec((None, tm, POOL_WIDTH), lambda g, i: (g, i, ucol)),
                  pl.BlockSpec((None, POOL_HALO, POOL_WIDTH), lambda g, i: (g, jnp.maximum(i * per - 1, 0), ucol)),
                  pl.BlockSpec((len(POOL_WINDOWS), POOL_GROUP_WIDTH, POOL_GROUP_WIDTH), lambda g, i: (0, 0, 0)),
                  pl.BlockSpec((1, POOL_WIDTH), lambda g, i: (0, 0))],
        out_specs=pl.BlockSpec((None, tm, POOL_WIDTH), lambda g, i: (g, i, 0)),
        out_shape=jax.ShapeDtypeStruct((groups, t, POOL_WIDTH), F32),
        scratch_shapes=[pltpu.VMEM((tm + POOL_HALO, POOL_WIDTH), F32)],
        compiler_params=_cparams("parallel", "parallel"),
        name="pool_prompt",
    )(z, z, pool_w, pool_scale)


def _pool_sample_kernel(u_ref, st_ref, w_ref, sc_ref, o_ref):
    for gi, w in enumerate(POOL_WINDOWS):
        cols = slice(gi * POOL_GROUP_WIDTH, (gi + 1) * POOL_GROUP_WIDTH)
        u = u_ref[:, cols]
        acc = u
        for back in range(1, w):
            acc = acc + st_ref[POOL_PAD - back, :, cols]
        cnt = float(min(PAST_LEN + 1, w))
        d = acc / cnt - u
        o_ref[:, cols] = _bdot(d, w_ref[gi]) * sc_ref[:, cols]


def _pool_sample(z, state_t, pool_w, pool_scale):
    rows = z.shape[1]
    ucol = U_OFF // POOL_WIDTH
    return pl.pallas_call(
        _pool_sample_kernel,
        grid=(1,),
        in_specs=[pl.BlockSpec((None, rows, POOL_WIDTH), lambda i: (0, 0, ucol)),
                  pl.BlockSpec((POOL_PAD, rows, POOL_WIDTH), lambda i: (0, 0, 0)),
                  pl.BlockSpec((len(POOL_WINDOWS), POOL_GROUP_WIDTH, POOL_GROUP_WIDTH), lambda i: (0, 0, 0)),
                  pl.BlockSpec((1, POOL_WIDTH), lambda i: (0, 0))],
        out_specs=pl.BlockSpec((None, rows, POOL_WIDTH), lambda i: (0, 0, 0)),
        out_shape=jax.ShapeDtypeStruct((1, rows, POOL_WIDTH), F32),
        compiler_params=_cparams("arbitrary"),
        name="pool_sample",
    )(z, state_t, pool_w, pool_scale)


PREP_HALO = 8


def _rwkv_prep_kernel(*refs, halo, has_vres):
    it = iter(refs)
    pr_ref, prev_ref = next(it), next(it)
    vf_ref = next(it) if has_vres else None
    mu_ref, w0_ref, a0_ref, kk_ref, ka_ref, rk_ref = (next(it) for _ in range(6))
    w2_ref, a2_ref, g2_ref, ones_ref = (next(it) for _ in range(4))
    if has_vres:
        vw1_ref, vw2_ref, vb_ref = next(it), next(it), next(it)
    r_o, w_o, k_o, v_o, a_o, b_o, g_o, bonus_o = (next(it) for _ in range(8))
    ext_ref = next(it) if halo else None

    pr = pr_ref[...]
    tm = pr.shape[0]
    if halo:
        i = pl.program_id(1)
        ext_ref[0:PREP_HALO, :] = jnp.where(i > 0, prev_ref[...], 0.0)
        ext_ref[PREP_HALO:PREP_HALO + tm, :] = pr
        prev = ext_ref[PREP_HALO - 1:PREP_HALO - 1 + tm, :]
    else:
        prev = prev_ref[...]
    xm = pr + (prev - pr) * mu_ref[...]
    wd = RWKV_WIDTH
    r, k, v, lora_in = xm[:, 0:wd], xm[:, wd:2 * wd], xm[:, 2 * wd:3 * wd], xm[:, 3 * wd:4 * wd]
    ones_bd = ones_ref[...]
    lw = _bdot(jnp.tanh(lora_in), w2_ref[...])
    la = _bdot(lora_in, a2_ref[...])
    g = _bdot(_sigmoid(lora_in), g2_ref[...])
    y = -(w0_ref[...] + lw)
    softplus = jnp.maximum(y, 0.0) + jnp.log(1.0 + jnp.exp(-jnp.abs(y)))
    decay = jnp.exp(-jnp.exp(-softplus - 0.5))
    a = _sigmoid(a0_ref[...] + la)
    if has_vres:
        t2 = _bdot(_bdot(v, vw1_ref[...]), vw2_ref[...])
        v = v + (vf_ref[...] - v) * _sigmoid(vb_ref[...] + t2)
    kk = k * kk_ref[...]
    kk = kk / jnp.maximum(jnp.sqrt(_seg_sum(kk * kk, ones_bd)), 1e-12)
    k = k * (1.0 + (a - 1.0) * ka_ref[...])
    r_o[...] = r
    w_o[...] = decay
    k_o[...] = k
    v_o[...] = v
    a_o[...] = -kk
    b_o[...] = kk * a
    g_o[...] = g
    bonus_o[...] = _seg_sum(r * k * rk_ref[...], ones_bd) * v


def _rwkv_prep(z, prev, v_first, rp, tm):
    groups, t, _ = z.shape
    halo = prev is None
    has_vres = v_first is not None
    prcol = PR_OFF // RWKV_PROJ_PAD
    wd = RWKV_WIDTH
    tok = lambda width: pl.BlockSpec((None, tm, width), lambda g, i: (g, i, 0))
    vec = lambda width: pl.BlockSpec((1, width), lambda g, i: (0, 0))
    mat = lambda a, b: pl.BlockSpec((a, b), lambda g, i: (0, 0))
    args = [z]
    in_specs = [pl.BlockSpec((None, tm, RWKV_PROJ_PAD), lambda g, i: (g, i, prcol))]
    if halo:
        per = tm // PREP_HALO
        args.append(z)
        in_specs.append(pl.BlockSpec((None, PREP_HALO, RWKV_PROJ_PAD),
                                     lambda g, i: (g, jnp.maximum(i * per - 1, 0), prcol)))
    else:
        args.append(prev)
        in_specs.append(tok(RWKV_PROJ_PAD))
    if has_vres:
        args.append(v_first)
        in_specs.append(tok(wd))
    args += [rp["mu"], rp["w0"], rp["a0"], rp["k_k"], rp["k_a"], rp["r_k"],
             rp["w2"], rp["a2"], rp["g2"], rp["ones_bd"]]
    in_specs += [vec(RWKV_PROJ_PAD)] + [vec(wd)] * 5 + [mat(wd, wd)] * 4
    if has_vres:
        args += [rp["vw1"], rp["vw2"], rp["vb"]]
        in_specs += [mat(wd, LANES), mat(LANES, wd), vec(wd)]
    return pl.pallas_call(
        functools.partial(_rwkv_prep_kernel, halo=halo, has_vres=has_vres),
        grid=(groups, t // tm),
        in_specs=in_specs,
        out_specs=[tok(wd)] * 8,
        out_shape=[jax.ShapeDtypeStruct((groups, t, wd), F32)] * 8,
        scratch_shapes=[pltpu.VMEM((tm + PREP_HALO, RWKV_PROJ_PAD), F32)] if halo else [],
        compiler_params=_cparams("parallel", "parallel"),
        name="rwkv_prep",
    )(*args)


def _wkv_scan_kernel(r_ref, w_ref, k_ref, v_ref, a_ref, b_ref, y_ref, s_ref):
    nb, tc, _ = r_ref.shape
    pairs = RWKV_HEADS // 2
    n = RWKV_HEAD_DIM

    @pl.when(pl.program_id(0) == 0)
    def _():
        s_ref[...] = jnp.zeros(s_ref.shape, F32)

    row = lax.broadcasted_iota(jnp.int32, (n, 2 * n), 0)
    lane = lax.broadcasted_iota(jnp.int32, (n, 2 * n), 1)
    first = lane < n
    diag_lo = first & (lane == row)
    diag_hi = (~first) & (lane - n == row)
    first_row = lax.broadcasted_iota(jnp.int32, (1, 2 * n), 1) < n

    def seg_rows(x):
        lo = jnp.where(first_row, x, 0.0)
        return lo, x - lo

    def steps(t8, carry):
        base = pl.multiple_of(t8 * SUBLANES, SUBLANES)
        rows = pl.ds(base, SUBLANES)
        chains = [(bi, slice(p * 2 * n, (p + 1) * 2 * n)) for bi in range(nb) for p in range(pairs)]
        tiles = [[ref[bi, rows, sl] for ref in (r_ref, w_ref, k_ref, v_ref, a_ref, b_ref)] for bi, sl in chains]
        s = [s_ref[c] for c in range(len(chains))]
        ys = [[] for _ in chains]
        lane_sum = lambda x: jnp.sum(x, axis=1, keepdims=True)
        for u in range(SUBLANES):
            one = slice(u, u + 1)
            sa, vcol = [], []
            for c, (r8, w8, k8, v8, a8, b8) in enumerate(tiles):
                a_lo, a_hi = seg_rows(a8[one])
                sa.append((lane_sum(s[c] * a_lo), lane_sum(s[c] * a_hi)))
                vcol.append((lane_sum(jnp.where(diag_lo, v8[one], 0.0)), lane_sum(jnp.where(diag_hi, v8[one], 0.0))))
            for c, (r8, w8, k8, v8, a8, b8) in enumerate(tiles):
                s[c] = (s[c] * w8[one] + jnp.where(first, sa[c][0], sa[c][1]) * b8[one]
                        + jnp.where(first, vcol[c][0], vcol[c][1]) * k8[one])
            ycol = []
            for c, (r8, w8, k8, v8, a8, b8) in enumerate(tiles):
                r_lo, r_hi = seg_rows(r8[one])
                ycol.append((lane_sum(s[c] * r_lo), lane_sum(s[c] * r_hi)))
            for c in range(len(chains)):
                y = jnp.where(diag_lo, ycol[c][0], jnp.where(diag_hi, ycol[c][1], 0.0))
                ys[c].append(jnp.sum(y, axis=0, keepdims=True))
        for c, (bi, sl) in enumerate(chains):
            s_ref[c] = s[c]
            y_ref[bi, rows, sl] = jnp.concatenate(ys[c], axis=0)
        return carry

    lax.fori_loop(0, tc // SUBLANES, steps, 0)


def _wkv_scan(r, w, k, v, a, b, tc):
    nb, t, wd = r.shape
    pairs = RWKV_HEADS // 2
    tok = pl.BlockSpec((nb, tc, wd), lambda i: (0, i, 0))
    return pl.pallas_call(
        _wkv_scan_kernel,
        grid=(t // tc,),
        in_specs=[tok] * 6,
        out_specs=[tok, pl.BlockSpec((nb * pairs, RWKV_HEAD_DIM, 2 * RWKV_HEAD_DIM), lambda i: (0, 0, 0))],
        out_shape=[jax.ShapeDtypeStruct((nb, t, wd), F32),
                   jax.ShapeDtypeStruct((nb * pairs, RWKV_HEAD_DIM, 2 * RWKV_HEAD_DIM), F32)],
        compiler_params=_cparams("arbitrary"),
        name="wkv_scan",
    )(r, w, k, v, a, b)


def _wkv_step_kernel(r_ref, w_ref, k_ref, v_ref, a_ref, b_ref, s_ref, y_ref, so_ref):
    heads = s_ref.shape[0]
    n = RWKV_HEAD_DIM
    diag = lax.broadcasted_iota(jnp.int32, (n, n), 0) == lax.broadcasted_iota(jnp.int32, (n, n), 1)

    def body(h8, carry):
        base = pl.multiple_of(h8 * SUBLANES, SUBLANES)
        rows = pl.ds(base, SUBLANES)
        r8, w8, k8, v8, a8, b8 = (ref[rows, :] for ref in (r_ref, w_ref, k_ref, v_ref, a_ref, b_ref))
        y8 = []
        for u in range(SUBLANES):
            one = slice(u, u + 1)
            s = s_ref[base + u]
            sa = jnp.sum(s * a8[one], axis=1, keepdims=True)
            vcol = jnp.sum(jnp.where(diag, v8[one], 0.0), axis=1, keepdims=True)
            s = s * w8[one] + sa * b8[one] + vcol * k8[one]
            so_ref[base + u] = s
            ycol = jnp.sum(s * r8[one], axis=1, keepdims=True)
            y8.append(jnp.sum(jnp.where(diag, ycol, 0.0), axis=0, keepdims=True))
        y_ref[rows, :] = jnp.concatenate(y8, axis=0)
        return carry

    lax.fori_loop(0, heads // SUBLANES, body, 0)


def _wkv_step(r, w, k, v, a, b, state):
    heads = state.shape[0]
    hb = _row_tile(heads, 128)
    n = RWKV_HEAD_DIM
    vec = pl.BlockSpec((hb, n), lambda i: (i, 0))
    st = pl.BlockSpec((hb, n, n), lambda i: (i, 0, 0))
    return pl.pallas_call(
        _wkv_step_kernel,
        grid=(heads // hb,),
        in_specs=[vec] * 6 + [st],
        out_specs=[vec, st],
        out_shape=[jax.ShapeDtypeStruct((heads, n), F32), jax.ShapeDtypeStruct((heads, n, n), F32)],
        compiler_params=_cparams("parallel"),
        name="wkv_step",
    )(r, w, k, v, a, b, state)


def _rwkv_post_kernel(y_ref, bonus_ref, g_ref, lg_ref, lb_ref, ones_ref, o_ref):
    y = y_ref[...]
    ones_bd = ones_ref[...]
    inv = 1.0 / RWKV_HEAD_DIM
    d = y - _seg_sum(y, ones_bd) * inv
    var = _seg_sum(d * d, ones_bd) * inv
    yn = d * lax.rsqrt(var + LNX_EPS) * lg_ref[...] + lb_ref[...]
    o_ref[...] = ((yn + bonus_ref[...]) * g_ref[...]).astype(o_ref.dtype)


def _rwkv_post(y, bonus, g, rp, tm):
    groups, t, wd = y.shape
    tok = pl.BlockSpec((None, tm, wd), lambda gi, i: (gi, i, 0))
    vec = pl.BlockSpec((1, wd), lambda gi, i: (0, 0))
    return pl.pallas_call(
        _rwkv_post_kernel,
        grid=(groups, t // tm),
        in_specs=[tok, tok, tok, vec, vec, pl.BlockSpec((wd, wd), lambda gi, i: (0, 0))],
        out_specs=tok,
        out_shape=jax.ShapeDtypeStruct((groups, t, wd), BF16),
        compiler_params=_cparams("parallel", "parallel"),
        name="rwkv_post",
    )(y, bonus, g, rp["lnx_g"], rp["lnx_b"], rp["ones_bd"])


MERGE_TN = 512


def _merge_kernel(oa_ref, zp_ref, yr_ref, ga_ref, gp_ref, gr_ref, wa_ref, wp_ref, wr_ref, o_ref):
    merged = (_sigmoid(ga_ref[...]) * _bdot(oa_ref[...], wa_ref[...])
              + _sigmoid(gp_ref[...]) * _bdot(zp_ref[...], wp_ref[...])
              + _sigmoid(gr_ref[...]) * _bdot(yr_ref[...], wr_ref[...]))
    o_ref[...] = merged.astype(o_ref.dtype)


def _merge(o_attn, z_pool, y_rwkv, z, w_attn_out, w_pool_out, w_rwkv_out, tm):
    groups, t, _ = z.shape
    tn = MERGE_TN
    gate = lambda br: pl.BlockSpec((None, tm, tn), lambda g, i, j: (g, i, (GL_OFF + br * D_MODEL) // tn + j))
    tok = lambda width: pl.BlockSpec((None, tm, width), lambda g, i, j: (g, i, 0))
    wt = lambda rows: pl.BlockSpec((rows, tn), lambda g, i, j: (0, j))
    return pl.pallas_call(
        _merge_kernel,
        grid=(groups, t // tm, D_MODEL // tn),
        in_specs=[tok(ATTN_WIDTH), tok(POOL_WIDTH), tok(RWKV_WIDTH), gate(0), gate(1), gate(2),
                  wt(ATTN_WIDTH), wt(POOL_WIDTH), wt(RWKV_WIDTH)],
        out_specs=pl.BlockSpec((None, tm, tn), lambda g, i, j: (g, i, j)),
        out_shape=jax.ShapeDtypeStruct((groups, t, D_MODEL), BF16),
        compiler_params=_cparams("parallel", "parallel", "arbitrary"),
        name="merge",
    )(o_attn, z_pool, y_rwkv, z, z, z, w_attn_out, w_pool_out, w_rwkv_out)


def _proj_residual_kernel(a_ref, w_ref, x_ref, gt_ref, o_ref):
    o_ref[...] = x_ref[...] + (1.0 + gt_ref[...]) * _bdot(a_ref[...], w_ref[...])


def _proj_residual(a, w, x, mod, chunk, tm, tn):
    groups, t, kdim = a.shape
    per_row = mod.shape[1] != 1
    gate = pl.BlockSpec((None, tm if per_row else 1, tn),
                        lambda g, i, j: (g, i if per_row else 0, chunk * (D_MODEL // tn) + j))
    return pl.pallas_call(
        _proj_residual_kernel,
        grid=(groups, t // tm, D_MODEL // tn),
        in_specs=[pl.BlockSpec((None, tm, kdim), lambda g, i, j: (g, i, 0)),
                  pl.BlockSpec((kdim, tn), lambda g, i, j: (0, j)),
                  pl.BlockSpec((None, tm, tn), lambda g, i, j: (g, i, j)),
                  gate],
        out_specs=pl.BlockSpec((None, tm, tn), lambda g, i, j: (g, i, j)),
        out_shape=jax.ShapeDtypeStruct((groups, t, D_MODEL), F32),
        compiler_params=_cparams("parallel", "parallel", "arbitrary"),
        name="proj_residual",
    )(a, w, x, mod)


FFN_TN = 256


def _ffn_up_kernel(x_ref, g_ref, sh_ref, sc_ref, wg_ref, wu_ref, o_ref, h_ref):
    @pl.when(pl.program_id(2) == 0)
    def _():
        h_ref[...] = _mod_norm(x_ref[...], g_ref[...], sh_ref[...], sc_ref[...]).astype(BF16)

    h = h_ref[...]
    gate = jnp.dot(h, wg_ref[...].astype(BF16), preferred_element_type=F32)
    up = jnp.dot(h, wu_ref[...].astype(BF16), preferred_element_type=F32)
    o_ref[...] = (gate * _sigmoid(gate) * up).astype(o_ref.dtype)


def _ffn_up(x, g2, mod, w_gate, w_up, tm):
    groups, t, _ = x.shape
    tn = FFN_TN
    wt = pl.BlockSpec((D_MODEL, tn), lambda g, i, j: (0, j))
    return pl.pallas_call(
        _ffn_up_kernel,
        grid=(groups, t // tm, D_FF // tn),
        in_specs=[pl.BlockSpec((None, tm, D_MODEL), lambda g, i, j: (g, i, 0)),
                  pl.BlockSpec((1, D_MODEL), lambda g, i, j: (0, 0)),
                  _mod_spec(mod, tm, 3), _mod_spec(mod, tm, 4), wt, wt],
        out_specs=pl.BlockSpec((None, tm, tn), lambda g, i, j: (g, i, j)),
        out_shape=jax.ShapeDtypeStruct((groups, t, D_FF), BF16),
        scratch_shapes=[pltpu.VMEM((tm, D_MODEL), BF16)],
        compiler_params=_cparams("parallel", "parallel", "arbitrary"),
        name="ffn_up",
    )(x, g2, mod, mod, w_gate, w_up)


def _final_norm_kernel(x_ref, g_ref, o_ref):
    x = x_ref[...]
    ms = jnp.mean(x * x, axis=-1, keepdims=True)
    o_ref[...] = x * lax.rsqrt(ms + RMS_EPS) * g_ref[...]


def _final_norm(x, g, tm):
    groups, t, _ = x.shape
    tok = pl.BlockSpec((None, tm, D_MODEL), lambda gi, i: (gi, i, 0))
    return pl.pallas_call(
        _final_norm_kernel,
        grid=(groups, t // tm),
        in_specs=[tok, pl.BlockSpec((1, D_MODEL), lambda gi, i: (0, 0))],
        out_specs=tok,
        out_shape=jax.ShapeDtypeStruct(x.shape, F32),
        compiler_params=_cparams("parallel", "parallel"),
        name="final_norm",
    )(x, g)


def _pad_cols(a, width):
    return jnp.pad(a, [(0, 0)] * (a.ndim - 1) + [(0, width - a.shape[-1])])


def _rwkv_params(l, rwkv_mu, rwkv_w0, rwkv_w2, rwkv_a0, rwkv_a2, rwkv_g2, rwkv_k_k, rwkv_k_a, rwkv_r_k,
                 rwkv_lnx_g, rwkv_lnx_b, vres_w1, vres_w2, vres_b):
    wd = RWKV_WIDTH
    row = lambda a: a.reshape(1, -1)
    lora_rows = lambda w, off: jnp.zeros((wd, wd), F32).at[off:off + w.shape[0]].set(w).astype(BF16)
    seg = np.arange(wd) // RWKV_HEAD_DIM
    rp = dict(
        mu=_pad_cols(row(rwkv_mu[l]), RWKV_PROJ_PAD), w0=row(rwkv_w0[l]), a0=row(rwkv_a0[l]),
        k_k=row(rwkv_k_k[l]), k_a=row(rwkv_k_a[l]), r_k=row(rwkv_r_k[l]),
        w2=lora_rows(rwkv_w2[l], 0), a2=lora_rows(rwkv_a2[l], DECAY_LORA),
        g2=lora_rows(rwkv_g2[l], DECAY_LORA + ICLR_LORA),
        ones_bd=jnp.asarray(seg[:, None] == seg[None, :], BF16),
        lnx_g=row(rwkv_lnx_g[l]), lnx_b=row(rwkv_lnx_b[l]))
    if l > 0:
        rp.update(vw1=_pad_cols(vres_w1[l - 1], LANES).astype(BF16),
                  vw2=jnp.pad(vres_w2[l - 1], ((0, LANES - VRES_LORA), (0, 0))).astype(BF16),
                  vb=row(vres_b[l - 1]))
    return rp


def _layer_common(x, mod, z, o_attn, z_pool, y, bonus, g, lw, rp, tm):
    y_rwkv = _rwkv_post(y, bonus, g, rp, _row_tile(x.shape[1], 512))
    merged = _merge(o_attn, z_pool, y_rwkv, z, lw["w_attn_out"], lw["w_pool_out"], lw["w_rwkv_out"], tm)
    x = _proj_residual(merged, lw["w_o"], x, mod, 2, tm, 512)
    hidden = _ffn_up(x, lw["norm2_g"], mod, lw["w_ffn_gate"], lw["w_ffn_up"], tm)
    return _proj_residual(hidden, lw["w_ffn_down"], x, mod, 5, tm, 256)


def kernel(x_prompt, x_sample, c_prompt, c_sample, cache_k_win, cache_v_win, state_pool, state_shift, state_wkv, ada_w, ada_b, norm1_g, norm2_g, w_in, attn_sink, w_attn_out, pool_w, pool_scale, w_pool_out, rwkv_mu, rwkv_w0, rwkv_w2, rwkv_a0, rwkv_a2, rwkv_g2, rwkv_k_k, rwkv_k_a, rwkv_r_k, rwkv_lnx_g, rwkv_lnx_b, w_rwkv_out, vres_w1, vres_w2, vres_b, w_o, w_ffn_gate, w_ffn_up, w_ffn_down, final_norm_g):
    depth = ada_w.shape[0]
    nb, t, _ = x_prompt.shape
    ns = x_sample.shape[0]
    assert x_sample.shape[1] == 1 and t % WINDOW == 0
    win = cache_k_win.shape[2]
    assert win <= WINDOW
    heads, n = RWKV_HEADS, RWKV_HEAD_DIM

    pad_rows = -nb % SUBLANES
    c_all = jnp.concatenate([c_prompt, jnp.zeros((pad_rows, D_MODEL), F32), c_sample], axis=0)
    mod_all = _ada(c_all, ada_w, ada_b)

    cos_p, sin_p = _rope_tables(jnp.arange(t))
    cos_s, sin_s = _rope_tables(jnp.full((ns,), PAST_LEN))

    xp = x_prompt
    xs = x_sample.reshape(1, ns, D_MODEL)
    tm_p = _row_tile(t, 1024)
    tm_s = _row_tile(ns, 1024)
    vf_p = vf_s = None
    st_p, st_s = [], []
    for l in range(depth):
        lw = dict(norm2_g=norm2_g[l].reshape(1, -1), w_attn_out=w_attn_out[l], w_pool_out=w_pool_out[l],
                  w_rwkv_out=w_rwkv_out[l], w_o=w_o[l], w_ffn_gate=w_ffn_gate[l], w_ffn_up=w_ffn_up[l],
                  w_ffn_down=w_ffn_down[l])
        rp = _rwkv_params(l, rwkv_mu, rwkv_w0, rwkv_w2, rwkv_a0, rwkv_a2, rwkv_g2, rwkv_k_k, rwkv_k_a, rwkv_r_k,
                          rwkv_lnx_g, rwkv_lnx_b, vres_w1, vres_w2, vres_b)
        g1 = norm1_g[l].reshape(1, -1)
        w_pad = jnp.concatenate([w_in[l][:, :PR_OFF + RWKV_PROJ],
                                 jnp.zeros((D_MODEL, RWKV_PROJ_PAD - RWKV_PROJ), F32),
                                 w_in[l][:, PR_OFF + RWKV_PROJ:]], axis=1)
        mod_p = mod_all[l, :nb].reshape(nb, 1, -1)
        mod_s = mod_all[l, nb + pad_rows:].reshape(1, ns, -1)

        z = _in_proj(xp, g1, mod_p, w_pad, cos_p, sin_p, tm_p)
        o_attn = _attn_prompt(z, attn_sink[l])
        z_pool = _pool_prompt(z, pool_w[l], pool_scale[l].reshape(1, -1), _row_tile(t, 512))
        r, w, k, v, a, b, g, bonus = _rwkv_prep(z, None, vf_p, rp, _row_tile(t, 256))
        if l == 0:
            vf_p = v
        y, s_pair = _wkv_scan(r, w, k, v, a, b, _row_tile(t, 256))
        wkv_new = s_pair.reshape(nb, heads // 2, n, 2, n).transpose(0, 1, 3, 2, 4).reshape(nb, heads, n, n)
        st_p.append((z[:, t - WINDOW:, K_OFF:V_OFF].reshape(nb, WINDOW, N_KV_HEADS, HEAD_DIM),
                     z[:, t - WINDOW:, V_OFF:U_OFF].reshape(nb, WINDOW, N_KV_HEADS, HEAD_DIM),
                     z[:, t - POOL_PAD:, U_OFF:PR_OFF],
                     z[:, t - 1, PR_OFF:PR_OFF + RWKV_PROJ],
                     wkv_new))
        xp = _layer_common(xp, mod_p, z, o_attn, z_pool, y, bonus, g, lw, rp, tm_p)

        z = _in_proj(xs, g1, mod_s, w_pad, cos_s, sin_s, tm_s)
        k_new, v_new = z[0, :, K_OFF:V_OFF], z[0, :, V_OFF:U_OFF]
        k_buf = cache_k_win[l].reshape(ns, win, KV_WIDTH)
        v_buf = cache_v_win[l].reshape(ns, win, KV_WIDTH)
        o3 = _attn_sample(z[0, :, :ATTN_WIDTH].reshape(ns, N_Q_HEADS, HEAD_DIM), k_new.reshape(ns, 1, KV_WIDTH),
                          v_new.reshape(ns, 1, KV_WIDTH), k_buf, v_buf, attn_sink[l])
        o_attn = o3.reshape(1, ns, ATTN_WIDTH).astype(BF16)
        z_pool = _pool_sample(z, state_pool[l].transpose(1, 0, 2), pool_w[l], pool_scale[l].reshape(1, -1))
        prev = _pad_cols(state_shift[l], RWKV_PROJ_PAD).reshape(1, ns, RWKV_PROJ_PAD)
        r, w, k, v, a, b, g, bonus = _rwkv_prep(z, prev, vf_s, rp, tm_s)
        if l == 0:
            vf_s = v
        per_head = lambda arr: arr.reshape(ns * heads, n)
        y, wkv_new = _wkv_step(*(per_head(arr) for arr in (r, w, k, v, a, b)),
                               state_wkv[l].reshape(ns * heads, n, n))
        u_new = z[0, :, None, U_OFF:PR_OFF]
        st_s.append((jnp.concatenate([k_buf[:, 1:], k_new[:, None]], axis=1).reshape(ns, win, N_KV_HEADS, HEAD_DIM),
                     jnp.concatenate([v_buf[:, 1:], v_new[:, None]], axis=1).reshape(ns, win, N_KV_HEADS, HEAD_DIM),
                     jnp.concatenate([state_pool[l][:, 1:], u_new], axis=1),
                     z[0, :, PR_OFF:PR_OFF + RWKV_PROJ],
                     wkv_new.reshape(ns, heads, n, n)))
        xs = _layer_common(xs, mod_s, z, o_attn, z_pool, y.reshape(1, ns, RWKV_WIDTH), bonus, g, lw, rp, tm_s)

    g_fin = final_norm_g.reshape(1, -1)
    y_prompt = _final_norm(xp, g_fin, tm_p)
    y_sample = _final_norm(xs, g_fin, tm_s).reshape(ns, 1, D_MODEL)
    stack = lambda states, i: jnp.stack([s[i] for s in states])
    return (y_prompt, y_sample) + tuple(stack(st_p, i) for i in range(5)) + tuple(stack(st_s, i) for i in range(5))
```

```python
import functools

import numpy as np
import jax
import jax.numpy as jnp
from jax import lax
from jax.experimental import pallas as pl
from jax.experimental.pallas import tpu as pltpu

F32 = jnp.float32
BF16 = jnp.bfloat16

D_MODEL = 2048
HEAD_DIM = 64
N_Q_HEADS = 16
N_KV_HEADS = 4
GROUP = N_Q_HEADS // N_KV_HEADS
ATTN_WIDTH = N_Q_HEADS * HEAD_DIM
KV_WIDTH = N_KV_HEADS * HEAD_DIM
WINDOW = 128
ROPE_THETA = 10000.0
POOL_WIDTH = 512
POOL_WINDOWS = (2, 4, 8, 16)
POOL_GROUP_WIDTH = 128
POOL_PAD = 15
RWKV_HEAD_DIM = 64
RWKV_WIDTH = 512
RWKV_HEADS = 8
DECAY_LORA = 96
ICLR_LORA = 96
GATE_LORA = 256
VRES_LORA = 32
RWKV_PROJ = 3 * RWKV_WIDTH + DECAY_LORA + ICLR_LORA + GATE_LORA
RWKV_PROJ_PAD = 2048
D_FF = 5632
PAST_LEN = 8192
RMS_EPS = 1e-6
LNX_EPS = 64e-5
NEG_INF = -1e30

Q_OFF = 0
K_OFF = ATTN_WIDTH
V_OFF = K_OFF + KV_WIDTH
U_OFF = V_OFF + KV_WIDTH
PR_OFF = U_OFF + POOL_WIDTH
GL_OFF = PR_OFF + RWKV_PROJ_PAD
IN_PAD = GL_OFF + 3 * D_MODEL
ROPE_END = V_OFF

LANES = 128
SUBLANES = 8
VMEM_LIMIT = 56 * 2**20


def _cparams(*sem):
    return pltpu.CompilerParams(dimension_semantics=sem, vmem_limit_bytes=VMEM_LIMIT)


def _row_tile(t, pref):
    tm = min(t, pref)
    while t % tm:
        tm -= SUBLANES
    return tm


def _sigmoid(x):
    return 1.0 / (1.0 + jnp.exp(-x))


def _mod_norm(x, g, sh, sc):
    ms = jnp.mean(x * x, axis=-1, keepdims=True)
    return (x * lax.rsqrt(ms + RMS_EPS) * g) * (1.0 + sc) + sh


def _bdot(a, b):
    return jnp.dot(a.astype(BF16), b.astype(BF16), preferred_element_type=F32)


def _seg_sum(x, ones_bd):
    hi = x.astype(BF16)
    lo = (x - hi.astype(F32)).astype(BF16)
    return (jnp.dot(hi, ones_bd, preferred_element_type=F32)
            + jnp.dot(lo, ones_bd, preferred_element_type=F32))


def _mod_spec(mod, tm, chunk):
    per_row = mod.shape[1] != 1
    rows = tm if per_row else 1
    return pl.BlockSpec((None, rows, D_MODEL), lambda g, i, j=0: (g, i if per_row else 0, chunk))


def _ada_kernel(c_ref, w_ref, b_ref, o_ref):
    c = c_ref[...]
    o_ref[...] = _bdot(c * _sigmoid(c), w_ref[...]) + b_ref[...]


def _ada(c_all, ada_w, ada_b):
    depth, _, width = ada_w.shape
    rows = c_all.shape[0]
    tn = 1024
    return pl.pallas_call(
        _ada_kernel,
        grid=(depth, width // tn),
        in_specs=[pl.BlockSpec((rows, D_MODEL), lambda l, j: (0, 0)),
                  pl.BlockSpec((None, D_MODEL, tn), lambda l, j: (l, 0, j)),
                  pl.BlockSpec((None, 1, tn), lambda l, j: (l, 0, j))],
        out_specs=pl.BlockSpec((None, rows, tn), lambda l, j: (l, 0, j)),
        out_shape=jax.ShapeDtypeStruct((depth, rows, width), F32),
        compiler_params=_cparams("parallel", "parallel"),
        name="ada",
    )(c_all, ada_w, ada_b.reshape(depth, 1, width))


IN_TN = 512
IN_MAIN_TILES = GL_OFF // IN_TN


def _in_proj_kernel(x_ref, g_ref, sh_ref, sc_ref, cos_ref, sin_ref, w_ref, wg_ref, o_ref, h_ref):
    j = pl.program_id(2)

    @pl.when(j == 0)
    def _():
        h_ref[...] = _mod_norm(x_ref[...], g_ref[...], sh_ref[...], sc_ref[...]).astype(BF16)

    @pl.when(j * IN_TN < ROPE_END)
    def _():
        acc = jnp.dot(h_ref[...], w_ref[...].astype(BF16), preferred_element_type=F32)
        reps = IN_TN // LANES
        cos = jnp.concatenate([cos_ref[...]] * reps, axis=1)
        sin = jnp.concatenate([sin_ref[...]] * reps, axis=1)
        lane = lax.broadcasted_iota(jnp.int32, acc.shape, 1)
        first_half = (lane % HEAD_DIM) < (HEAD_DIM // 2)
        rot = jnp.where(first_half, pltpu.roll(acc, IN_TN - HEAD_DIM // 2, 1),
                        pltpu.roll(acc, HEAD_DIM // 2, 1))
        roped = acc * cos + rot * sin
        o_ref[...] = jnp.where(j * IN_TN + lane < ROPE_END, roped, acc)

    @pl.when((j * IN_TN >= ROPE_END) & (j < IN_MAIN_TILES))
    def _():
        o_ref[...] = jnp.dot(h_ref[...], w_ref[...].astype(BF16), preferred_element_type=F32)

    @pl.when(j >= IN_MAIN_TILES)
    def _():
        o_ref[...] = jnp.dot(h_ref[...], wg_ref[...].astype(BF16), preferred_element_type=F32)


def _in_proj(x, g1, mod, w_in, w_gate, l, cos, sin, tm):
    groups, t, _ = x.shape
    last = IN_MAIN_TILES - 1
    return pl.pallas_call(
        _in_proj_kernel,
        grid=(groups, t // tm, IN_PAD // IN_TN),
        in_specs=[pl.BlockSpec((None, tm, D_MODEL), lambda g, i, j: (g, i, 0)),
                  pl.BlockSpec((1, D_MODEL), lambda g, i, j: (0, 0)),
                  _mod_spec(mod, tm, 0), _mod_spec(mod, tm, 1),
                  pl.BlockSpec((tm, LANES), lambda g, i, j: (i, 0)),
                  pl.BlockSpec((tm, LANES), lambda g, i, j: (i, 0)),
                  pl.BlockSpec((None, D_MODEL, IN_TN), lambda g, i, j: (l, 0, jnp.minimum(j, last))),
                  pl.BlockSpec((None, D_MODEL, IN_TN), lambda g, i, j: (l, 0, jnp.maximum(j - IN_MAIN_TILES, 0)))],
        out_specs=pl.BlockSpec((None, tm, IN_TN), lambda g, i, j: (g, i, j)),
        out_shape=jax.ShapeDtypeStruct((groups, t, IN_PAD), F32),
        scratch_shapes=[pltpu.VMEM((tm, D_MODEL), BF16)],
        compiler_params=_cparams("parallel", "parallel", "arbitrary"),
        name="in_proj",
    )(x, g1, mod, mod, cos, sin, w_in, w_gate)


def _rope_tables(pos):
    half = HEAD_DIM // 2
    inv = ROPE_THETA ** (-jnp.arange(half, dtype=F32) * 2.0 / HEAD_DIM)
    ang = pos.astype(F32)[:, None] * inv[None, :]
    cos, sin = jnp.cos(ang), jnp.sin(ang)
    reps = LANES // HEAD_DIM
    cos_t = jnp.concatenate([cos, cos] * reps, axis=1)
    sin_t = jnp.concatenate([-sin, sin] * reps, axis=1)
    return cos_t, sin_t


def _attn_prompt_kernel(sink_ref, q_ref, k_ref, v_ref, kp_ref, vp_ref, o_ref):
    n = pl.program_id(1)
    blk = q_ref.shape[0]
    q = q_ref[...]
    k2 = jnp.concatenate([kp_ref[...], k_ref[...]], axis=0).astype(BF16)
    v2 = jnp.concatenate([vp_ref[...], v_ref[...]], axis=0).astype(BF16)
    qi = lax.broadcasted_iota(jnp.int32, (blk, 2 * blk), 0)
    kc = lax.broadcasted_iota(jnp.int32, (blk, 2 * blk), 1)
    rel = qi - kc + blk
    first_key = jnp.where(n > 0, 0, blk)
    mask = (rel >= 0) & (rel <= WINDOW) & (kc >= first_key)
    outs = []
    for hd in range(N_Q_HEADS):
        h = hd // GROUP
        kh = k2[:, h * HEAD_DIM:(h + 1) * HEAD_DIM]
        vh = v2[:, h * HEAD_DIM:(h + 1) * HEAD_DIM]
        qh = q[:, hd * HEAD_DIM:(hd + 1) * HEAD_DIM].astype(BF16)
        s = lax.dot_general(qh, kh, (((1,), (1,)), ((), ())), preferred_element_type=F32) * (HEAD_DIM ** -0.5)
        s = jnp.where(mask, s, NEG_INF)
        sk = sink_ref[hd]
        m = jnp.maximum(jnp.max(s, axis=1, keepdims=True), sk)
        p = jnp.exp(s - m)
        den = jnp.sum(p, axis=1, keepdims=True) + jnp.exp(sk - m)
        outs.append(jnp.dot(p.astype(BF16), vh, preferred_element_type=F32) / den)
    o_ref[...] = jnp.concatenate(outs, axis=1).astype(o_ref.dtype)


def _attn_prompt(z, sink):
    groups, t, _ = z.shape
    blk = WINDOW
    kcol, vcol = K_OFF // KV_WIDTH, V_OFF // KV_WIDTH
    prev = lambda col: (lambda g, n: (g, jnp.maximum(n - 1, 0), col))
    return pl.pallas_call(
        _attn_prompt_kernel,
        grid=(groups, t // blk),
        in_specs=[pl.BlockSpec(memory_space=pltpu.SMEM),
                  pl.BlockSpec((None, blk, ATTN_WIDTH), lambda g, n: (g, n, 0)),
                  pl.BlockSpec((None, blk, KV_WIDTH), lambda g, n: (g, n, kcol)),
                  pl.BlockSpec((None, blk, KV_WIDTH), lambda g, n: (g, n, vcol)),
                  pl.BlockSpec((None, blk, KV_WIDTH), prev(kcol)),
                  pl.BlockSpec((None, blk, KV_WIDTH), prev(vcol))],
        out_specs=pl.BlockSpec((None, blk, ATTN_WIDTH), lambda g, n: (g, n, 0)),
        out_shape=jax.ShapeDtypeStruct((groups, t, ATTN_WIDTH), BF16),
        compiler_params=_cparams("parallel", "parallel"),
        name="attn_prompt",
    )(sink, z, z, z, z, z)


def _attn_sample_kernel(sink_ref, q_ref, kn_ref, vn_ref, kb_ref, vb_ref, o_ref):
    bb = q_ref.shape[0]
    row = lax.broadcasted_iota(jnp.int32, (N_Q_HEADS, KV_WIDTH), 0)
    lane = lax.broadcasted_iota(jnp.int32, (N_Q_HEADS, KV_WIDTH), 1)
    own = (row // GROUP) == (lane // HEAD_DIM)
    sink = sink_ref[...]
    scale = HEAD_DIM ** -0.5

    def body(b, carry):
        q = q_ref[b]
        qbd = jnp.where(own, jnp.concatenate([q] * N_KV_HEADS, axis=1), 0.0)
        s = lax.dot_general(qbd.astype(BF16), kb_ref[b].astype(BF16), (((1,), (1,)), ((), ())),
                            preferred_element_type=F32) * scale
        s_new = jnp.sum(qbd * kn_ref[b], axis=1, keepdims=True) * scale
        m = jnp.maximum(jnp.maximum(jnp.max(s, axis=1, keepdims=True), s_new), sink)
        p = jnp.exp(s - m)
        p_new = jnp.exp(s_new - m)
        den = jnp.sum(p, axis=1, keepdims=True) + p_new + jnp.exp(sink - m)
        o = jnp.dot(p.astype(BF16), vb_ref[b].astype(BF16), preferred_element_type=F32) + p_new * vn_ref[b]
        o = jnp.where(own, o, 0.0)
        o64 = o[:, 0:HEAD_DIM]
        for h in range(1, N_KV_HEADS):
            o64 = o64 + o[:, h * HEAD_DIM:(h + 1) * HEAD_DIM]
        o_ref[b] = o64 / den
        return carry

    lax.fori_loop(0, bb, body, 0)


def _attn_sample(q3, k_new, v_new, k_buf, v_buf, sink):
    nb, win = k_buf.shape[0], k_buf.shape[1]
    bb = _row_tile(nb, 16)
    return pl.pallas_call(
        _attn_sample_kernel,
        grid=(nb // bb,),
        in_specs=[pl.BlockSpec((N_Q_HEADS, 1), lambda i: (0, 0)),
                  pl.BlockSpec((bb, N_Q_HEADS, HEAD_DIM), lambda i: (i, 0, 0)),
                  pl.BlockSpec((bb, 1, KV_WIDTH), lambda i: (i, 0, 0)),
                  pl.BlockSpec((bb, 1, KV_WIDTH), lambda i: (i, 0, 0)),
                  pl.BlockSpec((bb, win, KV_WIDTH), lambda i: (i, 0, 0)),
                  pl.BlockSpec((bb, win, KV_WIDTH), lambda i: (i, 0, 0))],
        out_specs=pl.BlockSpec((bb, N_Q_HEADS, HEAD_DIM), lambda i: (i, 0, 0)),
        out_shape=jax.ShapeDtypeStruct((nb, N_Q_HEADS, HEAD_DIM), F32),
        compiler_params=_cparams("parallel"),
        name="attn_sample",
    )(sink.reshape(N_Q_HEADS, 1), q3, k_new, v_new, k_buf, v_buf)


POOL_HALO = 16


def _pool_prompt_kernel(u_ref, halo_ref, w_ref, sc_ref, o_ref, ext_ref):
    i = pl.program_id(1)
    tm = u_ref.shape[0]
    ext_ref[0:POOL_HALO, :] = jnp.where(i > 0, halo_ref[...], 0.0)
    ext_ref[POOL_HALO:POOL_HALO + tm, :] = u_ref[...]
    pos = i * tm + lax.broadcasted_iota(jnp.int32, (tm, 1), 0)
    for gi, w in enumerate(POOL_WINDOWS):
        cols = slice(gi * POOL_GROUP_WIDTH, (gi + 1) * POOL_GROUP_WIDTH)
        u = ext_ref[POOL_HALO:POOL_HALO + tm, cols]
        acc = u
        for back in range(1, w):
            acc = acc + ext_ref[POOL_HALO - back:POOL_HALO - back + tm, cols]
        cnt = jnp.minimum(pos + 1, w).astype(F32)
        d = acc / cnt - u
        o_ref[:, cols] = _bdot(d, w_ref[gi]) * sc_ref[:, cols]


def _pool_prompt(z, pool_w, pool_scale, tm):
    groups, t, _ = z.shape
    ucol = U_OFF // POOL_WIDTH
    per = tm // POOL_HALO
    return pl.pallas_call(
        _pool_prompt_kernel,
        grid=(groups, t // tm),
        in_specs=[pl.BlockSpec((None, tm, POOL_WIDTH), lambda g, i: (g, i, ucol)),
                  pl.BlockSpec((None, POOL_HALO, POOL_WIDTH), lambda g, i: (g, jnp.maximum(i * per - 1, 0), ucol)),
                  pl.BlockSpec((len(POOL_WINDOWS), POOL_GROUP_WIDTH, POOL_GROUP_WIDTH), lambda g, i: (0, 0, 0)),
                  pl.BlockSpec((1, POOL_WIDTH), lambda g, i: (0, 0))],
        out_specs=pl.BlockSpec((None, tm, POOL_WIDTH), lambda g, i: (g, i, 0)),
        out_shape=jax.ShapeDtypeStruct((groups, t, POOL_WIDTH), F32),
        scratch_shapes=[pltpu.VMEM((tm + POOL_HALO, POOL_WIDTH), F32)],
        compiler_params=_cparams("parallel", "parallel"),
        name="pool_prompt",
    )(z, z, pool_w, pool_scale)


def _pool_sample_kernel(u_ref, st_ref, w_ref, sc_ref, o_ref):
    for gi, w in enumerate(POOL_WINDOWS):
        cols = slice(gi * POOL_GROUP_WIDTH, (gi + 1) * POOL_GROUP_WIDTH)
        u = u_ref[:, cols]
        acc = u
        for back in range(1, w):
            acc = acc + st_ref[POOL_PAD - back, :, cols]
        cnt = float(min(PAST_LEN + 1, w))
        d = acc / cnt - u
        o_ref[:, cols] = _bdot(d, w_ref[gi]) * sc_ref[:, cols]


def _pool_sample(z, state_t, pool_w, pool_scale):
    rows = z.shape[1]
    ucol = U_OFF // POOL_WIDTH
    return pl.pallas_call(
        _pool_sample_kernel,
        grid=(1,),
        in_specs=[pl.BlockSpec((None, rows, POOL_WIDTH), lambda i: (0, 0, ucol)),
                  pl.BlockSpec((POOL_PAD, rows, POOL_WIDTH), lambda i: (0, 0, 0)),
                  pl.BlockSpec((len(POOL_WINDOWS), POOL_GROUP_WIDTH, POOL_GROUP_WIDTH), lambda i: (0, 0, 0)),
                  pl.BlockSpec((1, POOL_WIDTH), lambda i: (0, 0))],
        out_specs=pl.BlockSpec((None, rows, POOL_WIDTH), lambda i: (0, 0, 0)),
        out_shape=jax.ShapeDtypeStruct((1, rows, POOL_WIDTH), F32),
        compiler_params=_cparams("arbitrary"),
        name="pool_sample",
    )(z, state_t, pool_w, pool_scale)


PREP_HALO = 8


def _rwkv_prep_kernel(*refs, halo, has_vres):
    it = iter(refs)
    pr_ref, prev_ref = next(it), next(it)
    vf_ref = next(it) if has_vres else None
    mu_ref, w0_ref, a0_ref, kk_ref, ka_ref, rk_ref = (next(it) for _ in range(6))
    w2_ref, a2_ref, g2_ref, ones_ref = (next(it) for _ in range(4))
    if has_vres:
        vw1_ref, vw2_ref, vb_ref = next(it), next(it), next(it)
    r_o, w_o, k_o, v_o, a_o, b_o, g_o, bonus_o = (next(it) for _ in range(8))
    ext_ref = next(it) if halo else None

    pr = pr_ref[...]
    tm = pr.shape[0]
    if halo:
        i = pl.program_id(1)
        ext_ref[0:PREP_HALO, :] = jnp.where(i > 0, prev_ref[...], 0.0)
        ext_ref[PREP_HALO:PREP_HALO + tm, :] = pr
        prev = ext_ref[PREP_HALO - 1:PREP_HALO - 1 + tm, :]
    else:
        prev = prev_ref[...]
    xm = pr + (prev - pr) * mu_ref[...]
    wd = RWKV_WIDTH
    r, k, v, lora_in = xm[:, 0:wd], xm[:, wd:2 * wd], xm[:, 2 * wd:3 * wd], xm[:, 3 * wd:4 * wd]
    ones_bd = ones_ref[...]
    lw = _bdot(jnp.tanh(lora_in), w2_ref[...])
    la = _bdot(lora_in, a2_ref[...])
    g = _bdot(_sigmoid(lora_in), g2_ref[...])
    y = -(w0_ref[...] + lw)
    softplus = jnp.maximum(y, 0.0) + jnp.log(1.0 + jnp.exp(-jnp.abs(y)))
    decay = jnp.exp(-jnp.exp(-softplus - 0.5))
    a = _sigmoid(a0_ref[...] + la)
    if has_vres:
        t2 = _bdot(_bdot(v, vw1_ref[...]), vw2_ref[...])
        v = v + (vf_ref[...] - v) * _sigmoid(vb_ref[...] + t2)
    kk = k * kk_ref[...]
    kk = kk / jnp.maximum(jnp.sqrt(_seg_sum(kk * kk, ones_bd)), 1e-12)
    k = k * (1.0 + (a - 1.0) * ka_ref[...])
    r_o[...] = r
    w_o[...] = decay
    k_o[...] = k
    v_o[...] = v
    a_o[...] = -kk
    b_o[...] = kk * a
    g_o[...] = g
    bonus_o[...] = _seg_sum(r * k * rk_ref[...], ones_bd) * v


def _rwkv_prep(z, prev, v_first, rp, tm):
    groups, t, _ = z.shape
    halo = prev is None
    has_vres = v_first is not None
    prcol = PR_OFF // RWKV_PROJ_PAD
    wd = RWKV_WIDTH
    tok = lambda width: pl.BlockSpec((None, tm, width), lambda g, i: (g, i, 0))
    vec = lambda width: pl.BlockSpec((1, width), lambda g, i: (0, 0))
    mat = lambda a, b: pl.BlockSpec((a, b), lambda g, i: (0, 0))
    args = [z]
    in_specs = [pl.BlockSpec((None, tm, RWKV_PROJ_PAD), lambda g, i: (g, i, prcol))]
    if halo:
        per = tm // PREP_HALO
        args.append(z)
        in_specs.append(pl.BlockSpec((None, PREP_HALO, RWKV_PROJ_PAD),
                                     lambda g, i: (g, jnp.maximum(i * per - 1, 0), prcol)))
    else:
        args.append(prev)
        in_specs.append(tok(RWKV_PROJ_PAD))
    if has_vres:
        args.append(v_first)
        in_specs.append(tok(wd))
    args += [rp["mu"], rp["w0"], rp["a0"], rp["k_k"], rp["k_a"], rp["r_k"],
             rp["w2"], rp["a2"], rp["g2"], rp["ones_bd"]]
    in_specs += [vec(RWKV_PROJ_PAD)] + [vec(wd)] * 5 + [mat(wd, wd)] * 4
    if has_vres:
        args += [rp["vw1"], rp["vw2"], rp["vb"]]
        in_specs += [mat(wd, LANES), mat(LANES, wd), vec(wd)]
    return pl.pallas_call(
        functools.partial(_rwkv_prep_kernel, halo=halo, has_vres=has_vres),
        grid=(groups, t // tm),
        in_specs=in_specs,
        out_specs=[tok(wd)] * 8,
        out_shape=[jax.ShapeDtypeStruct((groups, t, wd), F32)] * 8,
        scratch_shapes=[pltpu.VMEM((tm + PREP_HALO, RWKV_PROJ_PAD), F32)] if halo else [],
        compiler_params=_cparams("parallel", "parallel"),
        name="rwkv_prep",
    )(*args)


def _wkv_scan_kernel(r_ref, w_ref, k_ref, v_ref, a_ref, b_ref, y_ref, s_ref):
    nb, tc, _ = r_ref.shape
    pairs = RWKV_HEADS // 2
    n = RWKV_HEAD_DIM

    @pl.when(pl.program_id(0) == 0)
    def _():
        s_ref[...] = jnp.zeros(s_ref.shape, F32)

    row = lax.broadcasted_iota(jnp.int32, (n, 2 * n), 0)
    lane = lax.broadcasted_iota(jnp.int32, (n, 2 * n), 1)
    diag = (lane % n) == row
    seg_r = lax.broadcasted_iota(jnp.int32, (2 * n, 2 * n), 0) // n
    seg_c = lax.broadcasted_iota(jnp.int32, (2 * n, 2 * n), 1) // n
    ones_bd = (seg_r == seg_c).astype(BF16)

    def seg_bcast(x, passes):
        hi = x.astype(BF16)
        out = jnp.dot(hi, ones_bd, preferred_element_type=F32)
        if passes == 2:
            out = out + jnp.dot((x - hi.astype(F32)).astype(BF16), ones_bd, preferred_element_type=F32)
        return out

    chains = [(bi, slice(p * 2 * n, (p + 1) * 2 * n)) for bi in range(nb) for p in range(pairs)]
    nc = len(chains)
    stack = lambda xs: jnp.concatenate(xs, axis=0)
    unstack = lambda x: [x[c * n:(c + 1) * n] for c in range(nc)]

    def steps(t8, carry):
        base = pl.multiple_of(t8 * SUBLANES, SUBLANES)
        rows = pl.ds(base, SUBLANES)
        tiles = [[ref[bi, rows, sl] for ref in (r_ref, w_ref, k_ref, v_ref, a_ref, b_ref)] for bi, sl in chains]
        s = [s_ref[c] for c in range(nc)]
        ys = [[] for _ in chains]
        for u in range(SUBLANES):
            one = slice(u, u + 1)
            vcol = unstack(seg_bcast(stack([jnp.where(diag, t[3][one], 0.0) for t in tiles]), 1))
            sa = unstack(seg_bcast(stack([s[c] * tiles[c][4][one] for c in range(nc)]), 2))
            for c, (r8, w8, k8, v8, a8, b8) in enumerate(tiles):
                s[c] = s[c] * w8[one] + sa[c] * b8[one] + vcol[c] * k8[one]
            ycol = unstack(seg_bcast(stack([s[c] * tiles[c][0][one] for c in range(nc)]), 1))
            for c in range(nc):
                ys[c].append(jnp.sum(jnp.where(diag, ycol[c], 0.0), axis=0, keepdims=True))
        for c, (bi, sl) in enumerate(chains):
            s_ref[c] = s[c]
            y_ref[bi, rows, sl] = jnp.concatenate(ys[c], axis=0)
        return carry

    lax.fori_loop(0, tc // SUBLANES, steps, 0)


def _wkv_scan(r, w, k, v, a, b, tc):
    nb, t, wd = r.shape
    pairs = RWKV_HEADS // 2
    tok = pl.BlockSpec((nb, tc, wd), lambda i: (0, i, 0))
    return pl.pallas_call(
        _wkv_scan_kernel,
        grid=(t // tc,),
        in_specs=[tok] * 6,
        out_specs=[tok, pl.BlockSpec((nb * pairs, RWKV_HEAD_DIM, 2 * RWKV_HEAD_DIM), lambda i: (0, 0, 0))],
        out_shape=[jax.ShapeDtypeStruct((nb, t, wd), F32),
                   jax.ShapeDtypeStruct((nb * pairs, RWKV_HEAD_DIM, 2 * RWKV_HEAD_DIM), F32)],
        compiler_params=_cparams("arbitrary"),
        name="wkv_scan",
    )(r, w, k, v, a, b)


def _wkv_step_kernel(r_ref, w_ref, k_ref, v_ref, a_ref, b_ref, s_ref, y_ref, so_ref):
    heads = s_ref.shape[0]
    n = RWKV_HEAD_DIM
    diag = lax.broadcasted_iota(jnp.int32, (n, n), 0) == lax.broadcasted_iota(jnp.int32, (n, n), 1)

    def body(h8, carry):
        base = pl.multiple_of(h8 * SUBLANES, SUBLANES)
        rows = pl.ds(base, SUBLANES)
        r8, w8, k8, v8, a8, b8 = (ref[rows, :] for ref in (r_ref, w_ref, k_ref, v_ref, a_ref, b_ref))
        y8 = []
        for u in range(SUBLANES):
            one = slice(u, u + 1)
            s = s_ref[base + u]
            sa = jnp.sum(s * a8[one], axis=1, keepdims=True)
            vcol = jnp.sum(jnp.where(diag, v8[one], 0.0), axis=1, keepdims=True)
            s = s * w8[one] + sa * b8[one] + vcol * k8[one]
            so_ref[base + u] = s
            ycol = jnp.sum(s * r8[one], axis=1, keepdims=True)
            y8.append(jnp.sum(jnp.where(diag, ycol, 0.0), axis=0, keepdims=True))
        y_ref[rows, :] = jnp.concatenate(y8, axis=0)
        return carry

    lax.fori_loop(0, heads // SUBLANES, body, 0)


def _wkv_step(r, w, k, v, a, b, state, l):
    heads = state.shape[1]
    hb = _row_tile(heads, 128)
    n = RWKV_HEAD_DIM
    vec = pl.BlockSpec((hb, n), lambda i: (i, 0))
    st = pl.BlockSpec((hb, n, n), lambda i: (i, 0, 0))
    return pl.pallas_call(
        _wkv_step_kernel,
        grid=(heads // hb,),
        in_specs=[vec] * 6 + [pl.BlockSpec((None, hb, n, n), lambda i: (l, i, 0, 0))],
        out_specs=[vec, st],
        out_shape=[jax.ShapeDtypeStruct((heads, n), F32), jax.ShapeDtypeStruct((heads, n, n), F32)],
        compiler_params=_cparams("parallel"),
        name="wkv_step",
    )(r, w, k, v, a, b, state)


def _rwkv_post_kernel(y_ref, bonus_ref, g_ref, lg_ref, lb_ref, ones_ref, o_ref):
    y = y_ref[...]
    ones_bd = ones_ref[...]
    inv = 1.0 / RWKV_HEAD_DIM
    d = y - _seg_sum(y, ones_bd) * inv
    var = _seg_sum(d * d, ones_bd) * inv
    yn = d * lax.rsqrt(var + LNX_EPS) * lg_ref[...] + lb_ref[...]
    o_ref[...] = ((yn + bonus_ref[...]) * g_ref[...]).astype(o_ref.dtype)


def _rwkv_post(y, bonus, g, rp, tm):
    groups, t, wd = y.shape
    tok = pl.BlockSpec((None, tm, wd), lambda gi, i: (gi, i, 0))
    vec = pl.BlockSpec((1, wd), lambda gi, i: (0, 0))
    return pl.pallas_call(
        _rwkv_post_kernel,
        grid=(groups, t // tm),
        in_specs=[tok, tok, tok, vec, vec, pl.BlockSpec((wd, wd), lambda gi, i: (0, 0))],
        out_specs=tok,
        out_shape=jax.ShapeDtypeStruct((groups, t, wd), BF16),
        compiler_params=_cparams("parallel", "parallel"),
        name="rwkv_post",
    )(y, bonus, g, rp["lnx_g"], rp["lnx_b"], rp["ones_bd"])


MERGE_TN = 512


def _merge_kernel(oa_ref, zp_ref, yr_ref, ga_ref, gp_ref, gr_ref, wa_ref, wp_ref, wr_ref, o_ref):
    merged = (_sigmoid(ga_ref[...]) * _bdot(oa_ref[...], wa_ref[...])
              + _sigmoid(gp_ref[...]) * _bdot(zp_ref[...], wp_ref[...])
              + _sigmoid(gr_ref[...]) * _bdot(yr_ref[...], wr_ref[...]))
    o_ref[...] = merged.astype(o_ref.dtype)


def _merge(o_attn, z_pool, y_rwkv, z, w_attn_out, w_pool_out, w_rwkv_out, l, tm):
    groups, t, _ = z.shape
    tn = MERGE_TN
    gate = lambda br: pl.BlockSpec((None, tm, tn), lambda g, i, j: (g, i, (GL_OFF + br * D_MODEL) // tn + j))
    tok = lambda width: pl.BlockSpec((None, tm, width), lambda g, i, j: (g, i, 0))
    wt = lambda rows: pl.BlockSpec((None, rows, tn), lambda g, i, j: (l, 0, j))
    return pl.pallas_call(
        _merge_kernel,
        grid=(groups, t // tm, D_MODEL // tn),
        in_specs=[tok(ATTN_WIDTH), tok(POOL_WIDTH), tok(RWKV_WIDTH), gate(0), gate(1), gate(2),
                  wt(ATTN_WIDTH), wt(POOL_WIDTH), wt(RWKV_WIDTH)],
        out_specs=pl.BlockSpec((None, tm, tn), lambda g, i, j: (g, i, j)),
        out_shape=jax.ShapeDtypeStruct((groups, t, D_MODEL), BF16),
        compiler_params=_cparams("parallel", "parallel", "arbitrary"),
        name="merge",
    )(o_attn, z_pool, y_rwkv, z, z, z, w_attn_out, w_pool_out, w_rwkv_out)


def _proj_residual_kernel(a_ref, w_ref, x_ref, gt_ref, o_ref):
    o_ref[...] = x_ref[...] + (1.0 + gt_ref[...]) * _bdot(a_ref[...], w_ref[...])


def _proj_residual(a, w, l, x, mod, chunk, tm, tn):
    groups, t, kdim = a.shape
    per_row = mod.shape[1] != 1
    gate = pl.BlockSpec((None, tm if per_row else 1, tn),
                        lambda g, i, j: (g, i if per_row else 0, chunk * (D_MODEL // tn) + j))
    return pl.pallas_call(
        _proj_residual_kernel,
        grid=(groups, t // tm, D_MODEL // tn),
        in_specs=[pl.BlockSpec((None, tm, kdim), lambda g, i, j: (g, i, 0)),
                  pl.BlockSpec((None, kdim, tn), lambda g, i, j: (l, 0, j)),
                  pl.BlockSpec((None, tm, tn), lambda g, i, j: (g, i, j)),
                  gate],
        out_specs=pl.BlockSpec((None, tm, tn), lambda g, i, j: (g, i, j)),
        out_shape=jax.ShapeDtypeStruct((groups, t, D_MODEL), F32),
        compiler_params=_cparams("parallel", "parallel", "arbitrary"),
        name="proj_residual",
    )(a, w, x, mod)


FFN_TN = 256


def _ffn_up_kernel(x_ref, g_ref, sh_ref, sc_ref, wg_ref, wu_ref, o_ref, h_ref):
    @pl.when(pl.program_id(2) == 0)
    def _():
        h_ref[...] = _mod_norm(x_ref[...], g_ref[...], sh_ref[...], sc_ref[...]).astype(BF16)

    h = h_ref[...]
    gate = jnp.dot(h, wg_ref[...].astype(BF16), preferred_element_type=F32)
    up = jnp.dot(h, wu_ref[...].astype(BF16), preferred_element_type=F32)
    o_ref[...] = (gate * _sigmoid(gate) * up).astype(o_ref.dtype)


def _ffn_up(x, g2, mod, w_gate, w_up, l, tm):
    groups, t, _ = x.shape
    tn = FFN_TN
    wt = pl.BlockSpec((None, D_MODEL, tn), lambda g, i, j: (l, 0, j))
    return pl.pallas_call(
        _ffn_up_kernel,
        grid=(groups, t // tm, D_FF // tn),
        in_specs=[pl.BlockSpec((None, tm, D_MODEL), lambda g, i, j: (g, i, 0)),
                  pl.BlockSpec((1, D_MODEL), lambda g, i, j: (0, 0)),
                  _mod_spec(mod, tm, 3), _mod_spec(mod, tm, 4), wt, wt],
        out_specs=pl.BlockSpec((None, tm, tn), lambda g, i, j: (g, i, j)),
        out_shape=jax.ShapeDtypeStruct((groups, t, D_FF), BF16),
        scratch_shapes=[pltpu.VMEM((tm, D_MODEL), BF16)],
        compiler_params=_cparams("parallel", "parallel", "arbitrary"),
        name="ffn_up",
    )(x, g2, mod, mod, w_gate, w_up)


def _final_norm_kernel(x_ref, g_ref, o_ref):
    x = x_ref[...]
    ms = jnp.mean(x * x, axis=-1, keepdims=True)
    o_ref[...] = x * lax.rsqrt(ms + RMS_EPS) * g_ref[...]


def _final_norm(x, g, tm):
    groups, t, _ = x.shape
    tok = pl.BlockSpec((None, tm, D_MODEL), lambda gi, i: (gi, i, 0))
    return pl.pallas_call(
        _final_norm_kernel,
        grid=(groups, t // tm),
        in_specs=[tok, pl.BlockSpec((1, D_MODEL), lambda gi, i: (0, 0))],
        out_specs=tok,
        out_shape=jax.ShapeDtypeStruct(x.shape, F32),
        compiler_params=_cparams("parallel", "parallel"),
        name="final_norm",
    )(x, g)


def _pad_cols(a, width):
    return jnp.pad(a, [(0, 0)] * (a.ndim - 1) + [(0, width - a.shape[-1])])


def _rwkv_params(l, rwkv_mu, rwkv_w0, rwkv_w2, rwkv_a0, rwkv_a2, rwkv_g2, rwkv_k_k, rwkv_k_a, rwkv_r_k,
                 rwkv_lnx_g, rwkv_lnx_b, vres_w1, vres_w2, vres_b):
    wd = RWKV_WIDTH
    row = lambda a: a.reshape(1, -1)
    lora_rows = lambda w, off: jnp.zeros((wd, wd), F32).at[off:off + w.shape[0]].set(w).astype(BF16)
    seg = np.arange(wd) // RWKV_HEAD_DIM
    rp = dict(
        mu=_pad_cols(row(rwkv_mu[l]), RWKV_PROJ_PAD), w0=row(rwkv_w0[l]), a0=row(rwkv_a0[l]),
        k_k=row(rwkv_k_k[l]), k_a=row(rwkv_k_a[l]), r_k=row(rwkv_r_k[l]),
        w2=lora_rows(rwkv_w2[l], 0), a2=lora_rows(rwkv_a2[l], DECAY_LORA),
        g2=lora_rows(rwkv_g2[l], DECAY_LORA + ICLR_LORA),
        ones_bd=jnp.asarray(seg[:, None] == seg[None, :], BF16),
        lnx_g=row(rwkv_lnx_g[l]), lnx_b=row(rwkv_lnx_b[l]))
    if l > 0:
        rp.update(vw1=_pad_cols(vres_w1[l - 1], LANES).astype(BF16),
                  vw2=jnp.pad(vres_w2[l - 1], ((0, LANES - VRES_LORA), (0, 0))).astype(BF16),
                  vb=row(vres_b[l - 1]))
    return rp


def _layer_common(x, mod, z, o_attn, z_pool, y, bonus, g, lw, l, rp, tm):
    y_rwkv = _rwkv_post(y, bonus, g, rp, _row_tile(x.shape[1], 512))
    merged = _merge(o_attn, z_pool, y_rwkv, z, lw["w_attn_out"], lw["w_pool_out"], lw["w_rwkv_out"], l, tm)
    x = _proj_residual(merged, lw["w_o"], l, x, mod, 2, tm, 512)
    hidden = _ffn_up(x, lw["norm2_g"][l].reshape(1, -1), mod, lw["w_ffn_gate"], lw["w_ffn_up"], l, tm)
    return _proj_residual(hidden, lw["w_ffn_down"], l, x, mod, 5, tm, 256)


def kernel(x_prompt, x_sample, c_prompt, c_sample, cache_k_win, cache_v_win, state_pool, state_shift, state_wkv, ada_w, ada_b, norm1_g, norm2_g, w_in, attn_sink, w_attn_out, pool_w, pool_scale, w_pool_out, rwkv_mu, rwkv_w0, rwkv_w2, rwkv_a0, rwkv_a2, rwkv_g2, rwkv_k_k, rwkv_k_a, rwkv_r_k, rwkv_lnx_g, rwkv_lnx_b, w_rwkv_out, vres_w1, vres_w2, vres_b, w_o, w_ffn_gate, w_ffn_up, w_ffn_down, final_norm_g):
    depth = ada_w.shape[0]
    nb, t, _ = x_prompt.shape
    ns = x_sample.shape[0]
    assert x_sample.shape[1] == 1 and t % WINDOW == 0
    win = cache_k_win.shape[2]
    assert win <= WINDOW
    heads, n = RWKV_HEADS, RWKV_HEAD_DIM

    pad_rows = -nb % SUBLANES
    c_all = jnp.concatenate([c_prompt, jnp.zeros((pad_rows, D_MODEL), F32), c_sample], axis=0)
    mod_all = _ada(c_all, ada_w, ada_b)

    cos_p, sin_p = _rope_tables(jnp.arange(t))
    cos_s, sin_s = _rope_tables(jnp.full((ns,), PAST_LEN))

    xp = x_prompt
    xs = x_sample.reshape(1, ns, D_MODEL)
    tm_p = _row_tile(t, 1024)
    tm_s = _row_tile(ns, 1024)
    vf_p = vf_s = None
    st_p, st_s = [], []
    lw = dict(norm2_g=norm2_g, w_attn_out=w_attn_out, w_pool_out=w_pool_out, w_rwkv_out=w_rwkv_out, w_o=w_o,
              w_ffn_gate=w_ffn_gate, w_ffn_up=w_ffn_up, w_ffn_down=w_ffn_down)
    w_gate = w_in[:, :, PR_OFF + RWKV_PROJ:]
    state_heads = state_wkv.reshape(depth, ns * heads, n, n)
    for l in range(depth):
        rp = _rwkv_params(l, rwkv_mu, rwkv_w0, rwkv_w2, rwkv_a0, rwkv_a2, rwkv_g2, rwkv_k_k, rwkv_k_a, rwkv_r_k,
                          rwkv_lnx_g, rwkv_lnx_b, vres_w1, vres_w2, vres_b)
        g1 = norm1_g[l].reshape(1, -1)
        mod_p = mod_all[l, :nb].reshape(nb, 1, -1)
        mod_s = mod_all[l, nb + pad_rows:].reshape(1, ns, -1)

        z = _in_proj(xp, g1, mod_p, w_in, w_gate, l, cos_p, sin_p, tm_p)
        o_attn = _attn_prompt(z, attn_sink[l])
        z_pool = _pool_prompt(z, pool_w[l], pool_scale[l].reshape(1, -1), _row_tile(t, 512))
        r, w, k, v, a, b, g, bonus = _rwkv_prep(z, None, vf_p, rp, _row_tile(t, 256))
        if l == 0:
            vf_p = v
        y, s_pair = _wkv_scan(r, w, k, v, a, b, _row_tile(t, 256))
        wkv_new = s_pair.reshape(nb, heads // 2, n, 2, n).transpose(0, 1, 3, 2, 4).reshape(nb, heads, n, n)
        st_p.append((z[:, t - WINDOW:, K_OFF:V_OFF].reshape(nb, WINDOW, N_KV_HEADS, HEAD_DIM),
                     z[:, t - WINDOW:, V_OFF:U_OFF].reshape(nb, WINDOW, N_KV_HEADS, HEAD_DIM),
                     z[:, t - POOL_PAD:, U_OFF:PR_OFF],
                     z[:, t - 1, PR_OFF:PR_OFF + RWKV_PROJ],
                     wkv_new))
        xp = _layer_common(xp, mod_p, z, o_attn, z_pool, y, bonus, g, lw, l, rp, tm_p)

        z = _in_proj(xs, g1, mod_s, w_in, w_gate, l, cos_s, sin_s, tm_s)
        k_new, v_new = z[0, :, K_OFF:V_OFF], z[0, :, V_OFF:U_OFF]
        k_buf = cache_k_win[l].reshape(ns, win, KV_WIDTH)
        v_buf = cache_v_win[l].reshape(ns, win, KV_WIDTH)
        o3 = _attn_sample(z[0, :, :ATTN_WIDTH].reshape(ns, N_Q_HEADS, HEAD_DIM), k_new.reshape(ns, 1, KV_WIDTH),
                          v_new.reshape(ns, 1, KV_WIDTH), k_buf, v_buf, attn_sink[l])
        o_attn = o3.reshape(1, ns, ATTN_WIDTH).astype(BF16)
        z_pool = _pool_sample(z, state_pool[l].transpose(1, 0, 2), pool_w[l], pool_scale[l].reshape(1, -1))
        prev = _pad_cols(state_shift[l], RWKV_PROJ_PAD).reshape(1, ns, RWKV_PROJ_PAD)
        r, w, k, v, a, b, g, bonus = _rwkv_prep(z, prev, vf_s, rp, tm_s)
        if l == 0:
            vf_s = v
        per_head = lambda arr: arr.reshape(ns * heads, n)
        y, wkv_new = _wkv_step(*(per_head(arr) for arr in (r, w, k, v, a, b)), state_heads, l)
        u_new = z[0, :, None, U_OFF:PR_OFF]
        st_s.append((jnp.concatenate([k_buf[:, 1:], k_new[:, None]], axis=1).reshape(ns, win, N_KV_HEADS, HEAD_DIM),
                     jnp.concatenate([v_buf[:, 1:], v_new[:, None]], axis=1).reshape(ns, win, N_KV_HEADS, HEAD_DIM),
                     jnp.concatenate([state_pool[l][:, 1:], u_new], axis=1),
                     z[0, :, PR_OFF:PR_OFF + RWKV_PROJ],
                     wkv_new.reshape(ns, heads, n, n)))
        xs = _layer_common(xs, mod_s, z, o_attn, z_pool, y.reshape(1, ns, RWKV_WIDTH), bonus, g, lw, l, rp, tm_s)

    g_fin = final_norm_g.reshape(1, -1)
    y_prompt = _final_norm(xp, g_fin, tm_p)
    y_sample = _final_norm(xs, g_fin, tm_s).reshape(ns, 1, D_MODEL)
    stack = lambda states, i: jnp.stack([s[i] for s in states])
    return (y_prompt, y_sample) + tuple(stack(st_p, i) for i in range(5)) + tuple(stack(st_s, i) for i in range(5))
```

```python
import functools

import numpy as np
import jax
import jax.numpy as jnp
from jax import lax
from jax.experimental import pallas as pl
from jax.experimental.pallas import tpu as pltpu

F32 = jnp.float32
BF16 = jnp.bfloat16

D_MODEL = 2048
HEAD_DIM = 64
N_Q_HEADS = 16
N_KV_HEADS = 4
GROUP = N_Q_HEADS // N_KV_HEADS
ATTN_WIDTH = N_Q_HEADS * HEAD_DIM
KV_WIDTH = N_KV_HEADS * HEAD_DIM
WINDOW = 128
ROPE_THETA = 10000.0
POOL_WIDTH = 512
POOL_WINDOWS = (2, 4, 8, 16)
POOL_GROUP_WIDTH = 128
POOL_PAD = 15
RWKV_HEAD_DIM = 64
RWKV_WIDTH = 512
RWKV_HEADS = 8
DECAY_LORA = 96
ICLR_LORA = 96
GATE_LORA = 256
VRES_LORA = 32
RWKV_PROJ = 3 * RWKV_WIDTH + DECAY_LORA + ICLR_LORA + GATE_LORA
RWKV_PROJ_PAD = 2048
D_FF = 5632
PAST_LEN = 8192
RMS_EPS = 1e-6
LNX_EPS = 64e-5
NEG_INF = -1e30

Q_OFF = 0
K_OFF = ATTN_WIDTH
V_OFF = K_OFF + KV_WIDTH
U_OFF = V_OFF + KV_WIDTH
PR_OFF = U_OFF + POOL_WIDTH
GL_OFF = PR_OFF + RWKV_PROJ_PAD
IN_PAD = GL_OFF + 3 * D_MODEL
ROPE_END = V_OFF

LANES = 128
SUBLANES = 8
VMEM_LIMIT = 56 * 2**20


def _cparams(*sem):
    return pltpu.CompilerParams(dimension_semantics=sem, vmem_limit_bytes=VMEM_LIMIT)


def _row_tile(t, pref):
    tm = min(t, pref)
    while t % tm:
        tm -= SUBLANES
    return tm


def _sigmoid(x):
    return 1.0 / (1.0 + jnp.exp(-x))


def _mod_norm(x, g, sh, sc):
    ms = jnp.mean(x * x, axis=-1, keepdims=True)
    return (x * lax.rsqrt(ms + RMS_EPS) * g) * (1.0 + sc) + sh


def _bdot(a, b):
    return jnp.dot(a.astype(BF16), b.astype(BF16), preferred_element_type=F32)


def _seg_sum(x, ones_bd):
    hi = x.astype(BF16)
    lo = (x - hi.astype(F32)).astype(BF16)
    return (jnp.dot(hi, ones_bd, preferred_element_type=F32)
            + jnp.dot(lo, ones_bd, preferred_element_type=F32))


def _mod_spec(mod, tm, chunk):
    per_row = mod.shape[1] != 1
    rows = tm if per_row else 1
    return pl.BlockSpec((None, rows, D_MODEL), lambda g, i, j=0: (g, i if per_row else 0, chunk))


def _ada_kernel(c_ref, w_ref, b_ref, o_ref):
    c = c_ref[...]
    o_ref[...] = _bdot(c * _sigmoid(c), w_ref[...]) + b_ref[...]


def _ada(c_all, ada_w, ada_b):
    depth, _, width = ada_w.shape
    rows = c_all.shape[0]
    tn = 1024
    return pl.pallas_call(
        _ada_kernel,
        grid=(depth, width // tn),
        in_specs=[pl.BlockSpec((rows, D_MODEL), lambda l, j: (0, 0)),
                  pl.BlockSpec((None, D_MODEL, tn), lambda l, j: (l, 0, j)),
                  pl.BlockSpec((None, 1, tn), lambda l, j: (l, 0, j))],
        out_specs=pl.BlockSpec((None, rows, tn), lambda l, j: (l, 0, j)),
        out_shape=jax.ShapeDtypeStruct((depth, rows, width), F32),
        compiler_params=_cparams("parallel", "parallel"),
        name="ada",
    )(c_all, ada_w, ada_b.reshape(depth, 1, width))


IN_TN = 512
IN_MAIN_TILES = GL_OFF // IN_TN


def _in_proj_kernel(x_ref, g_ref, sh_ref, sc_ref, cos_ref, sin_ref, w_ref, o_ref, h_ref):
    j = pl.program_id(2)

    @pl.when(j == 0)
    def _():
        h_ref[...] = _mod_norm(x_ref[...], g_ref[...], sh_ref[...], sc_ref[...]).astype(BF16)

    def project():
        return lax.dot_general(h_ref[...], w_ref[0].astype(BF16), (((1,), (1,)), ((), ())),
                               preferred_element_type=F32)

    @pl.when(j * IN_TN < ROPE_END)
    def _():
        acc = project()
        reps = IN_TN // LANES
        cos = jnp.concatenate([cos_ref[...]] * reps, axis=1)
        sin = jnp.concatenate([sin_ref[...]] * reps, axis=1)
        lane = lax.broadcasted_iota(jnp.int32, acc.shape, 1)
        first_half = (lane % HEAD_DIM) < (HEAD_DIM // 2)
        rot = jnp.where(first_half, pltpu.roll(acc, IN_TN - HEAD_DIM // 2, 1),
                        pltpu.roll(acc, HEAD_DIM // 2, 1))
        roped = acc * cos + rot * sin
        o_ref[...] = jnp.where(j * IN_TN + lane < ROPE_END, roped, acc)

    @pl.when(j * IN_TN >= ROPE_END)
    def _():
        o_ref[...] = project()


def _in_proj(x, g1, mod, w_in_t, l, cos, sin, tm):
    groups, t, _ = x.shape
    gate_shift = RWKV_PROJ_PAD - RWKV_PROJ

    def weight_rows(g, i, j):
        start = jnp.where(j < IN_MAIN_TILES, j * IN_TN, j * IN_TN - gate_shift)
        return (l, pl.multiple_of(start, SUBLANES), 0)

    return pl.pallas_call(
        _in_proj_kernel,
        grid=(groups, t // tm, IN_PAD // IN_TN),
        in_specs=[pl.BlockSpec((None, tm, D_MODEL), lambda g, i, j: (g, i, 0)),
                  pl.BlockSpec((1, D_MODEL), lambda g, i, j: (0, 0)),
                  _mod_spec(mod, tm, 0), _mod_spec(mod, tm, 1),
                  pl.BlockSpec((tm, LANES), lambda g, i, j: (i, 0)),
                  pl.BlockSpec((tm, LANES), lambda g, i, j: (i, 0)),
                  pl.BlockSpec((pl.Element(1), pl.Element(IN_TN), pl.Element(D_MODEL)), weight_rows)],
        out_specs=pl.BlockSpec((None, tm, IN_TN), lambda g, i, j: (g, i, j)),
        out_shape=jax.ShapeDtypeStruct((groups, t, IN_PAD), F32),
        scratch_shapes=[pltpu.VMEM((tm, D_MODEL), BF16)],
        compiler_params=_cparams("parallel", "parallel", "arbitrary"),
        name="in_proj",
    )(x, g1, mod, mod, cos, sin, w_in_t)


def _rope_tables(pos):
    half = HEAD_DIM // 2
    inv = ROPE_THETA ** (-jnp.arange(half, dtype=F32) * 2.0 / HEAD_DIM)
    ang = pos.astype(F32)[:, None] * inv[None, :]
    cos, sin = jnp.cos(ang), jnp.sin(ang)
    reps = LANES // HEAD_DIM
    cos_t = jnp.concatenate([cos, cos] * reps, axis=1)
    sin_t = jnp.concatenate([-sin, sin] * reps, axis=1)
    return cos_t, sin_t


def _attn_prompt_kernel(sink_ref, q_ref, k_ref, v_ref, kp_ref, vp_ref, o_ref):
    n = pl.program_id(1)
    blk = q_ref.shape[0]
    q = q_ref[...]
    k2 = jnp.concatenate([kp_ref[...], k_ref[...]], axis=0).astype(BF16)
    v2 = jnp.concatenate([vp_ref[...], v_ref[...]], axis=0).astype(BF16)
    qi = lax.broadcasted_iota(jnp.int32, (blk, 2 * blk), 0)
    kc = lax.broadcasted_iota(jnp.int32, (blk, 2 * blk), 1)
    rel = qi - kc + blk
    first_key = jnp.where(n > 0, 0, blk)
    mask = (rel >= 0) & (rel <= WINDOW) & (kc >= first_key)
    outs = []
    for hd in range(N_Q_HEADS):
        h = hd // GROUP
        kh = k2[:, h * HEAD_DIM:(h + 1) * HEAD_DIM]
        vh = v2[:, h * HEAD_DIM:(h + 1) * HEAD_DIM]
        qh = q[:, hd * HEAD_DIM:(hd + 1) * HEAD_DIM].astype(BF16)
        s = lax.dot_general(qh, kh, (((1,), (1,)), ((), ())), preferred_element_type=F32) * (HEAD_DIM ** -0.5)
        s = jnp.where(mask, s, NEG_INF)
        sk = sink_ref[hd]
        m = jnp.maximum(jnp.max(s, axis=1, keepdims=True), sk)
        p = jnp.exp(s - m)
        den = jnp.sum(p, axis=1, keepdims=True) + jnp.exp(sk - m)
        outs.append(jnp.dot(p.astype(BF16), vh, preferred_element_type=F32) / den)
    o_ref[...] = jnp.concatenate(outs, axis=1).astype(o_ref.dtype)


def _attn_prompt(z, sink):
    groups, t, _ = z.shape
    blk = WINDOW
    kcol, vcol = K_OFF // KV_WIDTH, V_OFF // KV_WIDTH
    prev = lambda col: (lambda g, n: (g, jnp.maximum(n - 1, 0), col))
    return pl.pallas_call(
        _attn_prompt_kernel,
        grid=(groups, t // blk),
        in_specs=[pl.BlockSpec(memory_space=pltpu.SMEM),
                  pl.BlockSpec((None, blk, ATTN_WIDTH), lambda g, n: (g, n, 0)),
                  pl.BlockSpec((None, blk, KV_WIDTH), lambda g, n: (g, n, kcol)),
                  pl.BlockSpec((None, blk, KV_WIDTH), lambda g, n: (g, n, vcol)),
                  pl.BlockSpec((None, blk, KV_WIDTH), prev(kcol)),
                  pl.BlockSpec((None, blk, KV_WIDTH), prev(vcol))],
        out_specs=pl.BlockSpec((None, blk, ATTN_WIDTH), lambda g, n: (g, n, 0)),
        out_shape=jax.ShapeDtypeStruct((groups, t, ATTN_WIDTH), BF16),
        compiler_params=_cparams("parallel", "parallel"),
        name="attn_prompt",
    )(sink, z, z, z, z, z)


def _attn_sample_kernel(sink_ref, q_ref, kn_ref, vn_ref, knc_ref, vnc_ref, kt_ref, vt_ref, o_ref, kto_ref, vto_ref):
    bb = q_ref.shape[0]
    win = kt_ref.shape[2]
    row = lax.broadcasted_iota(jnp.int32, (N_Q_HEADS, KV_WIDTH), 0)
    lane = lax.broadcasted_iota(jnp.int32, (N_Q_HEADS, KV_WIDTH), 1)
    own = (row // GROUP) == (lane // HEAD_DIM)
    newest = lax.broadcasted_iota(jnp.int32, (KV_WIDTH, win), 1) == win - 1
    sink = sink_ref[...]
    scale = HEAD_DIM ** -0.5

    def body(b, carry):
        q = q_ref[b]
        kt, vt = kt_ref[b], vt_ref[b]
        qbd = jnp.where(own, jnp.concatenate([q] * N_KV_HEADS, axis=1), 0.0)
        s = jnp.dot(qbd.astype(BF16), kt.astype(BF16), preferred_element_type=F32) * scale
        s_new = jnp.sum(qbd * kn_ref[b], axis=1, keepdims=True) * scale
        m = jnp.maximum(jnp.maximum(jnp.max(s, axis=1, keepdims=True), s_new), sink)
        p = jnp.exp(s - m)
        p_new = jnp.exp(s_new - m)
        den = jnp.sum(p, axis=1, keepdims=True) + p_new + jnp.exp(sink - m)
        o = lax.dot_general(p.astype(BF16), vt.astype(BF16), (((1,), (1,)), ((), ())),
                            preferred_element_type=F32) + p_new * vn_ref[b]
        o = jnp.where(own, o, 0.0)
        o64 = o[:, 0:HEAD_DIM]
        for h in range(1, N_KV_HEADS):
            o64 = o64 + o[:, h * HEAD_DIM:(h + 1) * HEAD_DIM]
        o_ref[b] = o64 / den
        kto_ref[b] = jnp.where(newest, knc_ref[b], pltpu.roll(kt, win - 1, 1))
        vto_ref[b] = jnp.where(newest, vnc_ref[b], pltpu.roll(vt, win - 1, 1))
        return carry

    lax.fori_loop(0, bb, body, 0)


def _attn_sample(q3, k_new, v_new, kt_all, vt_all, l, sink):
    _, nb, _, win = kt_all.shape
    assert win == LANES
    bb = _row_tile(nb, 8)
    row = pl.BlockSpec((bb, 1, KV_WIDTH), lambda i: (i, 0, 0))
    col = pl.BlockSpec((bb, KV_WIDTH, 1), lambda i: (i, 0, 0))
    cache_in = pl.BlockSpec((None, bb, KV_WIDTH, win), lambda i: (l, i, 0, 0))
    cache_out = pl.BlockSpec((bb, KV_WIDTH, win), lambda i: (i, 0, 0))
    heads = pl.BlockSpec((bb, N_Q_HEADS, HEAD_DIM), lambda i: (i, 0, 0))
    return pl.pallas_call(
        _attn_sample_kernel,
        grid=(nb // bb,),
        in_specs=[pl.BlockSpec((N_Q_HEADS, 1), lambda i: (0, 0)), heads, row, row, col, col, cache_in, cache_in],
        out_specs=[heads, cache_out, cache_out],
        out_shape=[jax.ShapeDtypeStruct((nb, N_Q_HEADS, HEAD_DIM), F32),
                   jax.ShapeDtypeStruct((nb, KV_WIDTH, win), F32),
                   jax.ShapeDtypeStruct((nb, KV_WIDTH, win), F32)],
        compiler_params=_cparams("parallel"),
        name="attn_sample",
    )(sink.reshape(N_Q_HEADS, 1), q3, k_new.reshape(nb, 1, KV_WIDTH), v_new.reshape(nb, 1, KV_WIDTH),
      k_new.reshape(nb, KV_WIDTH, 1), v_new.reshape(nb, KV_WIDTH, 1), kt_all, vt_all)


POOL_HALO = 16


def _pool_prompt_kernel(u_ref, halo_ref, w_ref, sc_ref, o_ref, ext_ref):
    i = pl.program_id(1)
    tm = u_ref.shape[0]
    ext_ref[0:POOL_HALO, :] = jnp.where(i > 0, halo_ref[...], 0.0)
    ext_ref[POOL_HALO:POOL_HALO + tm, :] = u_ref[...]
    pos = i * tm + lax.broadcasted_iota(jnp.int32, (tm, 1), 0)
    for gi, w in enumerate(POOL_WINDOWS):
        cols = slice(gi * POOL_GROUP_WIDTH, (gi + 1) * POOL_GROUP_WIDTH)
        u = ext_ref[POOL_HALO:POOL_HALO + tm, cols]
        acc = u
        for back in range(1, w):
            acc = acc + ext_ref[POOL_HALO - back:POOL_HALO - back + tm, cols]
        cnt = jnp.minimum(pos + 1, w).astype(F32)
        d = acc / cnt - u
        o_ref[:, cols] = _bdot(d, w_ref[gi]) * sc_ref[:, cols]


def _pool_prompt(z, pool_w, pool_scale, tm):
    groups, t, _ = z.shape
    ucol = U_OFF // POOL_WIDTH
    per = tm // POOL_HALO
    return pl.pallas_call(
        _pool_prompt_kernel,
        grid=(groups, t // tm),
        in_specs=[pl.BlockSpec((None, tm, POOL_WIDTH), lambda g, i: (g, i, ucol)),
                  pl.BlockSpec((None, POOL_HALO, POOL_WIDTH), lambda g, i: (g, jnp.maximum(i * per - 1, 0), ucol)),
                  pl.BlockSpec((len(POOL_WINDOWS), POOL_GROUP_WIDTH, POOL_GROUP_WIDTH), lambda g, i: (0, 0, 0)),
                  pl.BlockSpec((1, POOL_WIDTH), lambda g, i: (0, 0))],
        out_specs=pl.BlockSpec((None, tm, POOL_WIDTH), lambda g, i: (g, i, 0)),
        out_shape=jax.ShapeDtypeStruct((groups, t, POOL_WIDTH), F32),
        scratch_shapes=[pltpu.VMEM((tm + POOL_HALO, POOL_WIDTH), F32)],
        compiler_params=_cparams("parallel", "parallel"),
        name="pool_prompt",
    )(z, z, pool_w, pool_scale)


def _pool_sample_kernel(u_ref, st_ref, w_ref, sc_ref, o_ref):
    for gi, w in enumerate(POOL_WINDOWS):
        cols = slice(gi * POOL_GROUP_WIDTH, (gi + 1) * POOL_GROUP_WIDTH)
        u = u_ref[:, cols]
        acc = u
        for back in range(1, w):
            acc = acc + st_ref[POOL_PAD - back, :, cols]
        cnt = float(min(PAST_LEN + 1, w))
        d = acc / cnt - u
        o_ref[:, cols] = _bdot(d, w_ref[gi]) * sc_ref[:, cols]


def _pool_sample(z, state_t, pool_w, pool_scale):
    rows = z.shape[1]
    ucol = U_OFF // POOL_WIDTH
    return pl.pallas_call(
        _pool_sample_kernel,
        grid=(1,),
        in_specs=[pl.BlockSpec((None, rows, POOL_WIDTH), lambda i: (0, 0, ucol)),
                  pl.BlockSpec((POOL_PAD, rows, POOL_WIDTH), lambda i: (0, 0, 0)),
                  pl.BlockSpec((len(POOL_WINDOWS), POOL_GROUP_WIDTH, POOL_GROUP_WIDTH), lambda i: (0, 0, 0)),
                  pl.BlockSpec((1, POOL_WIDTH), lambda i: (0, 0))],
        out_specs=pl.BlockSpec((None, rows, POOL_WIDTH), lambda i: (0, 0, 0)),
        out_shape=jax.ShapeDtypeStruct((1, rows, POOL_WIDTH), F32),
        compiler_params=_cparams("arbitrary"),
        name="pool_sample",
    )(z, state_t, pool_w, pool_scale)


PREP_HALO = 8


def _rwkv_prep_kernel(*refs, halo, has_vres):
    it = iter(refs)
    pr_ref, prev_ref = next(it), next(it)
    vf_ref = next(it) if has_vres else None
    mu_ref, w0_ref, a0_ref, kk_ref, ka_ref, rk_ref = (next(it) for _ in range(6))
    w2_ref, a2_ref, g2_ref, ones_ref = (next(it) for _ in range(4))
    if has_vres:
        vw1_ref, vw2_ref, vb_ref = next(it), next(it), next(it)
    r_o, w_o, k_o, v_o, a_o, b_o, g_o, bonus_o = (next(it) for _ in range(8))
    ext_ref = next(it) if halo else None

    pr = pr_ref[...]
    tm = pr.shape[0]
    if halo:
        i = pl.program_id(1)
        ext_ref[0:PREP_HALO, :] = jnp.where(i > 0, prev_ref[...], 0.0)
        ext_ref[PREP_HALO:PREP_HALO + tm, :] = pr
        prev = ext_ref[PREP_HALO - 1:PREP_HALO - 1 + tm, :]
    else:
        prev = prev_ref[...]
    xm = pr + (prev - pr) * mu_ref[...]
    wd = RWKV_WIDTH
    r, k, v, lora_in = xm[:, 0:wd], xm[:, wd:2 * wd], xm[:, 2 * wd:3 * wd], xm[:, 3 * wd:4 * wd]
    ones_bd = ones_ref[...]
    lw = _bdot(jnp.tanh(lora_in), w2_ref[...])
    la = _bdot(lora_in, a2_ref[...])
    g = _bdot(_sigmoid(lora_in), g2_ref[...])
    y = -(w0_ref[...] + lw)
    softplus = jnp.maximum(y, 0.0) + jnp.log(1.0 + jnp.exp(-jnp.abs(y)))
    decay = jnp.exp(-jnp.exp(-softplus - 0.5))
    a = _sigmoid(a0_ref[...] + la)
    if has_vres:
        t2 = _bdot(_bdot(v, vw1_ref[...]), vw2_ref[...])
        v = v + (vf_ref[...] - v) * _sigmoid(vb_ref[...] + t2)
    kk = k * kk_ref[...]
    kk = kk / jnp.maximum(jnp.sqrt(_seg_sum(kk * kk, ones_bd)), 1e-12)
    k = k * (1.0 + (a - 1.0) * ka_ref[...])
    r_o[...] = r
    w_o[...] = decay
    k_o[...] = k
    v_o[...] = v
    a_o[...] = -kk
    b_o[...] = kk * a
    g_o[...] = g
    bonus_o[...] = _seg_sum(r * k * rk_ref[...], ones_bd) * v


def _rwkv_prep(z, prev, v_first, rp, tm):
    groups, t, _ = z.shape
    halo = prev is None
    has_vres = v_first is not None
    prcol = PR_OFF // RWKV_PROJ_PAD
    wd = RWKV_WIDTH
    tok = lambda width: pl.BlockSpec((None, tm, width), lambda g, i: (g, i, 0))
    vec = lambda width: pl.BlockSpec((1, width), lambda g, i: (0, 0))
    mat = lambda a, b: pl.BlockSpec((a, b), lambda g, i: (0, 0))
    args = [z]
    in_specs = [pl.BlockSpec((None, tm, RWKV_PROJ_PAD), lambda g, i: (g, i, prcol))]
    if halo:
        per = tm // PREP_HALO
        args.append(z)
        in_specs.append(pl.BlockSpec((None, PREP_HALO, RWKV_PROJ_PAD),
                                     lambda g, i: (g, jnp.maximum(i * per - 1, 0), prcol)))
    else:
        args.append(prev)
        in_specs.append(tok(RWKV_PROJ_PAD))
    if has_vres:
        args.append(v_first)
        in_specs.append(tok(wd))
    args += [rp["mu"], rp["w0"], rp["a0"], rp["k_k"], rp["k_a"], rp["r_k"],
             rp["w2"], rp["a2"], rp["g2"], rp["ones_bd"]]
    in_specs += [vec(RWKV_PROJ_PAD)] + [vec(wd)] * 5 + [mat(wd, wd)] * 4
    if has_vres:
        args += [rp["vw1"], rp["vw2"], rp["vb"]]
        in_specs += [mat(wd, LANES), mat(LANES, wd), vec(wd)]
    return pl.pallas_call(
        functools.partial(_rwkv_prep_kernel, halo=halo, has_vres=has_vres),
        grid=(groups, t // tm),
        in_specs=in_specs,
        out_specs=[tok(wd)] * 8,
        out_shape=[jax.ShapeDtypeStruct((groups, t, wd), F32)] * 8,
        scratch_shapes=[pltpu.VMEM((tm + PREP_HALO, RWKV_PROJ_PAD), F32)] if halo else [],
        compiler_params=_cparams("parallel", "parallel"),
        name="rwkv_prep",
    )(*args)


def _wkv_scan_kernel(r_ref, w_ref, k_ref, v_ref, a_ref, b_ref, y_ref, s_ref):
    nb, tc, _ = r_ref.shape
    pairs = RWKV_HEADS // 2
    n = RWKV_HEAD_DIM

    @pl.when(pl.program_id(0) == 0)
    def _():
        s_ref[...] = jnp.zeros(s_ref.shape, F32)

    row = lax.broadcasted_iota(jnp.int32, (n, 2 * n), 0)
    lane = lax.broadcasted_iota(jnp.int32, (n, 2 * n), 1)
    diag = (lane % n) == row
    chains = [(bi, slice(p * 2 * n, (p + 1) * 2 * n)) for bi in range(nb) for p in range(pairs)]
    nc = len(chains)
    seg_r = (lax.broadcasted_iota(jnp.int32, (4 * n, 2 * n), 0) % (2 * n)) // n
    seg_c = lax.broadcasted_iota(jnp.int32, (4 * n, 2 * n), 1) // n
    ones_bd2 = (seg_r == seg_c).astype(BF16)
    ones_bd = ones_bd2[:2 * n]
    stack = lambda xs: jnp.concatenate(xs, axis=0)
    unstack = lambda x: [x[c * n:(c + 1) * n] for c in range(nc)]

    def seg_bcast(x, exact):
        hi = x.astype(BF16)
        if not exact:
            return jnp.dot(hi, ones_bd, preferred_element_type=F32)
        lo = (x - hi.astype(F32)).astype(BF16)
        return jnp.dot(jnp.concatenate([hi, lo], axis=1), ones_bd2, preferred_element_type=F32)

    def steps(t8, carry):
        base = pl.multiple_of(t8 * SUBLANES, SUBLANES)
        rows = pl.ds(base, SUBLANES)
        tiles = [[ref[bi, rows, sl] for ref in (r_ref, w_ref, k_ref, v_ref, a_ref, b_ref)] for bi, sl in chains]
        s = [s_ref[c] for c in range(nc)]
        ys = [[] for _ in chains]
        for u in range(SUBLANES):
            one = slice(u, u + 1)
            vcol = unstack(seg_bcast(stack([jnp.where(diag, t[3][one], 0.0) for t in tiles]), False))
            sa = unstack(seg_bcast(stack([s[c] * tiles[c][4][one] for c in range(nc)]), True))
            for c, (r8, w8, k8, v8, a8, b8) in enumerate(tiles):
                s[c] = s[c] * w8[one] + sa[c] * b8[one] + vcol[c] * k8[one]
            ycol = unstack(seg_bcast(stack([s[c] * tiles[c][0][one] for c in range(nc)]), False))
            for c in range(nc):
                ys[c].append(jnp.sum(jnp.where(diag, ycol[c], 0.0), axis=0, keepdims=True))
        for c, (bi, sl) in enumerate(chains):
            s_ref[c] = s[c]
            y_ref[bi, rows, sl] = jnp.concatenate(ys[c], axis=0)
        return carry

    lax.fori_loop(0, tc // SUBLANES, steps, 0)


def _wkv_scan(r, w, k, v, a, b, tc):
    nb, t, wd = r.shape
    pairs = RWKV_HEADS // 2
    tok = pl.BlockSpec((nb, tc, wd), lambda i: (0, i, 0))
    return pl.pallas_call(
        _wkv_scan_kernel,
        grid=(t // tc,),
        in_specs=[tok] * 6,
        out_specs=[tok, pl.BlockSpec((nb * pairs, RWKV_HEAD_DIM, 2 * RWKV_HEAD_DIM), lambda i: (0, 0, 0))],
        out_shape=[jax.ShapeDtypeStruct((nb, t, wd), F32),
                   jax.ShapeDtypeStruct((nb * pairs, RWKV_HEAD_DIM, 2 * RWKV_HEAD_DIM), F32)],
        compiler_params=_cparams("arbitrary"),
        name="wkv_scan",
    )(r, w, k, v, a, b)


def _wkv_step_kernel(r_ref, w_ref, k_ref, v_ref, a_ref, b_ref, s_ref, y_ref, so_ref):
    heads = s_ref.shape[0]
    n = RWKV_HEAD_DIM
    diag = lax.broadcasted_iota(jnp.int32, (n, n), 0) == lax.broadcasted_iota(jnp.int32, (n, n), 1)

    def body(h8, carry):
        base = pl.multiple_of(h8 * SUBLANES, SUBLANES)
        rows = pl.ds(base, SUBLANES)
        r8, w8, k8, v8, a8, b8 = (ref[rows, :] for ref in (r_ref, w_ref, k_ref, v_ref, a_ref, b_ref))
        y8 = []
        for u in range(SUBLANES):
            one = slice(u, u + 1)
            s = s_ref[base + u]
            sa = jnp.sum(s * a8[one], axis=1, keepdims=True)
            vcol = jnp.sum(jnp.where(diag, v8[one], 0.0), axis=1, keepdims=True)
            s = s * w8[one] + sa * b8[one] + vcol * k8[one]
            so_ref[base + u] = s
            ycol = jnp.sum(s * r8[one], axis=1, keepdims=True)
            y8.append(jnp.sum(jnp.where(diag, ycol, 0.0), axis=0, keepdims=True))
        y_ref[rows, :] = jnp.concatenate(y8, axis=0)
        return carry

    lax.fori_loop(0, heads // SUBLANES, body, 0)


def _wkv_step(r, w, k, v, a, b, state, l):
    heads = state.shape[1]
    hb = _row_tile(heads, 128)
    n = RWKV_HEAD_DIM
    vec = pl.BlockSpec((hb, n), lambda i: (i, 0))
    st = pl.BlockSpec((hb, n, n), lambda i: (i, 0, 0))
    return pl.pallas_call(
        _wkv_step_kernel,
        grid=(heads // hb,),
        in_specs=[vec] * 6 + [pl.BlockSpec((None, hb, n, n), lambda i: (l, i, 0, 0))],
        out_specs=[vec, st],
        out_shape=[jax.ShapeDtypeStruct((heads, n), F32), jax.ShapeDtypeStruct((heads, n, n), F32)],
        compiler_params=_cparams("parallel"),
        name="wkv_step",
    )(r, w, k, v, a, b, state)


def _rwkv_post_kernel(y_ref, bonus_ref, g_ref, lg_ref, lb_ref, ones_ref, o_ref):
    y = y_ref[...]
    ones_bd = ones_ref[...]
    inv = 1.0 / RWKV_HEAD_DIM
    d = y - _seg_sum(y, ones_bd) * inv
    var = _seg_sum(d * d, ones_bd) * inv
    yn = d * lax.rsqrt(var + LNX_EPS) * lg_ref[...] + lb_ref[...]
    o_ref[...] = ((yn + bonus_ref[...]) * g_ref[...]).astype(o_ref.dtype)


def _rwkv_post(y, bonus, g, rp, tm):
    groups, t, wd = y.shape
    tok = pl.BlockSpec((None, tm, wd), lambda gi, i: (gi, i, 0))
    vec = pl.BlockSpec((1, wd), lambda gi, i: (0, 0))
    return pl.pallas_call(
        _rwkv_post_kernel,
        grid=(groups, t // tm),
        in_specs=[tok, tok, tok, vec, vec, pl.BlockSpec((wd, wd), lambda gi, i: (0, 0))],
        out_specs=tok,
        out_shape=jax.ShapeDtypeStruct((groups, t, wd), BF16),
        compiler_params=_cparams("parallel", "parallel"),
        name="rwkv_post",
    )(y, bonus, g, rp["lnx_g"], rp["lnx_b"], rp["ones_bd"])


MERGE_TN = 512


def _merge_kernel(oa_ref, zp_ref, yr_ref, ga_ref, gp_ref, gr_ref, wa_ref, wp_ref, wr_ref, o_ref):
    merged = (_sigmoid(ga_ref[...]) * _bdot(oa_ref[...], wa_ref[...])
              + _sigmoid(gp_ref[...]) * _bdot(zp_ref[...], wp_ref[...])
              + _sigmoid(gr_ref[...]) * _bdot(yr_ref[...], wr_ref[...]))
    o_ref[...] = merged.astype(o_ref.dtype)


def _merge(o_attn, z_pool, y_rwkv, z, w_attn_out, w_pool_out, w_rwkv_out, l, tm):
    groups, t, _ = z.shape
    tn = MERGE_TN
    gate = lambda br: pl.BlockSpec((None, tm, tn), lambda g, i, j: (g, i, (GL_OFF + br * D_MODEL) // tn + j))
    tok = lambda width: pl.BlockSpec((None, tm, width), lambda g, i, j: (g, i, 0))
    wt = lambda rows: pl.BlockSpec((None, rows, tn), lambda g, i, j: (l, 0, j))
    return pl.pallas_call(
        _merge_kernel,
        grid=(groups, t // tm, D_MODEL // tn),
        in_specs=[tok(ATTN_WIDTH), tok(POOL_WIDTH), tok(RWKV_WIDTH), gate(0), gate(1), gate(2),
                  wt(ATTN_WIDTH), wt(POOL_WIDTH), wt(RWKV_WIDTH)],
        out_specs=pl.BlockSpec((None, tm, tn), lambda g, i, j: (g, i, j)),
        out_shape=jax.ShapeDtypeStruct((groups, t, D_MODEL), BF16),
        compiler_params=_cparams("parallel", "parallel", "arbitrary"),
        name="merge",
    )(o_attn, z_pool, y_rwkv, z, z, z, w_attn_out, w_pool_out, w_rwkv_out)


def _proj_residual_kernel(a_ref, w_ref, x_ref, gt_ref, o_ref):
    o_ref[...] = x_ref[...] + (1.0 + gt_ref[...]) * _bdot(a_ref[...], w_ref[...])


def _proj_residual(a, w, l, x, mod, chunk, tm, tn):
    groups, t, kdim = a.shape
    per_row = mod.shape[1] != 1
    gate = pl.BlockSpec((None, tm if per_row else 1, tn),
                        lambda g, i, j: (g, i if per_row else 0, chunk * (D_MODEL // tn) + j))
    return pl.pallas_call(
        _proj_residual_kernel,
        grid=(groups, t // tm, D_MODEL // tn),
        in_specs=[pl.BlockSpec((None, tm, kdim), lambda g, i, j: (g, i, 0)),
                  pl.BlockSpec((None, kdim, tn), lambda g, i, j: (l, 0, j)),
                  pl.BlockSpec((None, tm, tn), lambda g, i, j: (g, i, j)),
                  gate],
        out_specs=pl.BlockSpec((None, tm, tn), lambda g, i, j: (g, i, j)),
        out_shape=jax.ShapeDtypeStruct((groups, t, D_MODEL), F32),
        compiler_params=_cparams("parallel", "parallel", "arbitrary"),
        name="proj_residual",
    )(a, w, x, mod)


FFN_TN = 256


def _ffn_up_kernel(x_ref, g_ref, sh_ref, sc_ref, wg_ref, wu_ref, o_ref, h_ref):
    @pl.when(pl.program_id(2) == 0)
    def _():
        h_ref[...] = _mod_norm(x_ref[...], g_ref[...], sh_ref[...], sc_ref[...]).astype(BF16)

    h = h_ref[...]
    gate = jnp.dot(h, wg_ref[...].astype(BF16), preferred_element_type=F32)
    up = jnp.dot(h, wu_ref[...].astype(BF16), preferred_element_type=F32)
    o_ref[...] = (gate * _sigmoid(gate) * up).astype(o_ref.dtype)


def _ffn_up(x, g2, mod, w_gate, w_up, l, tm):
    groups, t, _ = x.shape
    tn = FFN_TN
    wt = pl.BlockSpec((None, D_MODEL, tn), lambda g, i, j: (l, 0, j))
    return pl.pallas_call(
        _ffn_up_kernel,
        grid=(groups, t // tm, D_FF // tn),
        in_specs=[pl.BlockSpec((None, tm, D_MODEL), lambda g, i, j: (g, i, 0)),
                  pl.BlockSpec((1, D_MODEL), lambda g, i, j: (0, 0)),
                  _mod_spec(mod, tm, 3), _mod_spec(mod, tm, 4), wt, wt],
        out_specs=pl.BlockSpec((None, tm, tn), lambda g, i, j: (g, i, j)),
        out_shape=jax.ShapeDtypeStruct((groups, t, D_FF), BF16),
        scratch_shapes=[pltpu.VMEM((tm, D_MODEL), BF16)],
        compiler_params=_cparams("parallel", "parallel", "arbitrary"),
        name="ffn_up",
    )(x, g2, mod, mod, w_gate, w_up)


def _final_norm_kernel(x_ref, g_ref, o_ref):
    x = x_ref[...]
    ms = jnp.mean(x * x, axis=-1, keepdims=True)
    o_ref[...] = x * lax.rsqrt(ms + RMS_EPS) * g_ref[...]


def _final_norm(x, g, tm):
    groups, t, _ = x.shape
    tok = pl.BlockSpec((None, tm, D_MODEL), lambda gi, i: (gi, i, 0))
    return pl.pallas_call(
        _final_norm_kernel,
        grid=(groups, t // tm),
        in_specs=[tok, pl.BlockSpec((1, D_MODEL), lambda gi, i: (0, 0))],
        out_specs=tok,
        out_shape=jax.ShapeDtypeStruct(x.shape, F32),
        compiler_params=_cparams("parallel", "parallel"),
        name="final_norm",
    )(x, g)


def _pad_cols(a, width):
    return jnp.pad(a, [(0, 0)] * (a.ndim - 1) + [(0, width - a.shape[-1])])


def _rwkv_params(l, rwkv_mu, rwkv_w0, rwkv_w2, rwkv_a0, rwkv_a2, rwkv_g2, rwkv_k_k, rwkv_k_a, rwkv_r_k,
                 rwkv_lnx_g, rwkv_lnx_b, vres_w1, vres_w2, vres_b):
    wd = RWKV_WIDTH
    row = lambda a: a.reshape(1, -1)
    lora_rows = lambda w, off: jnp.zeros((wd, wd), F32).at[off:off + w.shape[0]].set(w).astype(BF16)
    seg = np.arange(wd) // RWKV_HEAD_DIM
    rp = dict(
        mu=_pad_cols(row(rwkv_mu[l]), RWKV_PROJ_PAD), w0=row(rwkv_w0[l]), a0=row(rwkv_a0[l]),
        k_k=row(rwkv_k_k[l]), k_a=row(rwkv_k_a[l]), r_k=row(rwkv_r_k[l]),
        w2=lora_rows(rwkv_w2[l], 0), a2=lora_rows(rwkv_a2[l], DECAY_LORA),
        g2=lora_rows(rwkv_g2[l], DECAY_LORA + ICLR_LORA),
        ones_bd=jnp.asarray(seg[:, None] == seg[None, :], BF16),
        lnx_g=row(rwkv_lnx_g[l]), lnx_b=row(rwkv_lnx_b[l]))
    if l > 0:
        rp.update(vw1=_pad_cols(vres_w1[l - 1], LANES).astype(BF16),
                  vw2=jnp.pad(vres_w2[l - 1], ((0, LANES - VRES_LORA), (0, 0))).astype(BF16),
                  vb=row(vres_b[l - 1]))
    return rp


def _layer_common(x, mod, z, o_attn, z_pool, y, bonus, g, lw, l, rp, tm):
    y_rwkv = _rwkv_post(y, bonus, g, rp, _row_tile(x.shape[1], 512))
    merged = _merge(o_attn, z_pool, y_rwkv, z, lw["w_attn_out"], lw["w_pool_out"], lw["w_rwkv_out"], l, tm)
    x = _proj_residual(merged, lw["w_o"], l, x, mod, 2, tm, 512)
    hidden = _ffn_up(x, lw["norm2_g"][l].reshape(1, -1), mod, lw["w_ffn_gate"], lw["w_ffn_up"], l, tm)
    return _proj_residual(hidden, lw["w_ffn_down"], l, x, mod, 5, tm, 256)


def kernel(x_prompt, x_sample, c_prompt, c_sample, cache_k_win, cache_v_win, state_pool, state_shift, state_wkv, ada_w, ada_b, norm1_g, norm2_g, w_in, attn_sink, w_attn_out, pool_w, pool_scale, w_pool_out, rwkv_mu, rwkv_w0, rwkv_w2, rwkv_a0, rwkv_a2, rwkv_g2, rwkv_k_k, rwkv_k_a, rwkv_r_k, rwkv_lnx_g, rwkv_lnx_b, w_rwkv_out, vres_w1, vres_w2, vres_b, w_o, w_ffn_gate, w_ffn_up, w_ffn_down, final_norm_g):
    depth = ada_w.shape[0]
    nb, t, _ = x_prompt.shape
    ns = x_sample.shape[0]
    assert x_sample.shape[1] == 1 and t % WINDOW == 0
    win = cache_k_win.shape[2]
    assert win <= WINDOW
    heads, n = RWKV_HEADS, RWKV_HEAD_DIM

    pad_rows = -nb % SUBLANES
    c_all = jnp.concatenate([c_prompt, jnp.zeros((pad_rows, D_MODEL), F32), c_sample], axis=0)
    mod_all = _ada(c_all, ada_w, ada_b)

    cos_p, sin_p = _rope_tables(jnp.arange(t))
    cos_s, sin_s = _rope_tables(jnp.full((ns,), PAST_LEN))

    xp = x_prompt
    xs = x_sample.reshape(1, ns, D_MODEL)
    tm_p = _row_tile(t, 1024)
    tm_s = _row_tile(ns, 1024)
    vf_p = vf_s = None
    st_p, st_s = [], []
    lw = dict(norm2_g=norm2_g, w_attn_out=w_attn_out, w_pool_out=w_pool_out, w_rwkv_out=w_rwkv_out, w_o=w_o,
              w_ffn_gate=w_ffn_gate, w_ffn_up=w_ffn_up, w_ffn_down=w_ffn_down)
    w_in_t = w_in.transpose(0, 2, 1)
    to_feature_major = lambda c: c.transpose(0, 1, 3, 4, 2).reshape(depth, ns, KV_WIDTH, win)
    from_feature_major = lambda c: c.reshape(ns, N_KV_HEADS, HEAD_DIM, win).transpose(0, 3, 1, 2)
    cache_kt, cache_vt = to_feature_major(cache_k_win), to_feature_major(cache_v_win)
    state_heads = state_wkv.reshape(depth, ns * heads, n, n)
    for l in range(depth):
        rp = _rwkv_params(l, rwkv_mu, rwkv_w0, rwkv_w2, rwkv_a0, rwkv_a2, rwkv_g2, rwkv_k_k, rwkv_k_a, rwkv_r_k,
                          rwkv_lnx_g, rwkv_lnx_b, vres_w1, vres_w2, vres_b)
        g1 = norm1_g[l].reshape(1, -1)
        mod_p = mod_all[l, :nb].reshape(nb, 1, -1)
        mod_s = mod_all[l, nb + pad_rows:].reshape(1, ns, -1)

        z = _in_proj(xp, g1, mod_p, w_in_t, l, cos_p, sin_p, tm_p)
        o_attn = _attn_prompt(z, attn_sink[l])
        z_pool = _pool_prompt(z, pool_w[l], pool_scale[l].reshape(1, -1), _row_tile(t, 512))
        r, w, k, v, a, b, g, bonus = _rwkv_prep(z, None, vf_p, rp, _row_tile(t, 256))
        if l == 0:
            vf_p = v
        y, s_pair = _wkv_scan(r, w, k, v, a, b, _row_tile(t, 256))
        wkv_new = s_pair.reshape(nb, heads // 2, n, 2, n).transpose(0, 1, 3, 2, 4).reshape(nb, heads, n, n)
        st_p.append((z[:, t - WINDOW:, K_OFF:V_OFF].reshape(nb, WINDOW, N_KV_HEADS, HEAD_DIM),
                     z[:, t - WINDOW:, V_OFF:U_OFF].reshape(nb, WINDOW, N_KV_HEADS, HEAD_DIM),
                     z[:, t - POOL_PAD:, U_OFF:PR_OFF],
                     z[:, t - 1, PR_OFF:PR_OFF + RWKV_PROJ],
                     wkv_new))
        xp = _layer_common(xp, mod_p, z, o_attn, z_pool, y, bonus, g, lw, l, rp, tm_p)

        z = _in_proj(xs, g1, mod_s, w_in_t, l, cos_s, sin_s, tm_s)
        k_new, v_new = z[0, :, K_OFF:V_OFF], z[0, :, V_OFF:U_OFF]
        o3, kt_new, vt_new = _attn_sample(z[0, :, :ATTN_WIDTH].reshape(ns, N_Q_HEADS, HEAD_DIM), k_new, v_new,
                                          cache_kt, cache_vt, l, attn_sink[l])
        o_attn = o3.reshape(1, ns, ATTN_WIDTH).astype(BF16)
        z_pool = _pool_sample(z, state_pool[l].transpose(1, 0, 2), pool_w[l], pool_scale[l].reshape(1, -1))
        prev = _pad_cols(state_shift[l], RWKV_PROJ_PAD).reshape(1, ns, RWKV_PROJ_PAD)
        r, w, k, v, a, b, g, bonus = _rwkv_prep(z, prev, vf_s, rp, tm_s)
        if l == 0:
            vf_s = v
        per_head = lambda arr: arr.reshape(ns * heads, n)
        y, wkv_new = _wkv_step(*(per_head(arr) for arr in (r, w, k, v, a, b)), state_heads, l)
        u_new = z[0, :, None, U_OFF:PR_OFF]
        st_s.append((from_feature_major(kt_new), from_feature_major(vt_new),
                     jnp.concatenate([state_pool[l][:, 1:], u_new], axis=1),
                     z[0, :, PR_OFF:PR_OFF + RWKV_PROJ],
                     wkv_new.reshape(ns, heads, n, n)))
        xs = _layer_common(xs, mod_s, z, o_attn, z_pool, y.reshape(1, ns, RWKV_WIDTH), bonus, g, lw, l, rp, tm_s)

    g_fin = final_norm_g.reshape(1, -1)
    y_prompt = _final_norm(xp, g_fin, tm_p)
    y_sample = _final_norm(xs, g_fin, tm_s).reshape(ns, 1, D_MODEL)
    stack = lambda states, i: jnp.stack([s[i] for s in states])
    return (y_prompt, y_sample) + tuple(stack(st_p, i) for i in range(5)) + tuple(stack(st_s, i) for i in range(5))
```

```python
import functools

import numpy as np
import jax
import jax.numpy as jnp
from jax import lax
from jax.experimental import pallas as pl
from jax.experimental.pallas import tpu as pltpu

F32 = jnp.float32
BF16 = jnp.bfloat16

D_MODEL = 2048
HEAD_DIM = 64
N_Q_HEADS = 16
N_KV_HEADS = 4
GROUP = N_Q_HEADS // N_KV_HEADS
ATTN_WIDTH = N_Q_HEADS * HEAD_DIM
KV_WIDTH = N_KV_HEADS * HEAD_DIM
WINDOW = 128
ROPE_THETA = 10000.0
POOL_WIDTH = 512
POOL_WINDOWS = (2, 4, 8, 16)
POOL_GROUP_WIDTH = 128
POOL_PAD = 15
RWKV_HEAD_DIM = 64
RWKV_WIDTH = 512
RWKV_HEADS = 8
DECAY_LORA = 96
ICLR_LORA = 96
GATE_LORA = 256
VRES_LORA = 32
RWKV_PROJ = 3 * RWKV_WIDTH + DECAY_LORA + ICLR_LORA + GATE_LORA
RWKV_PROJ_PAD = 2048
D_FF = 5632
PAST_LEN = 8192
RMS_EPS = 1e-6
LNX_EPS = 64e-5
NEG_INF = -1e30

Q_OFF = 0
K_OFF = ATTN_WIDTH
V_OFF = K_OFF + KV_WIDTH
U_OFF = V_OFF + KV_WIDTH
PR_OFF = U_OFF + POOL_WIDTH
GL_OFF = PR_OFF + RWKV_PROJ_PAD
IN_PAD = GL_OFF + 3 * D_MODEL
ROPE_END = V_OFF

LANES = 128
SUBLANES = 8
VMEM_LIMIT = 56 * 2**20

MATMUL_ROWS = 2048
NORM_ROWS = 1024


def _cparams(*sem):
    return pltpu.CompilerParams(dimension_semantics=sem, vmem_limit_bytes=VMEM_LIMIT)


def _row_tile(t, pref):
    tm = min(t, pref)
    while t % tm:
        tm -= SUBLANES
    return tm


def _sigmoid(x):
    return 1.0 / (1.0 + jnp.exp(-x))


def _mod_norm(x, g, sh, sc):
    ms = jnp.mean(x * x, axis=-1, keepdims=True)
    return (x * lax.rsqrt(ms + RMS_EPS) * g) * (1.0 + sc) + sh


def _bdot(a, b):
    return jnp.dot(a.astype(BF16), b.astype(BF16), preferred_element_type=F32)


def _seg_sum(x, ones_bd):
    hi = x.astype(BF16)
    lo = (x - hi.astype(F32)).astype(BF16)
    return (jnp.dot(hi, ones_bd, preferred_element_type=F32)
            + jnp.dot(lo, ones_bd, preferred_element_type=F32))


def _mod_spec(mod, tm, chunk):
    per_row = mod.shape[1] != 1
    rows = tm if per_row else 1
    return pl.BlockSpec((None, rows, D_MODEL), lambda g, i, j=0: (g, i if per_row else 0, chunk))


def _ada_kernel(c_ref, w_ref, b_ref, o_ref):
    c = c_ref[...]
    o_ref[...] = _bdot(c * _sigmoid(c), w_ref[...]) + b_ref[...]


def _ada(c_all, ada_w, ada_b):
    depth, _, width = ada_w.shape
    rows = c_all.shape[0]
    tn = 1024
    return pl.pallas_call(
        _ada_kernel,
        grid=(depth, width // tn),
        in_specs=[pl.BlockSpec((rows, D_MODEL), lambda l, j: (0, 0)),
                  pl.BlockSpec((None, D_MODEL, tn), lambda l, j: (l, 0, j)),
                  pl.BlockSpec((None, 1, tn), lambda l, j: (l, 0, j))],
        out_specs=pl.BlockSpec((None, rows, tn), lambda l, j: (l, 0, j)),
        out_shape=jax.ShapeDtypeStruct((depth, rows, width), F32),
        compiler_params=_cparams("parallel", "parallel"),
        name="ada",
    )(c_all, ada_w, ada_b.reshape(depth, 1, width))


IN_TN = 512
IN_MAIN_TILES = GL_OFF // IN_TN


def _in_proj_kernel(x_ref, g_ref, sh_ref, sc_ref, cos_ref, sin_ref, w_ref, o_ref, h_ref):
    j = pl.program_id(2)

    @pl.when(j == 0)
    def _():
        h_ref[...] = _mod_norm(x_ref[...], g_ref[...], sh_ref[...], sc_ref[...]).astype(BF16)

    def project():
        return lax.dot_general(h_ref[...], w_ref[0].astype(BF16), (((1,), (1,)), ((), ())),
                               preferred_element_type=F32)

    @pl.when(j * IN_TN < ROPE_END)
    def _():
        acc = project()
        reps = IN_TN // LANES
        cos = jnp.concatenate([cos_ref[...]] * reps, axis=1)
        sin = jnp.concatenate([sin_ref[...]] * reps, axis=1)
        lane = lax.broadcasted_iota(jnp.int32, acc.shape, 1)
        first_half = (lane % HEAD_DIM) < (HEAD_DIM // 2)
        rot = jnp.where(first_half, pltpu.roll(acc, IN_TN - HEAD_DIM // 2, 1),
                        pltpu.roll(acc, HEAD_DIM // 2, 1))
        roped = acc * cos + rot * sin
        o_ref[...] = jnp.where(j * IN_TN + lane < ROPE_END, roped, acc).astype(o_ref.dtype)

    @pl.when(j * IN_TN >= ROPE_END)
    def _():
        o_ref[...] = project().astype(o_ref.dtype)


def _in_proj(x, g1, mod, w_in_t, l, cos, sin, tm):
    groups, t, _ = x.shape
    gate_shift = RWKV_PROJ_PAD - RWKV_PROJ

    def weight_rows(g, i, j):
        start = jnp.where(j < IN_MAIN_TILES, j * IN_TN, j * IN_TN - gate_shift)
        return (l, pl.multiple_of(start, SUBLANES), 0)

    return pl.pallas_call(
        _in_proj_kernel,
        grid=(groups, t // tm, IN_PAD // IN_TN),
        in_specs=[pl.BlockSpec((None, tm, D_MODEL), lambda g, i, j: (g, i, 0), pipeline_mode=pl.Buffered(1)),
                  pl.BlockSpec((1, D_MODEL), lambda g, i, j: (0, 0)),
                  _mod_spec(mod, tm, 0), _mod_spec(mod, tm, 1),
                  pl.BlockSpec((tm, LANES), lambda g, i, j: (i, 0), pipeline_mode=pl.Buffered(1)),
                  pl.BlockSpec((tm, LANES), lambda g, i, j: (i, 0), pipeline_mode=pl.Buffered(1)),
                  pl.BlockSpec((pl.Element(1), pl.Element(IN_TN), pl.Element(D_MODEL)), weight_rows)],
        out_specs=pl.BlockSpec((None, tm, IN_TN), lambda g, i, j: (g, i, j)),
        out_shape=jax.ShapeDtypeStruct((groups, t, IN_PAD), BF16),
        scratch_shapes=[pltpu.VMEM((tm, D_MODEL), BF16)],
        compiler_params=_cparams("parallel", "parallel", "arbitrary"),
        name="in_proj",
    )(x, g1, mod, mod, cos, sin, w_in_t)


def _rope_tables(pos):
    half = HEAD_DIM // 2
    inv = ROPE_THETA ** (-jnp.arange(half, dtype=F32) * 2.0 / HEAD_DIM)
    ang = pos.astype(F32)[:, None] * inv[None, :]
    cos, sin = jnp.cos(ang), jnp.sin(ang)
    reps = LANES // HEAD_DIM
    cos_t = jnp.concatenate([cos, cos] * reps, axis=1)
    sin_t = jnp.concatenate([-sin, sin] * reps, axis=1)
    return cos_t, sin_t


def _attn_prompt_kernel(sink_ref, q_ref, k_ref, v_ref, kp_ref, vp_ref, o_ref):
    n = pl.program_id(1)
    blk = q_ref.shape[0]
    q = q_ref[...]
    k2 = jnp.concatenate([kp_ref[...], k_ref[...]], axis=0).astype(BF16)
    v2 = jnp.concatenate([vp_ref[...], v_ref[...]], axis=0).astype(BF16)
    qi = lax.broadcasted_iota(jnp.int32, (blk, 2 * blk), 0)
    kc = lax.broadcasted_iota(jnp.int32, (blk, 2 * blk), 1)
    rel = qi - kc + blk
    first_key = jnp.where(n > 0, 0, blk)
    mask = (rel >= 0) & (rel <= WINDOW) & (kc >= first_key)
    outs = []
    for hd in range(N_Q_HEADS):
        h = hd // GROUP
        kh = k2[:, h * HEAD_DIM:(h + 1) * HEAD_DIM]
        vh = v2[:, h * HEAD_DIM:(h + 1) * HEAD_DIM]
        qh = q[:, hd * HEAD_DIM:(hd + 1) * HEAD_DIM].astype(BF16)
        s = lax.dot_general(qh, kh, (((1,), (1,)), ((), ())), preferred_element_type=F32) * (HEAD_DIM ** -0.5)
        s = jnp.where(mask, s, NEG_INF)
        sk = sink_ref[hd]
        m = jnp.maximum(jnp.max(s, axis=1, keepdims=True), sk)
        p = jnp.exp(s - m)
        den = jnp.sum(p, axis=1, keepdims=True) + jnp.exp(sk - m)
        outs.append(jnp.dot(p.astype(BF16), vh, preferred_element_type=F32) / den)
    o_ref[...] = jnp.concatenate(outs, axis=1).astype(o_ref.dtype)


def _attn_prompt(z, sink):
    groups, t, _ = z.shape
    blk = WINDOW
    kcol, vcol = K_OFF // KV_WIDTH, V_OFF // KV_WIDTH
    prev = lambda col: (lambda g, n: (g, jnp.maximum(n - 1, 0), col))
    return pl.pallas_call(
        _attn_prompt_kernel,
        grid=(groups, t // blk),
        in_specs=[pl.BlockSpec(memory_space=pltpu.SMEM),
                  pl.BlockSpec((None, blk, ATTN_WIDTH), lambda g, n: (g, n, 0)),
                  pl.BlockSpec((None, blk, KV_WIDTH), lambda g, n: (g, n, kcol)),
                  pl.BlockSpec((None, blk, KV_WIDTH), lambda g, n: (g, n, vcol)),
                  pl.BlockSpec((None, blk, KV_WIDTH), prev(kcol)),
                  pl.BlockSpec((None, blk, KV_WIDTH), prev(vcol))],
        out_specs=pl.BlockSpec((None, blk, ATTN_WIDTH), lambda g, n: (g, n, 0)),
        out_shape=jax.ShapeDtypeStruct((groups, t, ATTN_WIDTH), BF16),
        compiler_params=_cparams("parallel", "parallel"),
        name="attn_prompt",
    )(sink, z, z, z, z, z)


def _attn_sample_kernel(sink_ref, q_ref, kn_ref, vn_ref, knc_ref, vnc_ref, kt_ref, vt_ref, o_ref, kto_ref, vto_ref):
    bb = q_ref.shape[0]
    win = kt_ref.shape[2]
    row = lax.broadcasted_iota(jnp.int32, (N_Q_HEADS, KV_WIDTH), 0)
    lane = lax.broadcasted_iota(jnp.int32, (N_Q_HEADS, KV_WIDTH), 1)
    own = (row // GROUP) == (lane // HEAD_DIM)
    newest = lax.broadcasted_iota(jnp.int32, (KV_WIDTH, win), 1) == win - 1
    sink = sink_ref[...]
    scale = HEAD_DIM ** -0.5

    def body(b, carry):
        q = q_ref[b]
        kt, vt = kt_ref[b], vt_ref[b]
        qbd = jnp.where(own, jnp.concatenate([q] * N_KV_HEADS, axis=1), 0.0)
        s = jnp.dot(qbd.astype(BF16), kt.astype(BF16), preferred_element_type=F32) * scale
        s_new = jnp.sum(qbd * kn_ref[b], axis=1, keepdims=True) * scale
        m = jnp.maximum(jnp.maximum(jnp.max(s, axis=1, keepdims=True), s_new), sink)
        p = jnp.exp(s - m)
        p_new = jnp.exp(s_new - m)
        den = jnp.sum(p, axis=1, keepdims=True) + p_new + jnp.exp(sink - m)
        o = lax.dot_general(p.astype(BF16), vt.astype(BF16), (((1,), (1,)), ((), ())),
                            preferred_element_type=F32) + p_new * vn_ref[b]
        o = jnp.where(own, o, 0.0)
        o64 = o[:, 0:HEAD_DIM]
        for h in range(1, N_KV_HEADS):
            o64 = o64 + o[:, h * HEAD_DIM:(h + 1) * HEAD_DIM]
        o_ref[b] = o64 / den
        kto_ref[b] = jnp.where(newest, knc_ref[b], pltpu.roll(kt, win - 1, 1))
        vto_ref[b] = jnp.where(newest, vnc_ref[b], pltpu.roll(vt, win - 1, 1))
        return carry

    lax.fori_loop(0, bb, body, 0)


def _attn_sample(q3, k_new, v_new, kt_all, vt_all, l, sink):
    _, nb, _, win = kt_all.shape
    assert win == LANES
    bb = _row_tile(nb, 8)
    row = pl.BlockSpec((bb, 1, KV_WIDTH), lambda i: (i, 0, 0))
    col = pl.BlockSpec((bb, KV_WIDTH, 1), lambda i: (i, 0, 0))
    cache_in = pl.BlockSpec((None, bb, KV_WIDTH, win), lambda i: (l, i, 0, 0))
    cache_out = pl.BlockSpec((bb, KV_WIDTH, win), lambda i: (i, 0, 0))
    heads = pl.BlockSpec((bb, N_Q_HEADS, HEAD_DIM), lambda i: (i, 0, 0))
    return pl.pallas_call(
        _attn_sample_kernel,
        grid=(nb // bb,),
        in_specs=[pl.BlockSpec((N_Q_HEADS, 1), lambda i: (0, 0)), heads, row, row, col, col, cache_in, cache_in],
        out_specs=[heads, cache_out, cache_out],
        out_shape=[jax.ShapeDtypeStruct((nb, N_Q_HEADS, HEAD_DIM), F32),
                   jax.ShapeDtypeStruct((nb, KV_WIDTH, win), F32),
                   jax.ShapeDtypeStruct((nb, KV_WIDTH, win), F32)],
        compiler_params=_cparams("parallel"),
        name="attn_sample",
    )(sink.reshape(N_Q_HEADS, 1), q3, k_new.reshape(nb, 1, KV_WIDTH), v_new.reshape(nb, 1, KV_WIDTH),
      k_new.reshape(nb, KV_WIDTH, 1), v_new.reshape(nb, KV_WIDTH, 1), kt_all, vt_all)


POOL_HALO = 16


def _pool_prompt_kernel(u_ref, halo_ref, w_ref, sc_ref, o_ref, ext_ref):
    i = pl.program_id(1)
    tm = u_ref.shape[0]
    ext_ref[0:POOL_HALO, :] = jnp.where(i > 0, halo_ref[...].astype(F32), 0.0)
    ext_ref[POOL_HALO:POOL_HALO + tm, :] = u_ref[...].astype(F32)
    pos = i * tm + lax.broadcasted_iota(jnp.int32, (tm, 1), 0)
    for gi, w in enumerate(POOL_WINDOWS):
        cols = slice(gi * POOL_GROUP_WIDTH, (gi + 1) * POOL_GROUP_WIDTH)
        u = ext_ref[POOL_HALO:POOL_HALO + tm, cols]
        acc = u
        for back in range(1, w):
            acc = acc + ext_ref[POOL_HALO - back:POOL_HALO - back + tm, cols]
        cnt = jnp.minimum(pos + 1, w).astype(F32)
        d = acc / cnt - u
        o_ref[:, cols] = _bdot(d, w_ref[gi]) * sc_ref[:, cols]


def _pool_prompt(z, pool_w, pool_scale, tm):
    groups, t, _ = z.shape
    ucol = U_OFF // POOL_WIDTH
    per = tm // POOL_HALO
    return pl.pallas_call(
        _pool_prompt_kernel,
        grid=(groups, t // tm),
        in_specs=[pl.BlockSpec((None, tm, POOL_WIDTH), lambda g, i: (g, i, ucol)),
                  pl.BlockSpec((None, POOL_HALO, POOL_WIDTH), lambda g, i: (g, jnp.maximum(i * per - 1, 0), ucol)),
                  pl.BlockSpec((len(POOL_WINDOWS), POOL_GROUP_WIDTH, POOL_GROUP_WIDTH), lambda g, i: (0, 0, 0)),
                  pl.BlockSpec((1, POOL_WIDTH), lambda g, i: (0, 0))],
        out_specs=pl.BlockSpec((None, tm, POOL_WIDTH), lambda g, i: (g, i, 0)),
        out_shape=jax.ShapeDtypeStruct((groups, t, POOL_WIDTH), F32),
        scratch_shapes=[pltpu.VMEM((tm + POOL_HALO, POOL_WIDTH), F32)],
        compiler_params=_cparams("parallel", "parallel"),
        name="pool_prompt",
    )(z, z, pool_w, pool_scale)


def _pool_sample_kernel(u_ref, st_ref, w_ref, sc_ref, o_ref):
    for gi, w in enumerate(POOL_WINDOWS):
        cols = slice(gi * POOL_GROUP_WIDTH, (gi + 1) * POOL_GROUP_WIDTH)
        u = u_ref[:, cols].astype(F32)
        acc = u
        for back in range(1, w):
            acc = acc + st_ref[POOL_PAD - back, :, cols]
        cnt = float(min(PAST_LEN + 1, w))
        d = acc / cnt - u
        o_ref[:, cols] = _bdot(d, w_ref[gi]) * sc_ref[:, cols]


def _pool_sample(z, state_t, pool_w, pool_scale):
    rows = z.shape[1]
    ucol = U_OFF // POOL_WIDTH
    return pl.pallas_call(
        _pool_sample_kernel,
        grid=(1,),
        in_specs=[pl.BlockSpec((None, rows, POOL_WIDTH), lambda i: (0, 0, ucol)),
                  pl.BlockSpec((POOL_PAD, rows, POOL_WIDTH), lambda i: (0, 0, 0)),
                  pl.BlockSpec((len(POOL_WINDOWS), POOL_GROUP_WIDTH, POOL_GROUP_WIDTH), lambda i: (0, 0, 0)),
                  pl.BlockSpec((1, POOL_WIDTH), lambda i: (0, 0))],
        out_specs=pl.BlockSpec((None, rows, POOL_WIDTH), lambda i: (0, 0, 0)),
        out_shape=jax.ShapeDtypeStruct((1, rows, POOL_WIDTH), F32),
        compiler_params=_cparams("arbitrary"),
        name="pool_sample",
    )(z, state_t, pool_w, pool_scale)


PREP_HALO = 16


def _rwkv_prep_kernel(*refs, halo, has_vres):
    it = iter(refs)
    pr_ref, prev_ref = next(it), next(it)
    vf_ref = next(it) if has_vres else None
    mu_ref, w0_ref, a0_ref, kk_ref, ka_ref, rk_ref = (next(it) for _ in range(6))
    w2_ref, a2_ref, g2_ref, ones_ref = (next(it) for _ in range(4))
    if has_vres:
        vw1_ref, vw2_ref, vb_ref = next(it), next(it), next(it)
    r_o, w_o, k_o, v_o, a_o, b_o, g_o, bonus_o = (next(it) for _ in range(8))
    ext_ref = next(it) if halo else None

    pr = pr_ref[...].astype(F32)
    tm = pr.shape[0]
    if halo:
        i = pl.program_id(1)
        ext_ref[0:PREP_HALO, :] = jnp.where(i > 0, prev_ref[...].astype(F32), 0.0)
        ext_ref[PREP_HALO:PREP_HALO + tm, :] = pr
        prev = ext_ref[PREP_HALO - 1:PREP_HALO - 1 + tm, :]
    else:
        prev = prev_ref[...]
    xm = pr + (prev - pr) * mu_ref[...]
    wd = RWKV_WIDTH
    r, k, v, lora_in = xm[:, 0:wd], xm[:, wd:2 * wd], xm[:, 2 * wd:3 * wd], xm[:, 3 * wd:4 * wd]
    ones_bd = ones_ref[...]
    lw = _bdot(jnp.tanh(lora_in), w2_ref[...])
    la = _bdot(lora_in, a2_ref[...])
    g = _bdot(_sigmoid(lora_in), g2_ref[...])
    y = -(w0_ref[...] + lw)
    softplus = jnp.maximum(y, 0.0) + jnp.log(1.0 + jnp.exp(-jnp.abs(y)))
    decay = jnp.exp(-jnp.exp(-softplus - 0.5))
    a = _sigmoid(a0_ref[...] + la)
    if has_vres:
        t2 = _bdot(_bdot(v, vw1_ref[...]), vw2_ref[...])
        v = v + (vf_ref[...] - v) * _sigmoid(vb_ref[...] + t2)
    kk = k * kk_ref[...]
    kk = kk / jnp.maximum(jnp.sqrt(_seg_sum(kk * kk, ones_bd)), 1e-12)
    k = k * (1.0 + (a - 1.0) * ka_ref[...])
    r_o[...] = r
    w_o[...] = decay
    k_o[...] = k
    v_o[...] = v
    a_o[...] = -kk
    b_o[...] = kk * a
    g_o[...] = g
    bonus_o[...] = _seg_sum(r * k * rk_ref[...], ones_bd) * v


def _rwkv_prep(z, prev, v_first, rp, tm):
    groups, t, _ = z.shape
    halo = prev is None
    has_vres = v_first is not None
    prcol = PR_OFF // RWKV_PROJ_PAD
    wd = RWKV_WIDTH
    tok = lambda width: pl.BlockSpec((None, tm, width), lambda g, i: (g, i, 0))
    vec = lambda width: pl.BlockSpec((1, width), lambda g, i: (0, 0))
    mat = lambda a, b: pl.BlockSpec((a, b), lambda g, i: (0, 0))
    args = [z]
    in_specs = [pl.BlockSpec((None, tm, RWKV_PROJ_PAD), lambda g, i: (g, i, prcol))]
    if halo:
        per = tm // PREP_HALO
        args.append(z)
        in_specs.append(pl.BlockSpec((None, PREP_HALO, RWKV_PROJ_PAD),
                                     lambda g, i: (g, jnp.maximum(i * per - 1, 0), prcol)))
    else:
        args.append(prev)
        in_specs.append(tok(RWKV_PROJ_PAD))
    if has_vres:
        args.append(v_first)
        in_specs.append(tok(wd))
    args += [rp["mu"], rp["w0"], rp["a0"], rp["k_k"], rp["k_a"], rp["r_k"],
             rp["w2"], rp["a2"], rp["g2"], rp["ones_bd"]]
    in_specs += [vec(RWKV_PROJ_PAD)] + [vec(wd)] * 5 + [mat(wd, wd)] * 4
    if has_vres:
        args += [rp["vw1"], rp["vw2"], rp["vb"]]
        in_specs += [mat(wd, LANES), mat(LANES, wd), vec(wd)]
    return pl.pallas_call(
        functools.partial(_rwkv_prep_kernel, halo=halo, has_vres=has_vres),
        grid=(groups, t // tm),
        in_specs=in_specs,
        out_specs=[tok(wd)] * 8,
        out_shape=[jax.ShapeDtypeStruct((groups, t, wd), F32)] * 8,
        scratch_shapes=[pltpu.VMEM((tm + PREP_HALO, RWKV_PROJ_PAD), F32)] if halo else [],
        compiler_params=_cparams("parallel", "parallel"),
        name="rwkv_prep",
    )(*args)


def _wkv_scan_kernel(r_ref, w_ref, k_ref, v_ref, a_ref, b_ref, y_ref, s_ref):
    nb, tc, _ = r_ref.shape
    pairs = RWKV_HEADS // 2
    n = RWKV_HEAD_DIM

    @pl.when(pl.program_id(0) == 0)
    def _():
        s_ref[...] = jnp.zeros(s_ref.shape, F32)

    row = lax.broadcasted_iota(jnp.int32, (n, 2 * n), 0)
    lane = lax.broadcasted_iota(jnp.int32, (n, 2 * n), 1)
    diag = (lane % n) == row
    chains = [(bi, slice(p * 2 * n, (p + 1) * 2 * n)) for bi in range(nb) for p in range(pairs)]
    nc = len(chains)
    seg_r = (lax.broadcasted_iota(jnp.int32, (4 * n, 2 * n), 0) % (2 * n)) // n
    seg_c = lax.broadcasted_iota(jnp.int32, (4 * n, 2 * n), 1) // n
    ones_bd2 = (seg_r == seg_c).astype(BF16)
    ones_bd = ones_bd2[:2 * n]
    stack = lambda xs: jnp.concatenate(xs, axis=0)
    unstack = lambda x: [x[c * n:(c + 1) * n] for c in range(nc)]

    def seg_bcast(x, exact):
        hi = x.astype(BF16)
        if not exact:
            return jnp.dot(hi, ones_bd, preferred_element_type=F32)
        lo = (x - hi.astype(F32)).astype(BF16)
        return jnp.dot(jnp.concatenate([hi, lo], axis=1), ones_bd2, preferred_element_type=F32)

    def steps(t8, carry):
        base = pl.multiple_of(t8 * SUBLANES, SUBLANES)
        rows = pl.ds(base, SUBLANES)
        tiles = [[ref[bi, rows, sl] for ref in (r_ref, w_ref, k_ref, v_ref, a_ref, b_ref)] for bi, sl in chains]
        s = [s_ref[c] for c in range(nc)]
        ys = [[] for _ in chains]
        for u in range(SUBLANES):
            one = slice(u, u + 1)
            vcol = unstack(seg_bcast(stack([jnp.where(diag, t[3][one], 0.0) for t in tiles]), False))
            sa = unstack(seg_bcast(stack([s[c] * tiles[c][4][one] for c in range(nc)]), True))
            for c, (r8, w8, k8, v8, a8, b8) in enumerate(tiles):
                s[c] = s[c] * w8[one] + sa[c] * b8[one] + vcol[c] * k8[one]
            ycol = unstack(seg_bcast(stack([s[c] * tiles[c][0][one] for c in range(nc)]), False))
            for c in range(nc):
                ys[c].append(jnp.sum(jnp.where(diag, ycol[c], 0.0), axis=0, keepdims=True))
        for c, (bi, sl) in enumerate(chains):
            s_ref[c] = s[c]
            y_ref[bi, rows, sl] = jnp.concatenate(ys[c], axis=0)
        return carry

    lax.fori_loop(0, tc // SUBLANES, steps, 0)


def _wkv_scan(r, w, k, v, a, b, tc):
    nb, t, wd = r.shape
    pairs = RWKV_HEADS // 2
    tok = pl.BlockSpec((nb, tc, wd), lambda i: (0, i, 0))
    return pl.pallas_call(
        _wkv_scan_kernel,
        grid=(t // tc,),
        in_specs=[tok] * 6,
        out_specs=[tok, pl.BlockSpec((nb * pairs, RWKV_HEAD_DIM, 2 * RWKV_HEAD_DIM), lambda i: (0, 0, 0))],
        out_shape=[jax.ShapeDtypeStruct((nb, t, wd), F32),
                   jax.ShapeDtypeStruct((nb * pairs, RWKV_HEAD_DIM, 2 * RWKV_HEAD_DIM), F32)],
        compiler_params=_cparams("arbitrary"),
        name="wkv_scan",
    )(r, w, k, v, a, b)


def _wkv_step_kernel(r_ref, w_ref, k_ref, v_ref, a_ref, b_ref, s_ref, y_ref, so_ref):
    heads = s_ref.shape[0]
    n = RWKV_HEAD_DIM
    diag = lax.broadcasted_iota(jnp.int32, (n, n), 0) == lax.broadcasted_iota(jnp.int32, (n, n), 1)

    def body(h8, carry):
        base = pl.multiple_of(h8 * SUBLANES, SUBLANES)
        rows = pl.ds(base, SUBLANES)
        r8, w8, k8, v8, a8, b8 = (ref[rows, :] for ref in (r_ref, w_ref, k_ref, v_ref, a_ref, b_ref))
        y8 = []
        for u in range(SUBLANES):
            one = slice(u, u + 1)
            s = s_ref[base + u]
            sa = jnp.sum(s * a8[one], axis=1, keepdims=True)
            vcol = jnp.sum(jnp.where(diag, v8[one], 0.0), axis=1, keepdims=True)
            s = s * w8[one] + sa * b8[one] + vcol * k8[one]
            so_ref[base + u] = s
            ycol = jnp.sum(s * r8[one], axis=1, keepdims=True)
            y8.append(jnp.sum(jnp.where(diag, ycol, 0.0), axis=0, keepdims=True))
        y_ref[rows, :] = jnp.concatenate(y8, axis=0)
        return carry

    lax.fori_loop(0, heads // SUBLANES, body, 0)


def _wkv_step(r, w, k, v, a, b, state, l):
    heads = state.shape[1]
    hb = _row_tile(heads, 128)
    n = RWKV_HEAD_DIM
    vec = pl.BlockSpec((hb, n), lambda i: (i, 0))
    st = pl.BlockSpec((hb, n, n), lambda i: (i, 0, 0))
    return pl.pallas_call(
        _wkv_step_kernel,
        grid=(heads // hb,),
        in_specs=[vec] * 6 + [pl.BlockSpec((None, hb, n, n), lambda i: (l, i, 0, 0))],
        out_specs=[vec, st],
        out_shape=[jax.ShapeDtypeStruct((heads, n), F32), jax.ShapeDtypeStruct((heads, n, n), F32)],
        compiler_params=_cparams("parallel"),
        name="wkv_step",
    )(r, w, k, v, a, b, state)


def _rwkv_post_kernel(y_ref, bonus_ref, g_ref, lg_ref, lb_ref, ones_ref, o_ref):
    y = y_ref[...]
    ones_bd = ones_ref[...]
    inv = 1.0 / RWKV_HEAD_DIM
    d = y - _seg_sum(y, ones_bd) * inv
    var = _seg_sum(d * d, ones_bd) * inv
    yn = d * lax.rsqrt(var + LNX_EPS) * lg_ref[...] + lb_ref[...]
    o_ref[...] = ((yn + bonus_ref[...]) * g_ref[...]).astype(o_ref.dtype)


def _rwkv_post(y, bonus, g, rp, tm):
    groups, t, wd = y.shape
    tok = pl.BlockSpec((None, tm, wd), lambda gi, i: (gi, i, 0))
    vec = pl.BlockSpec((1, wd), lambda gi, i: (0, 0))
    return pl.pallas_call(
        _rwkv_post_kernel,
        grid=(groups, t // tm),
        in_specs=[tok, tok, tok, vec, vec, pl.BlockSpec((wd, wd), lambda gi, i: (0, 0))],
        out_specs=tok,
        out_shape=jax.ShapeDtypeStruct((groups, t, wd), BF16),
        compiler_params=_cparams("parallel", "parallel"),
        name="rwkv_post",
    )(y, bonus, g, rp["lnx_g"], rp["lnx_b"], rp["ones_bd"])


MERGE_TN = 512


def _merge_kernel(oa_ref, zp_ref, yr_ref, ga_ref, gp_ref, gr_ref, wa_ref, wp_ref, wr_ref, o_ref):
    gate = lambda ref: _sigmoid(ref[...].astype(F32))
    merged = (gate(ga_ref) * _bdot(oa_ref[...], wa_ref[...])
              + gate(gp_ref) * _bdot(zp_ref[...], wp_ref[...])
              + gate(gr_ref) * _bdot(yr_ref[...], wr_ref[...]))
    o_ref[...] = merged.astype(o_ref.dtype)


def _merge(o_attn, z_pool, y_rwkv, z, w_attn_out, w_pool_out, w_rwkv_out, l, tm):
    groups, t, _ = z.shape
    tn = MERGE_TN
    gate = lambda br: pl.BlockSpec((None, tm, tn), lambda g, i, j: (g, i, (GL_OFF + br * D_MODEL) // tn + j))
    tok = lambda width: pl.BlockSpec((None, tm, width), lambda g, i, j: (g, i, 0))
    wt = lambda rows: pl.BlockSpec((None, rows, tn), lambda g, i, j: (l, 0, j))
    return pl.pallas_call(
        _merge_kernel,
        grid=(groups, t // tm, D_MODEL // tn),
        in_specs=[tok(ATTN_WIDTH), tok(POOL_WIDTH), tok(RWKV_WIDTH), gate(0), gate(1), gate(2),
                  wt(ATTN_WIDTH), wt(POOL_WIDTH), wt(RWKV_WIDTH)],
        out_specs=pl.BlockSpec((None, tm, tn), lambda g, i, j: (g, i, j)),
        out_shape=jax.ShapeDtypeStruct((groups, t, D_MODEL), BF16),
        compiler_params=_cparams("parallel", "parallel", "arbitrary"),
        name="merge",
    )(o_attn, z_pool, y_rwkv, z, z, z, w_attn_out, w_pool_out, w_rwkv_out)


def _proj_residual_kernel(a_ref, w_ref, x_ref, gt_ref, o_ref):
    o_ref[...] = x_ref[...] + (1.0 + gt_ref[...]) * _bdot(a_ref[...], w_ref[...])


def _proj_residual(a, w, l, x, mod, chunk, tm, tn):
    groups, t, kdim = a.shape
    per_row = mod.shape[1] != 1
    gate = pl.BlockSpec((None, tm if per_row else 1, tn),
                        lambda g, i, j: (g, i if per_row else 0, chunk * (D_MODEL // tn) + j))
    return pl.pallas_call(
        _proj_residual_kernel,
        grid=(groups, t // tm, D_MODEL // tn),
        in_specs=[pl.BlockSpec((None, tm, kdim), lambda g, i, j: (g, i, 0), pipeline_mode=pl.Buffered(1)),
                  pl.BlockSpec((None, kdim, tn), lambda g, i, j: (l, 0, j)),
                  pl.BlockSpec((None, tm, tn), lambda g, i, j: (g, i, j)),
                  gate],
        out_specs=pl.BlockSpec((None, tm, tn), lambda g, i, j: (g, i, j)),
        out_shape=jax.ShapeDtypeStruct((groups, t, D_MODEL), F32),
        compiler_params=_cparams("parallel", "parallel", "arbitrary"),
        name="proj_residual",
    )(a, w, x, mod)


FFN_TN = 256


def _ffn_up_kernel(x_ref, g_ref, sh_ref, sc_ref, wg_ref, wu_ref, o_ref, h_ref):
    @pl.when(pl.program_id(2) == 0)
    def _():
        h_ref[...] = _mod_norm(x_ref[...], g_ref[...], sh_ref[...], sc_ref[...]).astype(BF16)

    h = h_ref[...]
    gate = jnp.dot(h, wg_ref[...].astype(BF16), preferred_element_type=F32)
    up = jnp.dot(h, wu_ref[...].astype(BF16), preferred_element_type=F32)
    o_ref[...] = (gate * _sigmoid(gate) * up).astype(o_ref.dtype)


def _ffn_up(x, g2, mod, w_gate, w_up, l, tm):
    groups, t, _ = x.shape
    tn = FFN_TN
    wt = pl.BlockSpec((None, D_MODEL, tn), lambda g, i, j: (l, 0, j))
    return pl.pallas_call(
        _ffn_up_kernel,
        grid=(groups, t // tm, D_FF // tn),
        in_specs=[pl.BlockSpec((None, tm, D_MODEL), lambda g, i, j: (g, i, 0), pipeline_mode=pl.Buffered(1)),
                  pl.BlockSpec((1, D_MODEL), lambda g, i, j: (0, 0)),
                  _mod_spec(mod, tm, 3), _mod_spec(mod, tm, 4), wt, wt],
        out_specs=pl.BlockSpec((None, tm, tn), lambda g, i, j: (g, i, j)),
        out_shape=jax.ShapeDtypeStruct((groups, t, D_FF), BF16),
        scratch_shapes=[pltpu.VMEM((tm, D_MODEL), BF16)],
        compiler_params=_cparams("parallel", "parallel", "arbitrary"),
        name="ffn_up",
    )(x, g2, mod, mod, w_gate, w_up)


def _final_norm_kernel(x_ref, g_ref, o_ref):
    x = x_ref[...]
    ms = jnp.mean(x * x, axis=-1, keepdims=True)
    o_ref[...] = x * lax.rsqrt(ms + RMS_EPS) * g_ref[...]


def _final_norm(x, g, tm):
    groups, t, _ = x.shape
    tok = pl.BlockSpec((None, tm, D_MODEL), lambda gi, i: (gi, i, 0))
    return pl.pallas_call(
        _final_norm_kernel,
        grid=(groups, t // tm),
        in_specs=[tok, pl.BlockSpec((1, D_MODEL), lambda gi, i: (0, 0))],
        out_specs=tok,
        out_shape=jax.ShapeDtypeStruct(x.shape, F32),
        compiler_params=_cparams("parallel", "parallel"),
        name="final_norm",
    )(x, g)


def _pad_cols(a, width):
    return jnp.pad(a, [(0, 0)] * (a.ndim - 1) + [(0, width - a.shape[-1])])


def _rwkv_params(l, rwkv_mu, rwkv_w0, rwkv_w2, rwkv_a0, rwkv_a2, rwkv_g2, rwkv_k_k, rwkv_k_a, rwkv_r_k,
                 rwkv_lnx_g, rwkv_lnx_b, vres_w1, vres_w2, vres_b):
    wd = RWKV_WIDTH
    row = lambda a: a.reshape(1, -1)
    lora_rows = lambda w, off: jnp.zeros((wd, wd), F32).at[off:off + w.shape[0]].set(w).astype(BF16)
    seg = np.arange(wd) // RWKV_HEAD_DIM
    rp = dict(
        mu=_pad_cols(row(rwkv_mu[l]), RWKV_PROJ_PAD), w0=row(rwkv_w0[l]), a0=row(rwkv_a0[l]),
        k_k=row(rwkv_k_k[l]), k_a=row(rwkv_k_a[l]), r_k=row(rwkv_r_k[l]),
        w2=lora_rows(rwkv_w2[l], 0), a2=lora_rows(rwkv_a2[l], DECAY_LORA),
        g2=lora_rows(rwkv_g2[l], DECAY_LORA + ICLR_LORA),
        ones_bd=jnp.asarray(seg[:, None] == seg[None, :], BF16),
        lnx_g=row(rwkv_lnx_g[l]), lnx_b=row(rwkv_lnx_b[l]))
    if l > 0:
        rp.update(vw1=_pad_cols(vres_w1[l - 1], LANES).astype(BF16),
                  vw2=jnp.pad(vres_w2[l - 1], ((0, LANES - VRES_LORA), (0, 0))).astype(BF16),
                  vb=row(vres_b[l - 1]))
    return rp


def _layer_common(x, mod, z, o_attn, z_pool, y, bonus, g, lw, l, rp, tm):
    y_rwkv = _rwkv_post(y, bonus, g, rp, _row_tile(x.shape[1], 512))
    merged = _merge(o_attn, z_pool, y_rwkv, z, lw["w_attn_out"], lw["w_pool_out"], lw["w_rwkv_out"], l, tm)
    x = _proj_residual(merged, lw["w_o"], l, x, mod, 2, tm, 512)
    hidden = _ffn_up(x, lw["norm2_g"][l].reshape(1, -1), mod, lw["w_ffn_gate"], lw["w_ffn_up"], l, tm)
    return _proj_residual(hidden, lw["w_ffn_down"], l, x, mod, 5, tm, 256)


def kernel(x_prompt, x_sample, c_prompt, c_sample, cache_k_win, cache_v_win, state_pool, state_shift, state_wkv, ada_w, ada_b, norm1_g, norm2_g, w_in, attn_sink, w_attn_out, pool_w, pool_scale, w_pool_out, rwkv_mu, rwkv_w0, rwkv_w2, rwkv_a0, rwkv_a2, rwkv_g2, rwkv_k_k, rwkv_k_a, rwkv_r_k, rwkv_lnx_g, rwkv_lnx_b, w_rwkv_out, vres_w1, vres_w2, vres_b, w_o, w_ffn_gate, w_ffn_up, w_ffn_down, final_norm_g):
    depth = ada_w.shape[0]
    nb, t, _ = x_prompt.shape
    ns = x_sample.shape[0]
    assert x_sample.shape[1] == 1 and t % WINDOW == 0
    win = cache_k_win.shape[2]
    assert win <= WINDOW
    heads, n = RWKV_HEADS, RWKV_HEAD_DIM

    pad_rows = -nb % SUBLANES
    c_all = jnp.concatenate([c_prompt, jnp.zeros((pad_rows, D_MODEL), F32), c_sample], axis=0)
    mod_all = _ada(c_all, ada_w, ada_b)

    cos_p, sin_p = _rope_tables(jnp.arange(t))
    cos_s, sin_s = _rope_tables(jnp.full((ns,), PAST_LEN))

    xp = x_prompt
    xs = x_sample.reshape(1, ns, D_MODEL)
    tm_p = _row_tile(t, MATMUL_ROWS)
    tm_s = _row_tile(ns, MATMUL_ROWS)
    vf_p = vf_s = None
    st_p, st_s = [], []
    lw = dict(norm2_g=norm2_g, w_attn_out=w_attn_out, w_pool_out=w_pool_out, w_rwkv_out=w_rwkv_out, w_o=w_o,
              w_ffn_gate=w_ffn_gate, w_ffn_up=w_ffn_up, w_ffn_down=w_ffn_down)
    w_in_t = w_in.transpose(0, 2, 1)
    to_feature_major = lambda c: c.transpose(0, 1, 3, 4, 2).reshape(depth, ns, KV_WIDTH, win)
    from_feature_major = lambda c: c.reshape(ns, N_KV_HEADS, HEAD_DIM, win).transpose(0, 3, 1, 2)
    cache_kt, cache_vt = to_feature_major(cache_k_win), to_feature_major(cache_v_win)
    state_heads = state_wkv.reshape(depth, ns * heads, n, n)
    for l in range(depth):
        rp = _rwkv_params(l, rwkv_mu, rwkv_w0, rwkv_w2, rwkv_a0, rwkv_a2, rwkv_g2, rwkv_k_k, rwkv_k_a, rwkv_r_k,
                          rwkv_lnx_g, rwkv_lnx_b, vres_w1, vres_w2, vres_b)
        g1 = norm1_g[l].reshape(1, -1)
        mod_p = mod_all[l, :nb].reshape(nb, 1, -1)
        mod_s = mod_all[l, nb + pad_rows:].reshape(1, ns, -1)

        z = _in_proj(xp, g1, mod_p, w_in_t, l, cos_p, sin_p, tm_p)
        o_attn = _attn_prompt(z, attn_sink[l])
        z_pool = _pool_prompt(z, pool_w[l], pool_scale[l].reshape(1, -1), _row_tile(t, 512))
        r, w, k, v, a, b, g, bonus = _rwkv_prep(z, None, vf_p, rp, _row_tile(t, 256))
        if l == 0:
            vf_p = v
        y, s_pair = _wkv_scan(r, w, k, v, a, b, _row_tile(t, 256))
        wkv_new = s_pair.reshape(nb, heads // 2, n, 2, n).transpose(0, 1, 3, 2, 4).reshape(nb, heads, n, n)
        tail = z[:, t - WINDOW:, :GL_OFF].astype(F32)
        st_p.append((tail[:, :, K_OFF:V_OFF].reshape(nb, WINDOW, N_KV_HEADS, HEAD_DIM),
                     tail[:, :, V_OFF:U_OFF].reshape(nb, WINDOW, N_KV_HEADS, HEAD_DIM),
                     tail[:, WINDOW - POOL_PAD:, U_OFF:PR_OFF],
                     tail[:, WINDOW - 1, PR_OFF:PR_OFF + RWKV_PROJ],
                     wkv_new))
        xp = _layer_common(xp, mod_p, z, o_attn, z_pool, y, bonus, g, lw, l, rp, tm_p)

        z = _in_proj(xs, g1, mod_s, w_in_t, l, cos_s, sin_s, tm_s)
        zs = z[0, :, :GL_OFF].astype(F32)
        k_new, v_new = zs[:, K_OFF:V_OFF], zs[:, V_OFF:U_OFF]
        o3, kt_new, vt_new = _attn_sample(zs[:, :ATTN_WIDTH].reshape(ns, N_Q_HEADS, HEAD_DIM), k_new, v_new,
                                          cache_kt, cache_vt, l, attn_sink[l])
        o_attn = o3.reshape(1, ns, ATTN_WIDTH).astype(BF16)
        z_pool = _pool_sample(z, state_pool[l].transpose(1, 0, 2), pool_w[l], pool_scale[l].reshape(1, -1))
        prev = _pad_cols(state_shift[l], RWKV_PROJ_PAD).reshape(1, ns, RWKV_PROJ_PAD)
        r, w, k, v, a, b, g, bonus = _rwkv_prep(z, prev, vf_s, rp, tm_s)
        if l == 0:
            vf_s = v
        per_head = lambda arr: arr.reshape(ns * heads, n)
        y, wkv_new = _wkv_step(*(per_head(arr) for arr in (r, w, k, v, a, b)), state_heads, l)
        u_new = zs[:, None, U_OFF:PR_OFF]
        st_s.append((from_feature_major(kt_new), from_feature_major(vt_new),
                     jnp.concatenate([state_pool[l][:, 1:], u_new], axis=1),
                     zs[:, PR_OFF:PR_OFF + RWKV_PROJ],
                     wkv_new.reshape(ns, heads, n, n)))
        xs = _layer_common(xs, mod_s, z, o_attn, z_pool, y.reshape(1, ns, RWKV_WIDTH), bonus, g, lw, l, rp, tm_s)

    g_fin = final_norm_g.reshape(1, -1)
    y_prompt = _final_norm(xp, g_fin, _row_tile(t, NORM_ROWS))
    y_sample = _final_norm(xs, g_fin, _row_tile(ns, NORM_ROWS)).reshape(ns, 1, D_MODEL)
    stack = lambda states, i: jnp.stack([s[i] for s in states])
    return (y_prompt, y_sample) + tuple(stack(st_p, i) for i in range(5)) + tuple(stack(st_s, i) for i in range(5))
```

```python
import functools

import numpy as np
import jax
import jax.numpy as jnp
from jax import lax
from jax.experimental import pallas as pl
from jax.experimental.pallas import tpu as pltpu

F32 = jnp.float32
BF16 = jnp.bfloat16

D_MODEL = 2048
HEAD_DIM = 64
N_Q_HEADS = 16
N_KV_HEADS = 4
GROUP = N_Q_HEADS // N_KV_HEADS
ATTN_WIDTH = N_Q_HEADS * HEAD_DIM
KV_WIDTH = N_KV_HEADS * HEAD_DIM
WINDOW = 128
ROPE_THETA = 10000.0
POOL_WIDTH = 512
POOL_WINDOWS = (2, 4, 8, 16)
POOL_GROUP_WIDTH = 128
POOL_PAD = 15
RWKV_HEAD_DIM = 64
RWKV_WIDTH = 512
RWKV_HEADS = 8
DECAY_LORA = 96
ICLR_LORA = 96
GATE_LORA = 256
VRES_LORA = 32
RWKV_PROJ = 3 * RWKV_WIDTH + DECAY_LORA + ICLR_LORA + GATE_LORA
RWKV_PROJ_PAD = 2048
D_FF = 5632
PAST_LEN = 8192
RMS_EPS = 1e-6
LNX_EPS = 64e-5
NEG_INF = -1e30

Q_OFF = 0
K_OFF = ATTN_WIDTH
V_OFF = K_OFF + KV_WIDTH
U_OFF = V_OFF + KV_WIDTH
PR_OFF = U_OFF + POOL_WIDTH
GL_OFF = PR_OFF + RWKV_PROJ_PAD
IN_PAD = GL_OFF + 3 * D_MODEL
ROPE_END = V_OFF

LANES = 128
SUBLANES = 8
VMEM_LIMIT = 56 * 2**20

MATMUL_ROWS = 2048
NORM_ROWS = 1024


def _cparams(*sem):
    return pltpu.CompilerParams(dimension_semantics=sem, vmem_limit_bytes=VMEM_LIMIT)


def _row_tile(t, pref):
    tm = min(t, pref)
    while t % tm:
        tm -= SUBLANES
    return tm


def _sigmoid(x):
    return 1.0 / (1.0 + jnp.exp(-x))


def _mod_norm(x, g, sh, sc):
    ms = jnp.mean(x * x, axis=-1, keepdims=True)
    return (x * lax.rsqrt(ms + RMS_EPS) * g) * (1.0 + sc) + sh


def _bdot(a, b):
    return jnp.dot(a.astype(BF16), b.astype(BF16), preferred_element_type=F32)


def _seg_sum(x, ones_bd):
    hi = x.astype(BF16)
    lo = (x - hi.astype(F32)).astype(BF16)
    return (jnp.dot(hi, ones_bd, preferred_element_type=F32)
            + jnp.dot(lo, ones_bd, preferred_element_type=F32))


def _mod_spec(mod, tm, chunk):
    per_row = mod.shape[1] != 1
    rows = tm if per_row else 1
    return pl.BlockSpec((None, rows, D_MODEL), lambda g, i, j=0: (g, i if per_row else 0, chunk))


def _ada_kernel(c_ref, w_ref, b_ref, o_ref):
    c = c_ref[...]
    o_ref[...] = _bdot(c * _sigmoid(c), w_ref[...]) + b_ref[...]


def _ada(c_all, ada_w, ada_b):
    depth, _, width = ada_w.shape
    rows = c_all.shape[0]
    tn = 1024
    return pl.pallas_call(
        _ada_kernel,
        grid=(depth, width // tn),
        in_specs=[pl.BlockSpec((rows, D_MODEL), lambda l, j: (0, 0)),
                  pl.BlockSpec((None, D_MODEL, tn), lambda l, j: (l, 0, j)),
                  pl.BlockSpec((None, 1, tn), lambda l, j: (l, 0, j))],
        out_specs=pl.BlockSpec((None, rows, tn), lambda l, j: (l, 0, j)),
        out_shape=jax.ShapeDtypeStruct((depth, rows, width), F32),
        compiler_params=_cparams("parallel", "parallel"),
        name="ada",
    )(c_all, ada_w, ada_b.reshape(depth, 1, width))


IN_TN = 512
IN_MAIN_TILES = GL_OFF // IN_TN


def _in_proj_kernel(x_ref, g_ref, sh_ref, sc_ref, cos_ref, sin_ref, w_ref, o_ref, h_ref):
    j = pl.program_id(2)

    @pl.when(j == 0)
    def _():
        h_ref[...] = _mod_norm(x_ref[...], g_ref[...], sh_ref[...], sc_ref[...]).astype(BF16)

    def project():
        return lax.dot_general(h_ref[...], w_ref[0].astype(BF16), (((1,), (1,)), ((), ())),
                               preferred_element_type=F32)

    @pl.when(j * IN_TN < ROPE_END)
    def _():
        acc = project()
        reps = IN_TN // LANES
        cos = jnp.concatenate([cos_ref[...]] * reps, axis=1)
        sin = jnp.concatenate([sin_ref[...]] * reps, axis=1)
        lane = lax.broadcasted_iota(jnp.int32, acc.shape, 1)
        first_half = (lane % HEAD_DIM) < (HEAD_DIM // 2)
        rot = jnp.where(first_half, pltpu.roll(acc, IN_TN - HEAD_DIM // 2, 1),
                        pltpu.roll(acc, HEAD_DIM // 2, 1))
        roped = acc * cos + rot * sin
        o_ref[...] = jnp.where(j * IN_TN + lane < ROPE_END, roped, acc).astype(o_ref.dtype)

    @pl.when(j * IN_TN >= ROPE_END)
    def _():
        o_ref[...] = project().astype(o_ref.dtype)


def _in_proj(x, g1, mod, w_in_t, l, cos, sin, tm):
    groups, t, _ = x.shape
    gate_shift = RWKV_PROJ_PAD - RWKV_PROJ

    def weight_rows(g, i, j):
        start = jnp.where(j < IN_MAIN_TILES, j * IN_TN, j * IN_TN - gate_shift)
        return (l, pl.multiple_of(start, SUBLANES), 0)

    return pl.pallas_call(
        _in_proj_kernel,
        grid=(groups, t // tm, IN_PAD // IN_TN),
        in_specs=[pl.BlockSpec((None, tm, D_MODEL), lambda g, i, j: (g, i, 0), pipeline_mode=pl.Buffered(1)),
                  pl.BlockSpec((1, D_MODEL), lambda g, i, j: (0, 0)),
                  _mod_spec(mod, tm, 0), _mod_spec(mod, tm, 1),
                  pl.BlockSpec((tm, LANES), lambda g, i, j: (i, 0), pipeline_mode=pl.Buffered(1)),
                  pl.BlockSpec((tm, LANES), lambda g, i, j: (i, 0), pipeline_mode=pl.Buffered(1)),
                  pl.BlockSpec((pl.Element(1), pl.Element(IN_TN), pl.Element(D_MODEL)), weight_rows)],
        out_specs=pl.BlockSpec((None, tm, IN_TN), lambda g, i, j: (g, i, j)),
        out_shape=jax.ShapeDtypeStruct((groups, t, IN_PAD), BF16),
        scratch_shapes=[pltpu.VMEM((tm, D_MODEL), BF16)],
        compiler_params=_cparams("parallel", "parallel", "arbitrary"),
        name="in_proj",
    )(x, g1, mod, mod, cos, sin, w_in_t)


def _rope_tables(pos):
    half = HEAD_DIM // 2
    inv = ROPE_THETA ** (-jnp.arange(half, dtype=F32) * 2.0 / HEAD_DIM)
    ang = pos.astype(F32)[:, None] * inv[None, :]
    cos, sin = jnp.cos(ang), jnp.sin(ang)
    reps = LANES // HEAD_DIM
    cos_t = jnp.concatenate([cos, cos] * reps, axis=1)
    sin_t = jnp.concatenate([-sin, sin] * reps, axis=1)
    return cos_t, sin_t


def _attn_prompt_kernel(sink_ref, q_ref, k_ref, v_ref, kp_ref, vp_ref, o_ref):
    n = pl.program_id(1)
    blk = q_ref.shape[0]
    q = q_ref[...]
    k2 = jnp.concatenate([kp_ref[...], k_ref[...]], axis=0).astype(BF16)
    v2 = jnp.concatenate([vp_ref[...], v_ref[...]], axis=0).astype(BF16)
    qi = lax.broadcasted_iota(jnp.int32, (blk, 2 * blk), 0)
    kc = lax.broadcasted_iota(jnp.int32, (blk, 2 * blk), 1)
    rel = qi - kc + blk
    first_key = jnp.where(n > 0, 0, blk)
    mask = (rel >= 0) & (rel <= WINDOW) & (kc >= first_key)
    outs = []
    for hd in range(N_Q_HEADS):
        h = hd // GROUP
        kh = k2[:, h * HEAD_DIM:(h + 1) * HEAD_DIM]
        vh = v2[:, h * HEAD_DIM:(h + 1) * HEAD_DIM]
        qh = q[:, hd * HEAD_DIM:(hd + 1) * HEAD_DIM].astype(BF16)
        s = lax.dot_general(qh, kh, (((1,), (1,)), ((), ())), preferred_element_type=F32) * (HEAD_DIM ** -0.5)
        s = jnp.where(mask, s, NEG_INF)
        sk = sink_ref[hd]
        m = jnp.maximum(jnp.max(s, axis=1, keepdims=True), sk)
        p = jnp.exp(s - m)
        den = jnp.sum(p, axis=1, keepdims=True) + jnp.exp(sk - m)
        outs.append(jnp.dot(p.astype(BF16), vh, preferred_element_type=F32) / den)
    o_ref[...] = jnp.concatenate(outs, axis=1).astype(o_ref.dtype)


def _attn_prompt(z, sink):
    groups, t, _ = z.shape
    blk = WINDOW
    kcol, vcol = K_OFF // KV_WIDTH, V_OFF // KV_WIDTH
    prev = lambda col: (lambda g, n: (g, jnp.maximum(n - 1, 0), col))
    return pl.pallas_call(
        _attn_prompt_kernel,
        grid=(groups, t // blk),
        in_specs=[pl.BlockSpec(memory_space=pltpu.SMEM),
                  pl.BlockSpec((None, blk, ATTN_WIDTH), lambda g, n: (g, n, 0)),
                  pl.BlockSpec((None, blk, KV_WIDTH), lambda g, n: (g, n, kcol)),
                  pl.BlockSpec((None, blk, KV_WIDTH), lambda g, n: (g, n, vcol)),
                  pl.BlockSpec((None, blk, KV_WIDTH), prev(kcol)),
                  pl.BlockSpec((None, blk, KV_WIDTH), prev(vcol))],
        out_specs=pl.BlockSpec((None, blk, ATTN_WIDTH), lambda g, n: (g, n, 0)),
        out_shape=jax.ShapeDtypeStruct((groups, t, ATTN_WIDTH), BF16),
        compiler_params=_cparams("parallel", "parallel"),
        name="attn_prompt",
    )(sink, z, z, z, z, z)


def _attn_sample_kernel(sink_ref, q_ref, kn_ref, vn_ref, knc_ref, vnc_ref, kt_ref, vt_ref, o_ref, kto_ref, vto_ref):
    bb = q_ref.shape[0]
    win = kt_ref.shape[2]
    row = lax.broadcasted_iota(jnp.int32, (N_Q_HEADS, KV_WIDTH), 0)
    lane = lax.broadcasted_iota(jnp.int32, (N_Q_HEADS, KV_WIDTH), 1)
    own = (row // GROUP) == (lane // HEAD_DIM)
    newest = lax.broadcasted_iota(jnp.int32, (KV_WIDTH, win), 1) == win - 1
    sink = sink_ref[...]
    scale = HEAD_DIM ** -0.5

    rows = range(bb)
    qbd = [jnp.where(own, jnp.concatenate([q_ref[b]] * N_KV_HEADS, axis=1), 0.0) for b in rows]
    s = [jnp.dot(qbd[b].astype(BF16), kt_ref[b].astype(BF16), preferred_element_type=F32) * scale for b in rows]
    s_new = [jnp.sum(qbd[b] * kn_ref[b], axis=1, keepdims=True) * scale for b in rows]
    m = [jnp.maximum(jnp.maximum(jnp.max(s[b], axis=1, keepdims=True), s_new[b]), sink) for b in rows]
    p = [jnp.exp(s[b] - m[b]) for b in rows]
    p_new = [jnp.exp(s_new[b] - m[b]) for b in rows]
    den = [jnp.sum(p[b], axis=1, keepdims=True) + p_new[b] + jnp.exp(sink - m[b]) for b in rows]
    o = [lax.dot_general(p[b].astype(BF16), vt_ref[b].astype(BF16), (((1,), (1,)), ((), ())),
                         preferred_element_type=F32) + p_new[b] * vn_ref[b] for b in rows]
    for b in rows:
        ob = jnp.where(own, o[b], 0.0)
        o64 = ob[:, 0:HEAD_DIM]
        for h in range(1, N_KV_HEADS):
            o64 = o64 + ob[:, h * HEAD_DIM:(h + 1) * HEAD_DIM]
        o_ref[b] = o64 / den[b]
        kto_ref[b] = jnp.where(newest, knc_ref[b], pltpu.roll(kt_ref[b], win - 1, 1))
        vto_ref[b] = jnp.where(newest, vnc_ref[b], pltpu.roll(vt_ref[b], win - 1, 1))


def _attn_sample(q3, k_new, v_new, kt_all, vt_all, l, sink):
    _, nb, _, win = kt_all.shape
    assert win == LANES
    bb = _row_tile(nb, 8)
    row = pl.BlockSpec((bb, 1, KV_WIDTH), lambda i: (i, 0, 0))
    col = pl.BlockSpec((bb, KV_WIDTH, 1), lambda i: (i, 0, 0))
    cache_in = pl.BlockSpec((None, bb, KV_WIDTH, win), lambda i: (l, i, 0, 0))
    cache_out = pl.BlockSpec((bb, KV_WIDTH, win), lambda i: (i, 0, 0))
    heads = pl.BlockSpec((bb, N_Q_HEADS, HEAD_DIM), lambda i: (i, 0, 0))
    return pl.pallas_call(
        _attn_sample_kernel,
        grid=(nb // bb,),
        in_specs=[pl.BlockSpec((N_Q_HEADS, 1), lambda i: (0, 0)), heads, row, row, col, col, cache_in, cache_in],
        out_specs=[heads, cache_out, cache_out],
        out_shape=[jax.ShapeDtypeStruct((nb, N_Q_HEADS, HEAD_DIM), F32),
                   jax.ShapeDtypeStruct((nb, KV_WIDTH, win), F32),
                   jax.ShapeDtypeStruct((nb, KV_WIDTH, win), F32)],
        compiler_params=_cparams("parallel"),
        name="attn_sample",
    )(sink.reshape(N_Q_HEADS, 1), q3, k_new.reshape(nb, 1, KV_WIDTH), v_new.reshape(nb, 1, KV_WIDTH),
      k_new.reshape(nb, KV_WIDTH, 1), v_new.reshape(nb, KV_WIDTH, 1), kt_all, vt_all)


POOL_HALO = 16


def _pool_prompt_kernel(u_ref, halo_ref, w_ref, sc_ref, o_ref, ext_ref):
    i = pl.program_id(1)
    tm = u_ref.shape[0]
    ext_ref[0:POOL_HALO, :] = jnp.where(i > 0, halo_ref[...].astype(F32), 0.0)
    ext_ref[POOL_HALO:POOL_HALO + tm, :] = u_ref[...].astype(F32)
    pos = i * tm + lax.broadcasted_iota(jnp.int32, (tm, 1), 0)
    for gi, w in enumerate(POOL_WINDOWS):
        cols = slice(gi * POOL_GROUP_WIDTH, (gi + 1) * POOL_GROUP_WIDTH)
        u = ext_ref[POOL_HALO:POOL_HALO + tm, cols]
        acc = u
        for back in range(1, w):
            acc = acc + ext_ref[POOL_HALO - back:POOL_HALO - back + tm, cols]
        cnt = jnp.minimum(pos + 1, w).astype(F32)
        d = acc / cnt - u
        o_ref[:, cols] = _bdot(d, w_ref[gi]) * sc_ref[:, cols]


def _pool_prompt(z, pool_w, pool_scale, tm):
    groups, t, _ = z.shape
    ucol = U_OFF // POOL_WIDTH
    per = tm // POOL_HALO
    return pl.pallas_call(
        _pool_prompt_kernel,
        grid=(groups, t // tm),
        in_specs=[pl.BlockSpec((None, tm, POOL_WIDTH), lambda g, i: (g, i, ucol)),
                  pl.BlockSpec((None, POOL_HALO, POOL_WIDTH), lambda g, i: (g, jnp.maximum(i * per - 1, 0), ucol)),
                  pl.BlockSpec((len(POOL_WINDOWS), POOL_GROUP_WIDTH, POOL_GROUP_WIDTH), lambda g, i: (0, 0, 0)),
                  pl.BlockSpec((1, POOL_WIDTH), lambda g, i: (0, 0))],
        out_specs=pl.BlockSpec((None, tm, POOL_WIDTH), lambda g, i: (g, i, 0)),
        out_shape=jax.ShapeDtypeStruct((groups, t, POOL_WIDTH), F32),
        scratch_shapes=[pltpu.VMEM((tm + POOL_HALO, POOL_WIDTH), F32)],
        compiler_params=_cparams("parallel", "parallel"),
        name="pool_prompt",
    )(z, z, pool_w, pool_scale)


def _pool_sample_kernel(u_ref, st_ref, w_ref, sc_ref, o_ref):
    for gi, w in enumerate(POOL_WINDOWS):
        cols = slice(gi * POOL_GROUP_WIDTH, (gi + 1) * POOL_GROUP_WIDTH)
        u = u_ref[:, cols].astype(F32)
        acc = u
        for back in range(1, w):
            acc = acc + st_ref[POOL_PAD - back, :, cols]
        cnt = float(min(PAST_LEN + 1, w))
        d = acc / cnt - u
        o_ref[:, cols] = _bdot(d, w_ref[gi]) * sc_ref[:, cols]


def _pool_sample(z, state_t, pool_w, pool_scale):
    rows = z.shape[1]
    ucol = U_OFF // POOL_WIDTH
    return pl.pallas_call(
        _pool_sample_kernel,
        grid=(1,),
        in_specs=[pl.BlockSpec((None, rows, POOL_WIDTH), lambda i: (0, 0, ucol)),
                  pl.BlockSpec((POOL_PAD, rows, POOL_WIDTH), lambda i: (0, 0, 0)),
                  pl.BlockSpec((len(POOL_WINDOWS), POOL_GROUP_WIDTH, POOL_GROUP_WIDTH), lambda i: (0, 0, 0)),
                  pl.BlockSpec((1, POOL_WIDTH), lambda i: (0, 0))],
        out_specs=pl.BlockSpec((None, rows, POOL_WIDTH), lambda i: (0, 0, 0)),
        out_shape=jax.ShapeDtypeStruct((1, rows, POOL_WIDTH), F32),
        compiler_params=_cparams("arbitrary"),
        name="pool_sample",
    )(z, state_t, pool_w, pool_scale)


PREP_HALO = 16


def _rwkv_prep_kernel(*refs, halo, has_vres):
    it = iter(refs)
    pr_ref, prev_ref = next(it), next(it)
    vf_ref = next(it) if has_vres else None
    mu_ref, w0_ref, a0_ref, kk_ref, ka_ref, rk_ref = (next(it) for _ in range(6))
    w2_ref, a2_ref, g2_ref, ones_ref = (next(it) for _ in range(4))
    if has_vres:
        vw1_ref, vw2_ref, vb_ref = next(it), next(it), next(it)
    r_o, w_o, k_o, v_o, a_o, b_o, g_o, bonus_o = (next(it) for _ in range(8))
    ext_ref = next(it) if halo else None

    pr = pr_ref[...].astype(F32)
    tm = pr.shape[0]
    if halo:
        i = pl.program_id(1)
        ext_ref[0:PREP_HALO, :] = jnp.where(i > 0, prev_ref[...].astype(F32), 0.0)
        ext_ref[PREP_HALO:PREP_HALO + tm, :] = pr
        prev = ext_ref[PREP_HALO - 1:PREP_HALO - 1 + tm, :]
    else:
        prev = prev_ref[...]
    xm = pr + (prev - pr) * mu_ref[...]
    wd = RWKV_WIDTH
    r, k, v, lora_in = xm[:, 0:wd], xm[:, wd:2 * wd], xm[:, 2 * wd:3 * wd], xm[:, 3 * wd:4 * wd]
    ones_bd = ones_ref[...]
    lw = _bdot(jnp.tanh(lora_in), w2_ref[...])
    la = _bdot(lora_in, a2_ref[...])
    g = _bdot(_sigmoid(lora_in), g2_ref[...])
    y = -(w0_ref[...] + lw)
    softplus = jnp.maximum(y, 0.0) + jnp.log(1.0 + jnp.exp(-jnp.abs(y)))
    decay = jnp.exp(-jnp.exp(-softplus - 0.5))
    a = _sigmoid(a0_ref[...] + la)
    if has_vres:
        t2 = _bdot(_bdot(v, vw1_ref[...]), vw2_ref[...])
        v = v + (vf_ref[...] - v) * _sigmoid(vb_ref[...] + t2)
    kk = k * kk_ref[...]
    kk = kk / jnp.maximum(jnp.sqrt(_seg_sum(kk * kk, ones_bd)), 1e-12)
    k = k * (1.0 + (a - 1.0) * ka_ref[...])
    r_o[...] = r
    w_o[...] = decay
    k_o[...] = k
    v_o[...] = v
    a_o[...] = -kk
    b_o[...] = kk * a
    g_o[...] = g
    bonus_o[...] = _seg_sum(r * k * rk_ref[...], ones_bd) * v


def _rwkv_prep(z, prev, v_first, rp, tm):
    groups, t, _ = z.shape
    halo = prev is None
    has_vres = v_first is not None
    prcol = PR_OFF // RWKV_PROJ_PAD
    wd = RWKV_WIDTH
    tok = lambda width: pl.BlockSpec((None, tm, width), lambda g, i: (g, i, 0))
    vec = lambda width: pl.BlockSpec((1, width), lambda g, i: (0, 0))
    mat = lambda a, b: pl.BlockSpec((a, b), lambda g, i: (0, 0))
    args = [z]
    in_specs = [pl.BlockSpec((None, tm, RWKV_PROJ_PAD), lambda g, i: (g, i, prcol))]
    if halo:
        per = tm // PREP_HALO
        args.append(z)
        in_specs.append(pl.BlockSpec((None, PREP_HALO, RWKV_PROJ_PAD),
                                     lambda g, i: (g, jnp.maximum(i * per - 1, 0), prcol)))
    else:
        args.append(prev)
        in_specs.append(tok(RWKV_PROJ_PAD))
    if has_vres:
        args.append(v_first)
        in_specs.append(tok(wd))
    args += [rp["mu"], rp["w0"], rp["a0"], rp["k_k"], rp["k_a"], rp["r_k"],
             rp["w2"], rp["a2"], rp["g2"], rp["ones_bd"]]
    in_specs += [vec(RWKV_PROJ_PAD)] + [vec(wd)] * 5 + [mat(wd, wd)] * 4
    if has_vres:
        args += [rp["vw1"], rp["vw2"], rp["vb"]]
        in_specs += [mat(wd, LANES), mat(LANES, wd), vec(wd)]
    return pl.pallas_call(
        functools.partial(_rwkv_prep_kernel, halo=halo, has_vres=has_vres),
        grid=(groups, t // tm),
        in_specs=in_specs,
        out_specs=[tok(wd)] * 8,
        out_shape=[jax.ShapeDtypeStruct((groups, t, wd), F32)] * 8,
        scratch_shapes=[pltpu.VMEM((tm + PREP_HALO, RWKV_PROJ_PAD), F32)] if halo else [],
        compiler_params=_cparams("parallel", "parallel"),
        name="rwkv_prep",
    )(*args)


SCAN_STEPS_PER_TRIP = 32


def _wkv_scan_kernel(r_ref, w_ref, k_ref, v_ref, a_ref, b_ref, y_ref, s_ref):
    nb, tc, _ = r_ref.shape
    pairs = RWKV_HEADS // 2
    n = RWKV_HEAD_DIM

    @pl.when(pl.program_id(0) == 0)
    def _():
        s_ref[...] = jnp.zeros(s_ref.shape, F32)

    row = lax.broadcasted_iota(jnp.int32, (n, 2 * n), 0)
    lane = lax.broadcasted_iota(jnp.int32, (n, 2 * n), 1)
    diag = (lane % n) == row
    chains = [(bi, slice(p * 2 * n, (p + 1) * 2 * n)) for bi in range(nb) for p in range(pairs)]
    nc = len(chains)
    seg_r = lax.broadcasted_iota(jnp.int32, (2 * n, 2 * n), 0) // n
    seg_c = lax.broadcasted_iota(jnp.int32, (2 * n, 2 * n), 1) // n
    ones_bd = (seg_r == seg_c).astype(BF16)
    stack = lambda xs: jnp.concatenate(xs, axis=0)
    unstack = lambda x: [x[c * n:(c + 1) * n] for c in range(nc)]

    def seg_bcast(x):
        return jnp.dot(x.astype(BF16), ones_bd, preferred_element_type=F32)

    span = SCAN_STEPS_PER_TRIP

    def steps(trip, carry):
        base = pl.multiple_of(trip * span, span)
        rows = pl.ds(base, span)
        tiles = [[ref[bi, rows, sl] for ref in (r_ref, w_ref, k_ref, v_ref, a_ref, b_ref)] for bi, sl in chains]
        s = [s_ref[c] for c in range(nc)]
        ys = [[] for _ in chains]

        def y_rows(ycol):
            for c in range(nc):
                ys[c].append(jnp.sum(jnp.where(diag, ycol[c * n:(c + 1) * n], 0.0), axis=0, keepdims=True))

        vcol = seg_bcast(stack([jnp.where(diag, t[3][slice(u, u + 1)], 0.0) for u in range(span) for t in tiles]))
        sr = None
        for u in range(span):
            one = slice(u, u + 1)
            red = seg_bcast(stack([s[c] * tiles[c][4][one] for c in range(nc)] + (sr or [])))
            if sr:
                y_rows(red[nc * n:])
            for c, (r8, w8, k8, v8, a8, b8) in enumerate(tiles):
                s[c] = (s[c] * w8[one] + red[c * n:(c + 1) * n] * b8[one]
                        + vcol[(u * nc + c) * n:(u * nc + c + 1) * n] * k8[one])
            sr = [s[c] * tiles[c][0][one] for c in range(nc)]
        y_rows(seg_bcast(stack(sr)))
        for c, (bi, sl) in enumerate(chains):
            s_ref[c] = s[c]
            y_ref[bi, rows, sl] = jnp.concatenate(ys[c], axis=0)
        return carry

    lax.fori_loop(0, tc // span, steps, 0)


def _wkv_scan(r, w, k, v, a, b, tc):
    nb, t, wd = r.shape
    pairs = RWKV_HEADS // 2
    tok = pl.BlockSpec((nb, tc, wd), lambda i: (0, i, 0))
    return pl.pallas_call(
        _wkv_scan_kernel,
        grid=(t // tc,),
        in_specs=[tok] * 6,
        out_specs=[tok, pl.BlockSpec((nb * pairs, RWKV_HEAD_DIM, 2 * RWKV_HEAD_DIM), lambda i: (0, 0, 0))],
        out_shape=[jax.ShapeDtypeStruct((nb, t, wd), F32),
                   jax.ShapeDtypeStruct((nb * pairs, RWKV_HEAD_DIM, 2 * RWKV_HEAD_DIM), F32)],
        compiler_params=_cparams("arbitrary"),
        name="wkv_scan",
    )(r, w, k, v, a, b)


def _wkv_step_kernel(r_ref, w_ref, k_ref, v_ref, a_ref, b_ref, s_ref, y_ref, so_ref):
    heads = s_ref.shape[0]
    n = RWKV_HEAD_DIM
    diag = lax.broadcasted_iota(jnp.int32, (n, n), 0) == lax.broadcasted_iota(jnp.int32, (n, n), 1)

    def body(h8, carry):
        base = pl.multiple_of(h8 * SUBLANES, SUBLANES)
        rows = pl.ds(base, SUBLANES)
        r8, w8, k8, v8, a8, b8 = (ref[rows, :] for ref in (r_ref, w_ref, k_ref, v_ref, a_ref, b_ref))
        ones = [slice(u, u + 1) for u in range(SUBLANES)]
        s = [s_ref[base + u] for u in range(SUBLANES)]
        sa = [jnp.sum(s[u] * a8[one], axis=1, keepdims=True) for u, one in enumerate(ones)]
        vcol = [jnp.sum(jnp.where(diag, v8[one], 0.0), axis=1, keepdims=True) for one in ones]
        s = [s[u] * w8[one] + sa[u] * b8[one] + vcol[u] * k8[one] for u, one in enumerate(ones)]
        for u in range(SUBLANES):
            so_ref[base + u] = s[u]
        ycol = [jnp.sum(s[u] * r8[one], axis=1, keepdims=True) for u, one in enumerate(ones)]
        y_ref[rows, :] = jnp.concatenate(
            [jnp.sum(jnp.where(diag, yc, 0.0), axis=0, keepdims=True) for yc in ycol], axis=0)
        return carry

    lax.fori_loop(0, heads // SUBLANES, body, 0)


def _wkv_step(r, w, k, v, a, b, state, l):
    heads = state.shape[1]
    hb = _row_tile(heads, 128)
    n = RWKV_HEAD_DIM
    vec = pl.BlockSpec((hb, n), lambda i: (i, 0))
    st = pl.BlockSpec((hb, n, n), lambda i: (i, 0, 0))
    return pl.pallas_call(
        _wkv_step_kernel,
        grid=(heads // hb,),
        in_specs=[vec] * 6 + [pl.BlockSpec((None, hb, n, n), lambda i: (l, i, 0, 0))],
        out_specs=[vec, st],
        out_shape=[jax.ShapeDtypeStruct((heads, n), F32), jax.ShapeDtypeStruct((heads, n, n), F32)],
        compiler_params=_cparams("parallel"),
        name="wkv_step",
    )(r, w, k, v, a, b, state)


def _rwkv_post_kernel(y_ref, bonus_ref, g_ref, lg_ref, lb_ref, ones_ref, o_ref):
    y = y_ref[...]
    ones_bd = ones_ref[...]
    inv = 1.0 / RWKV_HEAD_DIM
    d = y - _seg_sum(y, ones_bd) * inv
    var = _seg_sum(d * d, ones_bd) * inv
    yn = d * lax.rsqrt(var + LNX_EPS) * lg_ref[...] + lb_ref[...]
    o_ref[...] = ((yn + bonus_ref[...]) * g_ref[...]).astype(o_ref.dtype)


def _rwkv_post(y, bonus, g, rp, tm):
    groups, t, wd = y.shape
    tok = pl.BlockSpec((None, tm, wd), lambda gi, i: (gi, i, 0))
    vec = pl.BlockSpec((1, wd), lambda gi, i: (0, 0))
    return pl.pallas_call(
        _rwkv_post_kernel,
        grid=(groups, t // tm),
        in_specs=[tok, tok, tok, vec, vec, pl.BlockSpec((wd, wd), lambda gi, i: (0, 0))],
        out_specs=tok,
        out_shape=jax.ShapeDtypeStruct((groups, t, wd), BF16),
        compiler_params=_cparams("parallel", "parallel"),
        name="rwkv_post",
    )(y, bonus, g, rp["lnx_g"], rp["lnx_b"], rp["ones_bd"])


MERGE_TN = 512


def _merge_kernel(oa_ref, zp_ref, yr_ref, ga_ref, gp_ref, gr_ref, wa_ref, wp_ref, wr_ref, o_ref):
    gate = lambda ref: _sigmoid(ref[...].astype(F32))
    merged = (gate(ga_ref) * _bdot(oa_ref[...], wa_ref[...])
              + gate(gp_ref) * _bdot(zp_ref[...], wp_ref[...])
              + gate(gr_ref) * _bdot(yr_ref[...], wr_ref[...]))
    o_ref[...] = merged.astype(o_ref.dtype)


def _merge(o_attn, z_pool, y_rwkv, z, w_attn_out, w_pool_out, w_rwkv_out, l, tm):
    groups, t, _ = z.shape
    tn = MERGE_TN
    gate = lambda br: pl.BlockSpec((None, tm, tn), lambda g, i, j: (g, i, (GL_OFF + br * D_MODEL) // tn + j))
    tok = lambda width: pl.BlockSpec((None, tm, width), lambda g, i, j: (g, i, 0))
    wt = lambda rows: pl.BlockSpec((None, rows, tn), lambda g, i, j: (l, 0, j))
    return pl.pallas_call(
        _merge_kernel,
        grid=(groups, t // tm, D_MODEL // tn),
        in_specs=[tok(ATTN_WIDTH), tok(POOL_WIDTH), tok(RWKV_WIDTH), gate(0), gate(1), gate(2),
                  wt(ATTN_WIDTH), wt(POOL_WIDTH), wt(RWKV_WIDTH)],
        out_specs=pl.BlockSpec((None, tm, tn), lambda g, i, j: (g, i, j)),
        out_shape=jax.ShapeDtypeStruct((groups, t, D_MODEL), BF16),
        compiler_params=_cparams("parallel", "parallel", "arbitrary"),
        name="merge",
    )(o_attn, z_pool, y_rwkv, z, z, z, w_attn_out, w_pool_out, w_rwkv_out)


def _proj_residual_kernel(a_ref, w_ref, x_ref, gt_ref, o_ref):
    o_ref[...] = x_ref[...] + (1.0 + gt_ref[...]) * _bdot(a_ref[...], w_ref[...])


def _proj_residual(a, w, l, x, mod, chunk, tm, tn):
    groups, t, kdim = a.shape
    per_row = mod.shape[1] != 1
    gate = pl.BlockSpec((None, tm if per_row else 1, tn),
                        lambda g, i, j: (g, i if per_row else 0, chunk * (D_MODEL // tn) + j))
    return pl.pallas_call(
        _proj_residual_kernel,
        grid=(groups, t // tm, D_MODEL // tn),
        in_specs=[pl.BlockSpec((None, tm, kdim), lambda g, i, j: (g, i, 0)),
                  pl.BlockSpec((None, kdim, tn), lambda g, i, j: (l, 0, j)),
                  pl.BlockSpec((None, tm, tn), lambda g, i, j: (g, i, j)),
                  gate],
        out_specs=pl.BlockSpec((None, tm, tn), lambda g, i, j: (g, i, j)),
        out_shape=jax.ShapeDtypeStruct((groups, t, D_MODEL), F32),
        compiler_params=_cparams("parallel", "parallel", "arbitrary"),
        name="proj_residual",
    )(a, w, x, mod)


FFN_TN = 256


def _ffn_up_kernel(x_ref, g_ref, sh_ref, sc_ref, wg_ref, wu_ref, o_ref, h_ref):
    @pl.when(pl.program_id(2) == 0)
    def _():
        h_ref[...] = _mod_norm(x_ref[...], g_ref[...], sh_ref[...], sc_ref[...]).astype(BF16)

    h = h_ref[...]
    gate = jnp.dot(h, wg_ref[...].astype(BF16), preferred_element_type=F32)
    up = jnp.dot(h, wu_ref[...].astype(BF16), preferred_element_type=F32)
    o_ref[...] = (gate * _sigmoid(gate) * up).astype(o_ref.dtype)


def _ffn_up(x, g2, mod, w_gate, w_up, l, tm):
    groups, t, _ = x.shape
    tn = FFN_TN
    wt = pl.BlockSpec((None, D_MODEL, tn), lambda g, i, j: (l, 0, j))
    return pl.pallas_call(
        _ffn_up_kernel,
        grid=(groups, t // tm, D_FF // tn),
        in_specs=[pl.BlockSpec((None, tm, D_MODEL), lambda g, i, j: (g, i, 0), pipeline_mode=pl.Buffered(1)),
                  pl.BlockSpec((1, D_MODEL), lambda g, i, j: (0, 0)),
                  _mod_spec(mod, tm, 3), _mod_spec(mod, tm, 4), wt, wt],
        out_specs=pl.BlockSpec((None, tm, tn), lambda g, i, j: (g, i, j)),
        out_shape=jax.ShapeDtypeStruct((groups, t, D_FF), BF16),
        scratch_shapes=[pltpu.VMEM((tm, D_MODEL), BF16)],
        compiler_params=_cparams("parallel", "parallel", "arbitrary"),
        name="ffn_up",
    )(x, g2, mod, mod, w_gate, w_up)


def _final_norm_kernel(x_ref, g_ref, o_ref):
    x = x_ref[...]
    ms = jnp.mean(x * x, axis=-1, keepdims=True)
    o_ref[...] = x * lax.rsqrt(ms + RMS_EPS) * g_ref[...]


def _final_norm(x, g, tm):
    groups, t, _ = x.shape
    tok = pl.BlockSpec((None, tm, D_MODEL), lambda gi, i: (gi, i, 0))
    return pl.pallas_call(
        _final_norm_kernel,
        grid=(groups, t // tm),
        in_specs=[tok, pl.BlockSpec((1, D_MODEL), lambda gi, i: (0, 0))],
        out_specs=tok,
        out_shape=jax.ShapeDtypeStruct(x.shape, F32),
        compiler_params=_cparams("parallel", "parallel"),
        name="final_norm",
    )(x, g)


def _pad_cols(a, width):
    return jnp.pad(a, [(0, 0)] * (a.ndim - 1) + [(0, width - a.shape[-1])])


def _rwkv_params(l, rwkv_mu, rwkv_w0, rwkv_w2, rwkv_a0, rwkv_a2, rwkv_g2, rwkv_k_k, rwkv_k_a, rwkv_r_k,
                 rwkv_lnx_g, rwkv_lnx_b, vres_w1, vres_w2, vres_b):
    wd = RWKV_WIDTH
    row = lambda a: a.reshape(1, -1)
    lora_rows = lambda w, off: jnp.zeros((wd, wd), F32).at[off:off + w.shape[0]].set(w).astype(BF16)
    seg = np.arange(wd) // RWKV_HEAD_DIM
    rp = dict(
        mu=_pad_cols(row(rwkv_mu[l]), RWKV_PROJ_PAD), w0=row(rwkv_w0[l]), a0=row(rwkv_a0[l]),
        k_k=row(rwkv_k_k[l]), k_a=row(rwkv_k_a[l]), r_k=row(rwkv_r_k[l]),
        w2=lora_rows(rwkv_w2[l], 0), a2=lora_rows(rwkv_a2[l], DECAY_LORA),
        g2=lora_rows(rwkv_g2[l], DECAY_LORA + ICLR_LORA),
        ones_bd=jnp.asarray(seg[:, None] == seg[None, :], BF16),
        lnx_g=row(rwkv_lnx_g[l]), lnx_b=row(rwkv_lnx_b[l]))
    if l > 0:
        rp.update(vw1=_pad_cols(vres_w1[l - 1], LANES).astype(BF16),
                  vw2=jnp.pad(vres_w2[l - 1], ((0, LANES - VRES_LORA), (0, 0))).astype(BF16),
                  vb=row(vres_b[l - 1]))
    return rp


def _layer_common(x, mod, z, o_attn, z_pool, y, bonus, g, lw, l, rp, tm):
    y_rwkv = _rwkv_post(y, bonus, g, rp, _row_tile(x.shape[1], 512))
    merged = _merge(o_attn, z_pool, y_rwkv, z, lw["w_attn_out"], lw["w_pool_out"], lw["w_rwkv_out"], l, tm)
    x = _proj_residual(merged, lw["w_o"], l, x, mod, 2, tm, 512)
    hidden = _ffn_up(x, lw["norm2_g"][l].reshape(1, -1), mod, lw["w_ffn_gate"], lw["w_ffn_up"], l, tm)
    return _proj_residual(hidden, lw["w_ffn_down"], l, x, mod, 5, _row_tile(x.shape[1], NORM_ROWS), 256)


def kernel(x_prompt, x_sample, c_prompt, c_sample, cache_k_win, cache_v_win, state_pool, state_shift, state_wkv, ada_w, ada_b, norm1_g, norm2_g, w_in, attn_sink, w_attn_out, pool_w, pool_scale, w_pool_out, rwkv_mu, rwkv_w0, rwkv_w2, rwkv_a0, rwkv_a2, rwkv_g2, rwkv_k_k, rwkv_k_a, rwkv_r_k, rwkv_lnx_g, rwkv_lnx_b, w_rwkv_out, vres_w1, vres_w2, vres_b, w_o, w_ffn_gate, w_ffn_up, w_ffn_down, final_norm_g):
    depth = ada_w.shape[0]
    nb, t, _ = x_prompt.shape
    ns = x_sample.shape[0]
    assert x_sample.shape[1] == 1 and t % WINDOW == 0
    win = cache_k_win.shape[2]
    assert win <= WINDOW
    heads, n = RWKV_HEADS, RWKV_HEAD_DIM

    pad_rows = -nb % SUBLANES
    c_all = jnp.concatenate([c_prompt, jnp.zeros((pad_rows, D_MODEL), F32), c_sample], axis=0)
    mod_all = _ada(c_all, ada_w, ada_b)

    cos_p, sin_p = _rope_tables(jnp.arange(t))
    cos_s, sin_s = _rope_tables(jnp.full((ns,), PAST_LEN))

    xp = x_prompt
    xs = x_sample.reshape(1, ns, D_MODEL)
    tm_p = _row_tile(t, MATMUL_ROWS)
    tm_s = _row_tile(ns, MATMUL_ROWS)
    vf_p = vf_s = None
    st_p, st_s = [], []
    lw = dict(norm2_g=norm2_g, w_attn_out=w_attn_out, w_pool_out=w_pool_out, w_rwkv_out=w_rwkv_out, w_o=w_o,
              w_ffn_gate=w_ffn_gate, w_ffn_up=w_ffn_up, w_ffn_down=w_ffn_down)
    w_in_t = w_in.transpose(0, 2, 1)
    to_feature_major = lambda c: c.transpose(0, 1, 3, 4, 2).reshape(depth, ns, KV_WIDTH, win)
    from_feature_major = lambda c: c.reshape(ns, N_KV_HEADS, HEAD_DIM, win).transpose(0, 3, 1, 2)
    cache_kt, cache_vt = to_feature_major(cache_k_win), to_feature_major(cache_v_win)
    state_heads = state_wkv.reshape(depth, ns * heads, n, n)
    for l in range(depth):
        rp = _rwkv_params(l, rwkv_mu, rwkv_w0, rwkv_w2, rwkv_a0, rwkv_a2, rwkv_g2, rwkv_k_k, rwkv_k_a, rwkv_r_k,
                          rwkv_lnx_g, rwkv_lnx_b, vres_w1, vres_w2, vres_b)
        g1 = norm1_g[l].reshape(1, -1)
        mod_p = mod_all[l, :nb].reshape(nb, 1, -1)
        mod_s = mod_all[l, nb + pad_rows:].reshape(1, ns, -1)

        z = _in_proj(xp, g1, mod_p, w_in_t, l, cos_p, sin_p, tm_p)
        o_attn = _attn_prompt(z, attn_sink[l])
        z_pool = _pool_prompt(z, pool_w[l], pool_scale[l].reshape(1, -1), _row_tile(t, 512))
        r, w, k, v, a, b, g, bonus = _rwkv_prep(z, None, vf_p, rp, _row_tile(t, 256))
        if l == 0:
            vf_p = v
        y, s_pair = _wkv_scan(r, w, k, v, a, b, _row_tile(t, 256))
        wkv_new = s_pair.reshape(nb, heads // 2, n, 2, n).transpose(0, 1, 3, 2, 4).reshape(nb, heads, n, n)
        tail = z[:, t - WINDOW:, :GL_OFF].astype(F32)
        st_p.append((tail[:, :, K_OFF:V_OFF].reshape(nb, WINDOW, N_KV_HEADS, HEAD_DIM),
                     tail[:, :, V_OFF:U_OFF].reshape(nb, WINDOW, N_KV_HEADS, HEAD_DIM),
                     tail[:, WINDOW - POOL_PAD:, U_OFF:PR_OFF],
                     tail[:, WINDOW - 1, PR_OFF:PR_OFF + RWKV_PROJ],
                     wkv_new))
        xp = _layer_common(xp, mod_p, z, o_attn, z_pool, y, bonus, g, lw, l, rp, tm_p)

        z = _in_proj(xs, g1, mod_s, w_in_t, l, cos_s, sin_s, tm_s)
        zs = z[0, :, :GL_OFF].astype(F32)
        k_new, v_new = zs[:, K_OFF:V_OFF], zs[:, V_OFF:U_OFF]
        o3, kt_new, vt_new = _attn_sample(zs[:, :ATTN_WIDTH].reshape(ns, N_Q_HEADS, HEAD_DIM), k_new, v_new,
                                          cache_kt, cache_vt, l, attn_sink[l])
        o_attn = o3.reshape(1, ns, ATTN_WIDTH).astype(BF16)
        z_pool = _pool_sample(z, state_pool[l].transpose(1, 0, 2), pool_w[l], pool_scale[l].reshape(1, -1))
        prev = _pad_cols(state_shift[l], RWKV_PROJ_PAD).reshape(1, ns, RWKV_PROJ_PAD)
        r, w, k, v, a, b, g, bonus = _rwkv_prep(z, prev, vf_s, rp, tm_s)
        if l == 0:
            vf_s = v
        per_head = lambda arr: arr.reshape(ns * heads, n)
        y, wkv_new = _wkv_step(*(per_head(arr) for arr in (r, w, k, v, a, b)), state_heads, l)
        u_new = zs[:, None, U_OFF:PR_OFF]
        st_s.append((from_feature_major(kt_new), from_feature_major(vt_new),
                     jnp.concatenate([state_pool[l][:, 1:], u_new], axis=1),
                     zs[:, PR_OFF:PR_OFF + RWKV_PROJ],
                     wkv_new.reshape(ns, heads, n, n)))
        xs = _layer_common(xs, mod_s, z, o_attn, z_pool, y.reshape(1, ns, RWKV_WIDTH), bonus, g, lw, l, rp, tm_s)

    g_fin = final_norm_g.reshape(1, -1)
    y_prompt = _final_norm(xp, g_fin, _row_tile(t, NORM_ROWS))
    y_sample = _final_norm(xs, g_fin, _row_tile(ns, NORM_ROWS)).reshape(ns, 1, D_MODEL)
    stack = lambda states, i: jnp.stack([s[i] for s in states])
    return (y_prompt, y_sample) + tuple(stack(st_p, i) for i in range(5)) + tuple(stack(st_s, i) for i in range(5))
```

```python
import functools

import numpy as np
import jax
import jax.numpy as jnp
from jax import lax
from jax.experimental import pallas as pl
from jax.experimental.pallas import tpu as pltpu

F32 = jnp.float32
BF16 = jnp.bfloat16

D_MODEL = 2048
HEAD_DIM = 64
N_Q_HEADS = 16
N_KV_HEADS = 4
GROUP = N_Q_HEADS // N_KV_HEADS
ATTN_WIDTH = N_Q_HEADS * HEAD_DIM
KV_WIDTH = N_KV_HEADS * HEAD_DIM
WINDOW = 128
ROPE_THETA = 10000.0
POOL_WIDTH = 512
POOL_WINDOWS = (2, 4, 8, 16)
POOL_GROUP_WIDTH = 128
POOL_PAD = 15
RWKV_HEAD_DIM = 64
RWKV_WIDTH = 512
RWKV_HEADS = 8
DECAY_LORA = 96
ICLR_LORA = 96
GATE_LORA = 256
VRES_LORA = 32
RWKV_PROJ = 3 * RWKV_WIDTH + DECAY_LORA + ICLR_LORA + GATE_LORA
RWKV_PROJ_PAD = 2048
D_FF = 5632
PAST_LEN = 8192
RMS_EPS = 1e-6
LNX_EPS = 64e-5
NEG_INF = -1e30

Q_OFF = 0
K_OFF = ATTN_WIDTH
V_OFF = K_OFF + KV_WIDTH
U_OFF = V_OFF + KV_WIDTH
PR_OFF = U_OFF + POOL_WIDTH
GL_OFF = PR_OFF + RWKV_PROJ_PAD
IN_PAD = GL_OFF + 3 * D_MODEL
ROPE_END = V_OFF

LANES = 128
SUBLANES = 8
VMEM_LIMIT = 56 * 2**20

MATMUL_ROWS = 2048
NORM_ROWS = 1024


def _cparams(*sem):
    return pltpu.CompilerParams(dimension_semantics=sem, vmem_limit_bytes=VMEM_LIMIT)


def _row_tile(t, pref):
    tm = min(t, pref)
    while t % tm:
        tm -= SUBLANES
    return tm


def _sigmoid(x):
    return 1.0 / (1.0 + jnp.exp(-x))


def _mod_norm(x, g, sh, sc):
    ms = jnp.mean(x * x, axis=-1, keepdims=True)
    return (x * lax.rsqrt(ms + RMS_EPS) * g) * (1.0 + sc) + sh


def _bdot(a, b):
    return jnp.dot(a.astype(BF16), b.astype(BF16), preferred_element_type=F32)


def _seg_sum(x, ones_bd):
    hi = x.astype(BF16)
    lo = (x - hi.astype(F32)).astype(BF16)
    return (jnp.dot(hi, ones_bd, preferred_element_type=F32)
            + jnp.dot(lo, ones_bd, preferred_element_type=F32))


def _mod_spec(mod, tm, chunk):
    per_row = mod.shape[1] != 1
    rows = tm if per_row else 1
    return pl.BlockSpec((None, rows, D_MODEL), lambda g, i, j=0: (g, i if per_row else 0, chunk))


def _ada_kernel(c_ref, w_ref, b_ref, o_ref):
    c = c_ref[...]
    o_ref[...] = _bdot(c * _sigmoid(c), w_ref[...]) + b_ref[...]


def _ada(c_all, ada_w, ada_b):
    depth, _, width = ada_w.shape
    rows = c_all.shape[0]
    tn = 1024
    return pl.pallas_call(
        _ada_kernel,
        grid=(depth, width // tn),
        in_specs=[pl.BlockSpec((rows, D_MODEL), lambda l, j: (0, 0)),
                  pl.BlockSpec((None, D_MODEL, tn), lambda l, j: (l, 0, j)),
                  pl.BlockSpec((None, 1, tn), lambda l, j: (l, 0, j))],
        out_specs=pl.BlockSpec((None, rows, tn), lambda l, j: (l, 0, j)),
        out_shape=jax.ShapeDtypeStruct((depth, rows, width), F32),
        compiler_params=_cparams("parallel", "parallel"),
        name="ada",
    )(c_all, ada_w, ada_b.reshape(depth, 1, width))


IN_TN = 512
IN_MAIN_TILES = GL_OFF // IN_TN


def _in_proj_kernel(x_ref, g_ref, sh_ref, sc_ref, cos_ref, sin_ref, w_ref, o_ref, h_ref):
    j = pl.program_id(2)

    @pl.when(j == 0)
    def _():
        h_ref[...] = _mod_norm(x_ref[...], g_ref[...], sh_ref[...], sc_ref[...]).astype(BF16)

    def project():
        return lax.dot_general(h_ref[...], w_ref[0].astype(BF16), (((1,), (1,)), ((), ())),
                               preferred_element_type=F32)

    @pl.when(j * IN_TN < ROPE_END)
    def _():
        acc = project()
        reps = IN_TN // LANES
        cos = jnp.concatenate([cos_ref[...]] * reps, axis=1)
        sin = jnp.concatenate([sin_ref[...]] * reps, axis=1)
        lane = lax.broadcasted_iota(jnp.int32, acc.shape, 1)
        first_half = (lane % HEAD_DIM) < (HEAD_DIM // 2)
        rot = jnp.where(first_half, pltpu.roll(acc, IN_TN - HEAD_DIM // 2, 1),
                        pltpu.roll(acc, HEAD_DIM // 2, 1))
        roped = acc * cos + rot * sin
        o_ref[...] = jnp.where(j * IN_TN + lane < ROPE_END, roped, acc).astype(o_ref.dtype)

    @pl.when(j * IN_TN >= ROPE_END)
    def _():
        o_ref[...] = project().astype(o_ref.dtype)


def _in_proj(x, g1, mod, w_in_t, l, cos, sin, tm):
    groups, t, _ = x.shape
    gate_shift = RWKV_PROJ_PAD - RWKV_PROJ

    def weight_rows(g, i, j):
        start = jnp.where(j < IN_MAIN_TILES, j * IN_TN, j * IN_TN - gate_shift)
        return (l, pl.multiple_of(start, SUBLANES), 0)

    return pl.pallas_call(
        _in_proj_kernel,
        grid=(groups, t // tm, IN_PAD // IN_TN),
        in_specs=[pl.BlockSpec((None, tm, D_MODEL), lambda g, i, j: (g, i, 0), pipeline_mode=pl.Buffered(1)),
                  pl.BlockSpec((1, D_MODEL), lambda g, i, j: (0, 0)),
                  _mod_spec(mod, tm, 0), _mod_spec(mod, tm, 1),
                  pl.BlockSpec((tm, LANES), lambda g, i, j: (i, 0), pipeline_mode=pl.Buffered(1)),
                  pl.BlockSpec((tm, LANES), lambda g, i, j: (i, 0), pipeline_mode=pl.Buffered(1)),
                  pl.BlockSpec((pl.Element(1), pl.Element(IN_TN), pl.Element(D_MODEL)), weight_rows)],
        out_specs=pl.BlockSpec((None, tm, IN_TN), lambda g, i, j: (g, i, j)),
        out_shape=jax.ShapeDtypeStruct((groups, t, IN_PAD), BF16),
        scratch_shapes=[pltpu.VMEM((tm, D_MODEL), BF16)],
        compiler_params=_cparams("parallel", "parallel", "arbitrary"),
        name="in_proj",
    )(x, g1, mod, mod, cos, sin, w_in_t)


def _rope_tables(pos):
    half = HEAD_DIM // 2
    inv = ROPE_THETA ** (-jnp.arange(half, dtype=F32) * 2.0 / HEAD_DIM)
    ang = pos.astype(F32)[:, None] * inv[None, :]
    cos, sin = jnp.cos(ang), jnp.sin(ang)
    reps = LANES // HEAD_DIM
    cos_t = jnp.concatenate([cos, cos] * reps, axis=1)
    sin_t = jnp.concatenate([-sin, sin] * reps, axis=1)
    return cos_t, sin_t


def _attn_prompt_kernel(sink_ref, q_ref, k_ref, v_ref, kp_ref, vp_ref, o_ref):
    n = pl.program_id(1)
    blk = q_ref.shape[0]
    q = q_ref[...]
    k2 = jnp.concatenate([kp_ref[...], k_ref[...]], axis=0).astype(BF16)
    v2 = jnp.concatenate([vp_ref[...], v_ref[...]], axis=0).astype(BF16)
    qi = lax.broadcasted_iota(jnp.int32, (blk, 2 * blk), 0)
    kc = lax.broadcasted_iota(jnp.int32, (blk, 2 * blk), 1)
    rel = qi - kc + blk
    first_key = jnp.where(n > 0, 0, blk)
    mask = (rel >= 0) & (rel <= WINDOW) & (kc >= first_key)
    outs = []
    for hd in range(N_Q_HEADS):
        h = hd // GROUP
        kh = k2[:, h * HEAD_DIM:(h + 1) * HEAD_DIM]
        vh = v2[:, h * HEAD_DIM:(h + 1) * HEAD_DIM]
        qh = q[:, hd * HEAD_DIM:(hd + 1) * HEAD_DIM].astype(BF16)
        s = lax.dot_general(qh, kh, (((1,), (1,)), ((), ())), preferred_element_type=F32) * (HEAD_DIM ** -0.5)
        s = jnp.where(mask, s, NEG_INF)
        sk = sink_ref[hd]
        m = jnp.maximum(jnp.max(s, axis=1, keepdims=True), sk)
        p = jnp.exp(s - m)
        den = jnp.sum(p, axis=1, keepdims=True) + jnp.exp(sk - m)
        outs.append(jnp.dot(p.astype(BF16), vh, preferred_element_type=F32) / den)
    o_ref[...] = jnp.concatenate(outs, axis=1).astype(o_ref.dtype)


def _attn_prompt(z, sink):
    groups, t, _ = z.shape
    blk = WINDOW
    kcol, vcol = K_OFF // KV_WIDTH, V_OFF // KV_WIDTH
    prev = lambda col: (lambda g, n: (g, jnp.maximum(n - 1, 0), col))
    return pl.pallas_call(
        _attn_prompt_kernel,
        grid=(groups, t // blk),
        in_specs=[pl.BlockSpec(memory_space=pltpu.SMEM),
                  pl.BlockSpec((None, blk, ATTN_WIDTH), lambda g, n: (g, n, 0)),
                  pl.BlockSpec((None, blk, KV_WIDTH), lambda g, n: (g, n, kcol)),
                  pl.BlockSpec((None, blk, KV_WIDTH), lambda g, n: (g, n, vcol)),
                  pl.BlockSpec((None, blk, KV_WIDTH), prev(kcol)),
                  pl.BlockSpec((None, blk, KV_WIDTH), prev(vcol))],
        out_specs=pl.BlockSpec((None, blk, ATTN_WIDTH), lambda g, n: (g, n, 0)),
        out_shape=jax.ShapeDtypeStruct((groups, t, ATTN_WIDTH), BF16),
        compiler_params=_cparams("parallel", "parallel"),
        name="attn_prompt",
    )(sink, z, z, z, z, z)


def _attn_sample_kernel(sink_ref, q_ref, kn_ref, vn_ref, knc_ref, vnc_ref, kt_ref, vt_ref, o_ref, kto_ref, vto_ref):
    bb = q_ref.shape[0]
    win = kt_ref.shape[2]
    row = lax.broadcasted_iota(jnp.int32, (N_Q_HEADS, KV_WIDTH), 0)
    lane = lax.broadcasted_iota(jnp.int32, (N_Q_HEADS, KV_WIDTH), 1)
    own = (row // GROUP) == (lane // HEAD_DIM)
    newest = lax.broadcasted_iota(jnp.int32, (KV_WIDTH, win), 1) == win - 1
    sink = sink_ref[...]
    scale = HEAD_DIM ** -0.5

    rows = range(bb)
    qbd = [jnp.where(own, jnp.concatenate([q_ref[b]] * N_KV_HEADS, axis=1), 0.0) for b in rows]
    s = [jnp.dot(qbd[b].astype(BF16), kt_ref[b].astype(BF16), preferred_element_type=F32) * scale for b in rows]
    s_new = [jnp.sum(qbd[b] * kn_ref[b], axis=1, keepdims=True) * scale for b in rows]
    m = [jnp.maximum(jnp.maximum(jnp.max(s[b], axis=1, keepdims=True), s_new[b]), sink) for b in rows]
    p = [jnp.exp(s[b] - m[b]) for b in rows]
    p_new = [jnp.exp(s_new[b] - m[b]) for b in rows]
    den = [jnp.sum(p[b], axis=1, keepdims=True) + p_new[b] + jnp.exp(sink - m[b]) for b in rows]
    o = [lax.dot_general(p[b].astype(BF16), vt_ref[b].astype(BF16), (((1,), (1,)), ((), ())),
                         preferred_element_type=F32) + p_new[b] * vn_ref[b] for b in rows]
    for b in rows:
        ob = jnp.where(own, o[b], 0.0)
        o64 = ob[:, 0:HEAD_DIM]
        for h in range(1, N_KV_HEADS):
            o64 = o64 + ob[:, h * HEAD_DIM:(h + 1) * HEAD_DIM]
        o_ref[b] = o64 / den[b]
        kto_ref[b] = jnp.where(newest, knc_ref[b], pltpu.roll(kt_ref[b], win - 1, 1))
        vto_ref[b] = jnp.where(newest, vnc_ref[b], pltpu.roll(vt_ref[b], win - 1, 1))


def _attn_sample(q3, k_new, v_new, kt_all, vt_all, l, sink):
    _, nb, _, win = kt_all.shape
    assert win == LANES
    bb = _row_tile(nb, 8)
    row = pl.BlockSpec((bb, 1, KV_WIDTH), lambda i: (i, 0, 0))
    col = pl.BlockSpec((bb, KV_WIDTH, 1), lambda i: (i, 0, 0))
    cache_in = pl.BlockSpec((None, bb, KV_WIDTH, win), lambda i: (l, i, 0, 0))
    cache_out = pl.BlockSpec((bb, KV_WIDTH, win), lambda i: (i, 0, 0))
    heads = pl.BlockSpec((bb, N_Q_HEADS, HEAD_DIM), lambda i: (i, 0, 0))
    return pl.pallas_call(
        _attn_sample_kernel,
        grid=(nb // bb,),
        in_specs=[pl.BlockSpec((N_Q_HEADS, 1), lambda i: (0, 0)), heads, row, row, col, col, cache_in, cache_in],
        out_specs=[heads, cache_out, cache_out],
        out_shape=[jax.ShapeDtypeStruct((nb, N_Q_HEADS, HEAD_DIM), F32),
                   jax.ShapeDtypeStruct((nb, KV_WIDTH, win), F32),
                   jax.ShapeDtypeStruct((nb, KV_WIDTH, win), F32)],
        compiler_params=_cparams("parallel"),
        name="attn_sample",
    )(sink.reshape(N_Q_HEADS, 1), q3, k_new.reshape(nb, 1, KV_WIDTH), v_new.reshape(nb, 1, KV_WIDTH),
      k_new.reshape(nb, KV_WIDTH, 1), v_new.reshape(nb, KV_WIDTH, 1), kt_all, vt_all)


POOL_HALO = 16


def _pool_prompt_kernel(u_ref, halo_ref, w_ref, sc_ref, o_ref, ext_ref):
    i = pl.program_id(1)
    tm = u_ref.shape[0]
    ext_ref[0:POOL_HALO, :] = jnp.where(i > 0, halo_ref[...].astype(F32), 0.0)
    ext_ref[POOL_HALO:POOL_HALO + tm, :] = u_ref[...].astype(F32)
    pos = i * tm + lax.broadcasted_iota(jnp.int32, (tm, 1), 0)
    for gi, w in enumerate(POOL_WINDOWS):
        cols = slice(gi * POOL_GROUP_WIDTH, (gi + 1) * POOL_GROUP_WIDTH)
        u = ext_ref[POOL_HALO:POOL_HALO + tm, cols]
        acc = u
        for back in range(1, w):
            acc = acc + ext_ref[POOL_HALO - back:POOL_HALO - back + tm, cols]
        cnt = jnp.minimum(pos + 1, w).astype(F32)
        d = acc / cnt - u
        o_ref[:, cols] = _bdot(d, w_ref[gi]) * sc_ref[:, cols]


def _pool_prompt(z, pool_w, pool_scale, tm):
    groups, t, _ = z.shape
    ucol = U_OFF // POOL_WIDTH
    per = tm // POOL_HALO
    return pl.pallas_call(
        _pool_prompt_kernel,
        grid=(groups, t // tm),
        in_specs=[pl.BlockSpec((None, tm, POOL_WIDTH), lambda g, i: (g, i, ucol)),
                  pl.BlockSpec((None, POOL_HALO, POOL_WIDTH), lambda g, i: (g, jnp.maximum(i * per - 1, 0), ucol)),
                  pl.BlockSpec((len(POOL_WINDOWS), POOL_GROUP_WIDTH, POOL_GROUP_WIDTH), lambda g, i: (0, 0, 0)),
                  pl.BlockSpec((1, POOL_WIDTH), lambda g, i: (0, 0))],
        out_specs=pl.BlockSpec((None, tm, POOL_WIDTH), lambda g, i: (g, i, 0)),
        out_shape=jax.ShapeDtypeStruct((groups, t, POOL_WIDTH), F32),
        scratch_shapes=[pltpu.VMEM((tm + POOL_HALO, POOL_WIDTH), F32)],
        compiler_params=_cparams("parallel", "parallel"),
        name="pool_prompt",
    )(z, z, pool_w, pool_scale)


def _pool_sample_kernel(u_ref, st_ref, w_ref, sc_ref, o_ref):
    for gi, w in enumerate(POOL_WINDOWS):
        cols = slice(gi * POOL_GROUP_WIDTH, (gi + 1) * POOL_GROUP_WIDTH)
        u = u_ref[:, cols].astype(F32)
        acc = u
        for back in range(1, w):
            acc = acc + st_ref[POOL_PAD - back, :, cols]
        cnt = float(min(PAST_LEN + 1, w))
        d = acc / cnt - u
        o_ref[:, cols] = _bdot(d, w_ref[gi]) * sc_ref[:, cols]


def _pool_sample(z, state_t, pool_w, pool_scale):
    rows = z.shape[1]
    ucol = U_OFF // POOL_WIDTH
    return pl.pallas_call(
        _pool_sample_kernel,
        grid=(1,),
        in_specs=[pl.BlockSpec((None, rows, POOL_WIDTH), lambda i: (0, 0, ucol)),
                  pl.BlockSpec((POOL_PAD, rows, POOL_WIDTH), lambda i: (0, 0, 0)),
                  pl.BlockSpec((len(POOL_WINDOWS), POOL_GROUP_WIDTH, POOL_GROUP_WIDTH), lambda i: (0, 0, 0)),
                  pl.BlockSpec((1, POOL_WIDTH), lambda i: (0, 0))],
        out_specs=pl.BlockSpec((None, rows, POOL_WIDTH), lambda i: (0, 0, 0)),
        out_shape=jax.ShapeDtypeStruct((1, rows, POOL_WIDTH), F32),
        compiler_params=_cparams("arbitrary"),
        name="pool_sample",
    )(z, state_t, pool_w, pool_scale)


PREP_HALO = 16


def _rwkv_prep_kernel(*refs, halo, has_vres):
    it = iter(refs)
    pr_ref, prev_ref = next(it), next(it)
    vf_ref = next(it) if has_vres else None
    mu_ref, w0_ref, a0_ref, kk_ref, ka_ref, rk_ref = (next(it) for _ in range(6))
    w2_ref, a2_ref, g2_ref, ones_ref = (next(it) for _ in range(4))
    if has_vres:
        vw1_ref, vw2_ref, vb_ref = next(it), next(it), next(it)
    r_o, w_o, k_o, v_o, a_o, b_o, g_o, bonus_o = (next(it) for _ in range(8))
    ext_ref = next(it) if halo else None

    pr = pr_ref[...].astype(F32)
    tm = pr.shape[0]
    if halo:
        i = pl.program_id(1)
        ext_ref[0:PREP_HALO, :] = jnp.where(i > 0, prev_ref[...].astype(F32), 0.0)
        ext_ref[PREP_HALO:PREP_HALO + tm, :] = pr
        prev = ext_ref[PREP_HALO - 1:PREP_HALO - 1 + tm, :]
    else:
        prev = prev_ref[...]
    xm = pr + (prev - pr) * mu_ref[...]
    wd = RWKV_WIDTH
    r, k, v, lora_in = xm[:, 0:wd], xm[:, wd:2 * wd], xm[:, 2 * wd:3 * wd], xm[:, 3 * wd:4 * wd]
    ones_bd = ones_ref[...]
    lw = _bdot(jnp.tanh(lora_in), w2_ref[...])
    la = _bdot(lora_in, a2_ref[...])
    g = _bdot(_sigmoid(lora_in), g2_ref[...])
    y = -(w0_ref[...] + lw)
    softplus = jnp.maximum(y, 0.0) + jnp.log(1.0 + jnp.exp(-jnp.abs(y)))
    decay = jnp.exp(-jnp.exp(-softplus - 0.5))
    a = _sigmoid(a0_ref[...] + la)
    if has_vres:
        t2 = _bdot(_bdot(v, vw1_ref[...]), vw2_ref[...])
        v = v + (vf_ref[...] - v) * _sigmoid(vb_ref[...] + t2)
    kk = k * kk_ref[...]
    kk = kk / jnp.maximum(jnp.sqrt(_seg_sum(kk * kk, ones_bd)), 1e-12)
    k = k * (1.0 + (a - 1.0) * ka_ref[...])
    r_o[...] = r
    w_o[...] = decay
    k_o[...] = k
    v_o[...] = v
    a_o[...] = -kk
    b_o[...] = kk * a
    g_o[...] = g
    bonus_o[...] = _seg_sum(r * k * rk_ref[...], ones_bd) * v


def _rwkv_prep(z, prev, v_first, rp, tm):
    groups, t, _ = z.shape
    halo = prev is None
    has_vres = v_first is not None
    prcol = PR_OFF // RWKV_PROJ_PAD
    wd = RWKV_WIDTH
    tok = lambda width: pl.BlockSpec((None, tm, width), lambda g, i: (g, i, 0))
    vec = lambda width: pl.BlockSpec((1, width), lambda g, i: (0, 0))
    mat = lambda a, b: pl.BlockSpec((a, b), lambda g, i: (0, 0))
    args = [z]
    in_specs = [pl.BlockSpec((None, tm, RWKV_PROJ_PAD), lambda g, i: (g, i, prcol))]
    if halo:
        per = tm // PREP_HALO
        args.append(z)
        in_specs.append(pl.BlockSpec((None, PREP_HALO, RWKV_PROJ_PAD),
                                     lambda g, i: (g, jnp.maximum(i * per - 1, 0), prcol)))
    else:
        args.append(prev)
        in_specs.append(tok(RWKV_PROJ_PAD))
    if has_vres:
        args.append(v_first)
        in_specs.append(tok(wd))
    args += [rp["mu"], rp["w0"], rp["a0"], rp["k_k"], rp["k_a"], rp["r_k"],
             rp["w2"], rp["a2"], rp["g2"], rp["ones_bd"]]
    in_specs += [vec(RWKV_PROJ_PAD)] + [vec(wd)] * 5 + [mat(wd, wd)] * 4
    if has_vres:
        args += [rp["vw1"], rp["vw2"], rp["vb"]]
        in_specs += [mat(wd, LANES), mat(LANES, wd), vec(wd)]
    return pl.pallas_call(
        functools.partial(_rwkv_prep_kernel, halo=halo, has_vres=has_vres),
        grid=(groups, t // tm),
        in_specs=in_specs,
        out_specs=[tok(wd)] * 8,
        out_shape=[jax.ShapeDtypeStruct((groups, t, wd), F32)] * 8,
        scratch_shapes=[pltpu.VMEM((tm + PREP_HALO, RWKV_PROJ_PAD), F32)] if halo else [],
        compiler_params=_cparams("parallel", "parallel"),
        name="rwkv_prep",
    )(*args)


SCAN_STEPS_PER_TRIP = 64


def _wkv_scan_kernel(r_ref, w_ref, k_ref, v_ref, a_ref, b_ref, y_ref, s_ref):
    nb, tc, _ = r_ref.shape
    pairs = RWKV_HEADS // 2
    n = RWKV_HEAD_DIM

    @pl.when(pl.program_id(0) == 0)
    def _():
        s_ref[...] = jnp.zeros(s_ref.shape, F32)

    row = lax.broadcasted_iota(jnp.int32, (n, 2 * n), 0)
    lane = lax.broadcasted_iota(jnp.int32, (n, 2 * n), 1)
    diag = (lane % n) == row
    chains = [(bi, slice(p * 2 * n, (p + 1) * 2 * n)) for bi in range(nb) for p in range(pairs)]
    nc = len(chains)
    seg_r = lax.broadcasted_iota(jnp.int32, (2 * n, 2 * n), 0) // n
    seg_c = lax.broadcasted_iota(jnp.int32, (2 * n, 2 * n), 1) // n
    ones_bd = (seg_r == seg_c).astype(BF16)
    stack = lambda xs: jnp.concatenate(xs, axis=0)
    unstack = lambda x: [x[c * n:(c + 1) * n] for c in range(nc)]

    def seg_bcast(x):
        return jnp.dot(x.astype(BF16), ones_bd, preferred_element_type=F32)

    span = SCAN_STEPS_PER_TRIP

    def steps(trip, carry):
        base = pl.multiple_of(trip * span, span)
        rows = pl.ds(base, span)
        tiles = [[ref[bi, rows, sl] for ref in (r_ref, w_ref, k_ref, v_ref, a_ref, b_ref)] for bi, sl in chains]
        s = [s_ref[c] for c in range(nc)]
        ys = [[] for _ in chains]

        def y_rows(ycol):
            for c in range(nc):
                ys[c].append(jnp.sum(jnp.where(diag, ycol[c * n:(c + 1) * n], 0.0), axis=0, keepdims=True))

        vcol = seg_bcast(stack([jnp.where(diag, t[3][slice(u, u + 1)], 0.0) for u in range(span) for t in tiles]))
        sr = None
        for u in range(span):
            one = slice(u, u + 1)
            red = seg_bcast(stack([s[c] * tiles[c][4][one] for c in range(nc)] + (sr or [])))
            if sr:
                y_rows(red[nc * n:])
            for c, (r8, w8, k8, v8, a8, b8) in enumerate(tiles):
                s[c] = (s[c] * w8[one] + red[c * n:(c + 1) * n] * b8[one]
                        + vcol[(u * nc + c) * n:(u * nc + c + 1) * n] * k8[one])
            sr = [s[c] * tiles[c][0][one] for c in range(nc)]
        y_rows(seg_bcast(stack(sr)))
        for c, (bi, sl) in enumerate(chains):
            s_ref[c] = s[c]
            y_ref[bi, rows, sl] = jnp.concatenate(ys[c], axis=0)
        return carry

    lax.fori_loop(0, tc // span, steps, 0)


def _wkv_scan(r, w, k, v, a, b, tc):
    nb, t, wd = r.shape
    pairs = RWKV_HEADS // 2
    tok = pl.BlockSpec((nb, tc, wd), lambda i: (0, i, 0))
    return pl.pallas_call(
        _wkv_scan_kernel,
        grid=(t // tc,),
        in_specs=[tok] * 6,
        out_specs=[tok, pl.BlockSpec((nb * pairs, RWKV_HEAD_DIM, 2 * RWKV_HEAD_DIM), lambda i: (0, 0, 0))],
        out_shape=[jax.ShapeDtypeStruct((nb, t, wd), F32),
                   jax.ShapeDtypeStruct((nb * pairs, RWKV_HEAD_DIM, 2 * RWKV_HEAD_DIM), F32)],
        compiler_params=_cparams("arbitrary"),
        name="wkv_scan",
    )(r, w, k, v, a, b)


def _wkv_step_kernel(x_ref, s_ref, y_ref, so_ref):
    n = RWKV_HEAD_DIM
    r, w, k, v, a, b = (x_ref[q] for q in range(6))
    for i0 in range(0, n, SUBLANES):
        rows = range(i0, i0 + SUBLANES)
        s = [s_ref[i] for i in rows]
        sa = [jnp.sum(si * a, axis=0, keepdims=True) for si in s]
        s = [si * w + sai * b + v[i:i + 1] * k for si, sai, i in zip(s, sa, rows)]
        for si, i in zip(s, rows):
            so_ref[i] = si
        y_ref[i0:i0 + SUBLANES, :] = jnp.concatenate([jnp.sum(si * r, axis=0, keepdims=True) for si in s], axis=0)


def _wkv_step(x, state_t, l):
    _, heads, n, nb = x.shape
    return pl.pallas_call(
        _wkv_step_kernel,
        grid=(heads,),
        in_specs=[pl.BlockSpec((6, None, n, nb), lambda h: (0, h, 0, 0)),
                  pl.BlockSpec((None, None, n, n, nb), lambda h: (l, h, 0, 0, 0))],
        out_specs=[pl.BlockSpec((None, n, nb), lambda h: (h, 0, 0)),
                   pl.BlockSpec((None, n, n, nb), lambda h: (h, 0, 0, 0))],
        out_shape=[jax.ShapeDtypeStruct((heads, n, nb), F32), jax.ShapeDtypeStruct((heads, n, n, nb), F32)],
        compiler_params=_cparams("parallel"),
        name="wkv_step",
    )(x, state_t)


def _rwkv_post_kernel(y_ref, bonus_ref, g_ref, lg_ref, lb_ref, ones_ref, o_ref):
    y = y_ref[...]
    ones_bd = ones_ref[...]
    inv = 1.0 / RWKV_HEAD_DIM
    d = y - _seg_sum(y, ones_bd) * inv
    var = _seg_sum(d * d, ones_bd) * inv
    yn = d * lax.rsqrt(var + LNX_EPS) * lg_ref[...] + lb_ref[...]
    o_ref[...] = ((yn + bonus_ref[...]) * g_ref[...]).astype(o_ref.dtype)


def _rwkv_post(y, bonus, g, rp, tm):
    groups, t, wd = y.shape
    tok = pl.BlockSpec((None, tm, wd), lambda gi, i: (gi, i, 0))
    vec = pl.BlockSpec((1, wd), lambda gi, i: (0, 0))
    return pl.pallas_call(
        _rwkv_post_kernel,
        grid=(groups, t // tm),
        in_specs=[tok, tok, tok, vec, vec, pl.BlockSpec((wd, wd), lambda gi, i: (0, 0))],
        out_specs=tok,
        out_shape=jax.ShapeDtypeStruct((groups, t, wd), BF16),
        compiler_params=_cparams("parallel", "parallel"),
        name="rwkv_post",
    )(y, bonus, g, rp["lnx_g"], rp["lnx_b"], rp["ones_bd"])


MERGE_TN = 512


def _merge_kernel(oa_ref, zp_ref, yr_ref, ga_ref, gp_ref, gr_ref, wa_ref, wp_ref, wr_ref, o_ref):
    gate = lambda ref: _sigmoid(ref[...].astype(F32))
    merged = (gate(ga_ref) * _bdot(oa_ref[...], wa_ref[...])
              + gate(gp_ref) * _bdot(zp_ref[...], wp_ref[...])
              + gate(gr_ref) * _bdot(yr_ref[...], wr_ref[...]))
    o_ref[...] = merged.astype(o_ref.dtype)


def _merge(o_attn, z_pool, y_rwkv, z, w_attn_out, w_pool_out, w_rwkv_out, l, tm):
    groups, t, _ = z.shape
    tn = MERGE_TN
    gate = lambda br: pl.BlockSpec((None, tm, tn), lambda g, i, j: (g, i, (GL_OFF + br * D_MODEL) // tn + j))
    tok = lambda width: pl.BlockSpec((None, tm, width), lambda g, i, j: (g, i, 0))
    wt = lambda rows: pl.BlockSpec((None, rows, tn), lambda g, i, j: (l, 0, j))
    return pl.pallas_call(
        _merge_kernel,
        grid=(groups, t // tm, D_MODEL // tn),
        in_specs=[tok(ATTN_WIDTH), tok(POOL_WIDTH), tok(RWKV_WIDTH), gate(0), gate(1), gate(2),
                  wt(ATTN_WIDTH), wt(POOL_WIDTH), wt(RWKV_WIDTH)],
        out_specs=pl.BlockSpec((None, tm, tn), lambda g, i, j: (g, i, j)),
        out_shape=jax.ShapeDtypeStruct((groups, t, D_MODEL), BF16),
        compiler_params=_cparams("parallel", "parallel", "arbitrary"),
        name="merge",
    )(o_attn, z_pool, y_rwkv, z, z, z, w_attn_out, w_pool_out, w_rwkv_out)


def _proj_residual_kernel(a_ref, w_ref, x_ref, gt_ref, o_ref):
    o_ref[...] = x_ref[...] + (1.0 + gt_ref[...]) * _bdot(a_ref[...], w_ref[...])


def _proj_residual(a, w, l, x, mod, chunk, tm, tn):
    groups, t, kdim = a.shape
    per_row = mod.shape[1] != 1
    gate = pl.BlockSpec((None, tm if per_row else 1, tn),
                        lambda g, i, j: (g, i if per_row else 0, chunk * (D_MODEL // tn) + j))
    return pl.pallas_call(
        _proj_residual_kernel,
        grid=(groups, t // tm, D_MODEL // tn),
        in_specs=[pl.BlockSpec((None, tm, kdim), lambda g, i, j: (g, i, 0)),
                  pl.BlockSpec((None, kdim, tn), lambda g, i, j: (l, 0, j)),
                  pl.BlockSpec((None, tm, tn), lambda g, i, j: (g, i, j)),
                  gate],
        out_specs=pl.BlockSpec((None, tm, tn), lambda g, i, j: (g, i, j)),
        out_shape=jax.ShapeDtypeStruct((groups, t, D_MODEL), F32),
        compiler_params=_cparams("parallel", "parallel", "arbitrary"),
        name="proj_residual",
    )(a, w, x, mod)


FFN_TN = 256


def _ffn_up_kernel(x_ref, g_ref, sh_ref, sc_ref, wg_ref, wu_ref, o_ref, h_ref):
    @pl.when(pl.program_id(2) == 0)
    def _():
        h_ref[...] = _mod_norm(x_ref[...], g_ref[...], sh_ref[...], sc_ref[...]).astype(BF16)

    h = h_ref[...]
    gate = jnp.dot(h, wg_ref[...].astype(BF16), preferred_element_type=F32)
    up = jnp.dot(h, wu_ref[...].astype(BF16), preferred_element_type=F32)
    o_ref[...] = (gate * _sigmoid(gate) * up).astype(o_ref.dtype)


def _ffn_up(x, g2, mod, w_gate, w_up, l, tm):
    groups, t, _ = x.shape
    tn = FFN_TN
    wt = pl.BlockSpec((None, D_MODEL, tn), lambda g, i, j: (l, 0, j))
    return pl.pallas_call(
        _ffn_up_kernel,
        grid=(groups, t // tm, D_FF // tn),
        in_specs=[pl.BlockSpec((None, tm, D_MODEL), lambda g, i, j: (g, i, 0), pipeline_mode=pl.Buffered(1)),
                  pl.BlockSpec((1, D_MODEL), lambda g, i, j: (0, 0)),
                  _mod_spec(mod, tm, 3), _mod_spec(mod, tm, 4), wt, wt],
        out_specs=pl.BlockSpec((None, tm, tn), lambda g, i, j: (g, i, j)),
        out_shape=jax.ShapeDtypeStruct((groups, t, D_FF), BF16),
        scratch_shapes=[pltpu.VMEM((tm, D_MODEL), BF16)],
        compiler_params=_cparams("parallel", "parallel", "arbitrary"),
        name="ffn_up",
    )(x, g2, mod, mod, w_gate, w_up)


def _final_norm_kernel(x_ref, g_ref, o_ref):
    x = x_ref[...]
    ms = jnp.mean(x * x, axis=-1, keepdims=True)
    o_ref[...] = x * lax.rsqrt(ms + RMS_EPS) * g_ref[...]


def _final_norm(x, g, tm):
    groups, t, _ = x.shape
    tok = pl.BlockSpec((None, tm, D_MODEL), lambda gi, i: (gi, i, 0))
    return pl.pallas_call(
        _final_norm_kernel,
        grid=(groups, t // tm),
        in_specs=[tok, pl.BlockSpec((1, D_MODEL), lambda gi, i: (0, 0))],
        out_specs=tok,
        out_shape=jax.ShapeDtypeStruct(x.shape, F32),
        compiler_params=_cparams("parallel", "parallel"),
        name="final_norm",
    )(x, g)


def _pad_cols(a, width):
    return jnp.pad(a, [(0, 0)] * (a.ndim - 1) + [(0, width - a.shape[-1])])


def _rwkv_params(l, rwkv_mu, rwkv_w0, rwkv_w2, rwkv_a0, rwkv_a2, rwkv_g2, rwkv_k_k, rwkv_k_a, rwkv_r_k,
                 rwkv_lnx_g, rwkv_lnx_b, vres_w1, vres_w2, vres_b):
    wd = RWKV_WIDTH
    row = lambda a: a.reshape(1, -1)
    lora_rows = lambda w, off: jnp.zeros((wd, wd), F32).at[off:off + w.shape[0]].set(w).astype(BF16)
    seg = np.arange(wd) // RWKV_HEAD_DIM
    rp = dict(
        mu=_pad_cols(row(rwkv_mu[l]), RWKV_PROJ_PAD), w0=row(rwkv_w0[l]), a0=row(rwkv_a0[l]),
        k_k=row(rwkv_k_k[l]), k_a=row(rwkv_k_a[l]), r_k=row(rwkv_r_k[l]),
        w2=lora_rows(rwkv_w2[l], 0), a2=lora_rows(rwkv_a2[l], DECAY_LORA),
        g2=lora_rows(rwkv_g2[l], DECAY_LORA + ICLR_LORA),
        ones_bd=jnp.asarray(seg[:, None] == seg[None, :], BF16),
        lnx_g=row(rwkv_lnx_g[l]), lnx_b=row(rwkv_lnx_b[l]))
    if l > 0:
        rp.update(vw1=_pad_cols(vres_w1[l - 1], LANES).astype(BF16),
                  vw2=jnp.pad(vres_w2[l - 1], ((0, LANES - VRES_LORA), (0, 0))).astype(BF16),
                  vb=row(vres_b[l - 1]))
    return rp


def _layer_common(x, mod, z, o_attn, z_pool, y, bonus, g, lw, l, rp, tm):
    y_rwkv = _rwkv_post(y, bonus, g, rp, _row_tile(x.shape[1], 512))
    merged = _merge(o_attn, z_pool, y_rwkv, z, lw["w_attn_out"], lw["w_pool_out"], lw["w_rwkv_out"], l, tm)
    x = _proj_residual(merged, lw["w_o"], l, x, mod, 2, tm, 512)
    hidden = _ffn_up(x, lw["norm2_g"][l].reshape(1, -1), mod, lw["w_ffn_gate"], lw["w_ffn_up"], l, tm)
    return _proj_residual(hidden, lw["w_ffn_down"], l, x, mod, 5, _row_tile(x.shape[1], NORM_ROWS), 256)


def kernel(x_prompt, x_sample, c_prompt, c_sample, cache_k_win, cache_v_win, state_pool, state_shift, state_wkv, ada_w, ada_b, norm1_g, norm2_g, w_in, attn_sink, w_attn_out, pool_w, pool_scale, w_pool_out, rwkv_mu, rwkv_w0, rwkv_w2, rwkv_a0, rwkv_a2, rwkv_g2, rwkv_k_k, rwkv_k_a, rwkv_r_k, rwkv_lnx_g, rwkv_lnx_b, w_rwkv_out, vres_w1, vres_w2, vres_b, w_o, w_ffn_gate, w_ffn_up, w_ffn_down, final_norm_g):
    depth = ada_w.shape[0]
    nb, t, _ = x_prompt.shape
    ns = x_sample.shape[0]
    assert x_sample.shape[1] == 1 and t % WINDOW == 0
    win = cache_k_win.shape[2]
    assert win <= WINDOW
    heads, n = RWKV_HEADS, RWKV_HEAD_DIM

    pad_rows = -nb % SUBLANES
    c_all = jnp.concatenate([c_prompt, jnp.zeros((pad_rows, D_MODEL), F32), c_sample], axis=0)
    mod_all = _ada(c_all, ada_w, ada_b)

    cos_p, sin_p = _rope_tables(jnp.arange(t))
    cos_s, sin_s = _rope_tables(jnp.full((ns,), PAST_LEN))

    xp = x_prompt
    xs = x_sample.reshape(1, ns, D_MODEL)
    tm_p = _row_tile(t, MATMUL_ROWS)
    tm_s = _row_tile(ns, MATMUL_ROWS)
    vf_p = vf_s = None
    st_p, st_s = [], []
    lw = dict(norm2_g=norm2_g, w_attn_out=w_attn_out, w_pool_out=w_pool_out, w_rwkv_out=w_rwkv_out, w_o=w_o,
              w_ffn_gate=w_ffn_gate, w_ffn_up=w_ffn_up, w_ffn_down=w_ffn_down)
    w_in_t = w_in.transpose(0, 2, 1)
    to_feature_major = lambda c: c.transpose(0, 1, 3, 4, 2).reshape(depth, ns, KV_WIDTH, win)
    from_feature_major = lambda c: c.reshape(ns, N_KV_HEADS, HEAD_DIM, win).transpose(0, 3, 1, 2)
    cache_kt, cache_vt = to_feature_major(cache_k_win), to_feature_major(cache_v_win)
    state_t = state_wkv.transpose(0, 2, 3, 4, 1)
    for l in range(depth):
        rp = _rwkv_params(l, rwkv_mu, rwkv_w0, rwkv_w2, rwkv_a0, rwkv_a2, rwkv_g2, rwkv_k_k, rwkv_k_a, rwkv_r_k,
                          rwkv_lnx_g, rwkv_lnx_b, vres_w1, vres_w2, vres_b)
        g1 = norm1_g[l].reshape(1, -1)
        mod_p = mod_all[l, :nb].reshape(nb, 1, -1)
        mod_s = mod_all[l, nb + pad_rows:].reshape(1, ns, -1)

        z = _in_proj(xp, g1, mod_p, w_in_t, l, cos_p, sin_p, tm_p)
        o_attn = _attn_prompt(z, attn_sink[l])
        z_pool = _pool_prompt(z, pool_w[l], pool_scale[l].reshape(1, -1), _row_tile(t, 512))
        r, w, k, v, a, b, g, bonus = _rwkv_prep(z, None, vf_p, rp, _row_tile(t, 256))
        if l == 0:
            vf_p = v
        y, s_pair = _wkv_scan(r, w, k, v, a, b, _row_tile(t, 256))
        wkv_new = s_pair.reshape(nb, heads // 2, n, 2, n).transpose(0, 1, 3, 2, 4).reshape(nb, heads, n, n)
        tail = z[:, t - WINDOW:, :GL_OFF].astype(F32)
        st_p.append((tail[:, :, K_OFF:V_OFF].reshape(nb, WINDOW, N_KV_HEADS, HEAD_DIM),
                     tail[:, :, V_OFF:U_OFF].reshape(nb, WINDOW, N_KV_HEADS, HEAD_DIM),
                     tail[:, WINDOW - POOL_PAD:, U_OFF:PR_OFF],
                     tail[:, WINDOW - 1, PR_OFF:PR_OFF + RWKV_PROJ],
                     wkv_new))
        xp = _layer_common(xp, mod_p, z, o_attn, z_pool, y, bonus, g, lw, l, rp, tm_p)

        z = _in_proj(xs, g1, mod_s, w_in_t, l, cos_s, sin_s, tm_s)
        zs = z[0, :, :GL_OFF].astype(F32)
        k_new, v_new = zs[:, K_OFF:V_OFF], zs[:, V_OFF:U_OFF]
        o3, kt_new, vt_new = _attn_sample(zs[:, :ATTN_WIDTH].reshape(ns, N_Q_HEADS, HEAD_DIM), k_new, v_new,
                                          cache_kt, cache_vt, l, attn_sink[l])
        o_attn = o3.reshape(1, ns, ATTN_WIDTH).astype(BF16)
        z_pool = _pool_sample(z, state_pool[l].transpose(1, 0, 2), pool_w[l], pool_scale[l].reshape(1, -1))
        prev = _pad_cols(state_shift[l], RWKV_PROJ_PAD).reshape(1, ns, RWKV_PROJ_PAD)
        r, w, k, v, a, b, g, bonus = _rwkv_prep(z, prev, vf_s, rp, tm_s)
        if l == 0:
            vf_s = v
        step_in = jnp.stack([r, w, k, v, a, b]).reshape(6, ns, heads, n).transpose(0, 2, 3, 1)
        y_t, wkv_t = _wkv_step(step_in, state_t, l)
        y = y_t.transpose(2, 0, 1).reshape(1, ns, RWKV_WIDTH)
        u_new = zs[:, None, U_OFF:PR_OFF]
        st_s.append((from_feature_major(kt_new), from_feature_major(vt_new),
                     jnp.concatenate([state_pool[l][:, 1:], u_new], axis=1),
                     zs[:, PR_OFF:PR_OFF + RWKV_PROJ],
                     wkv_t.transpose(3, 0, 1, 2)))
        xs = _layer_common(xs, mod_s, z, o_attn, z_pool, y, bonus, g, lw, l, rp, tm_s)

    g_fin = final_norm_g.reshape(1, -1)
    y_prompt = _final_norm(xp, g_fin, _row_tile(t, NORM_ROWS))
    y_sample = _final_norm(xs, g_fin, _row_tile(ns, NORM_ROWS)).reshape(ns, 1, D_MODEL)
    stack = lambda states, i: jnp.stack([s[i] for s in states])
    return (y_prompt, y_sample) + tuple(stack(st_p, i) for i in range(5)) + tuple(stack(st_s, i) for i in range(5))
```

```python
import functools

import numpy as np
import jax
import jax.numpy as jnp
from jax import lax
from jax.experimental import pallas as pl
from jax.experimental.pallas import tpu as pltpu

F32 = jnp.float32
BF16 = jnp.bfloat16

D_MODEL = 2048
HEAD_DIM = 64
N_Q_HEADS = 16
N_KV_HEADS = 4
GROUP = N_Q_HEADS // N_KV_HEADS
ATTN_WIDTH = N_Q_HEADS * HEAD_DIM
KV_WIDTH = N_KV_HEADS * HEAD_DIM
WINDOW = 128
ROPE_THETA = 10000.0
POOL_WIDTH = 512
POOL_WINDOWS = (2, 4, 8, 16)
POOL_GROUP_WIDTH = 128
POOL_PAD = 15
RWKV_HEAD_DIM = 64
RWKV_WIDTH = 512
RWKV_HEADS = 8
DECAY_LORA = 96
ICLR_LORA = 96
GATE_LORA = 256
VRES_LORA = 32
RWKV_PROJ = 3 * RWKV_WIDTH + DECAY_LORA + ICLR_LORA + GATE_LORA
RWKV_PROJ_PAD = 2048
D_FF = 5632
PAST_LEN = 8192
RMS_EPS = 1e-6
LNX_EPS = 64e-5
NEG_INF = -1e30

Q_OFF = 0
K_OFF = ATTN_WIDTH
V_OFF = K_OFF + KV_WIDTH
U_OFF = V_OFF + KV_WIDTH
PR_OFF = U_OFF + POOL_WIDTH
GL_OFF = PR_OFF + RWKV_PROJ_PAD
IN_PAD = GL_OFF + 3 * D_MODEL
ROPE_END = V_OFF

LANES = 128
SUBLANES = 8
VMEM_LIMIT = 56 * 2**20

MATMUL_ROWS = 2048
NORM_ROWS = 1024


def _cparams(*sem):
    return pltpu.CompilerParams(dimension_semantics=sem, vmem_limit_bytes=VMEM_LIMIT)


def _row_tile(t, pref):
    tm = min(t, pref)
    while t % tm:
        tm -= SUBLANES
    return tm


def _sigmoid(x):
    return 1.0 / (1.0 + jnp.exp(-x))


def _mod_norm(x, g, sh, sc):
    ms = jnp.mean(x * x, axis=-1, keepdims=True)
    return (x * lax.rsqrt(ms + RMS_EPS) * g) * (1.0 + sc) + sh


NORM_CHUNKS = 4


def _row_chunks(tm, n):
    size = tm // n
    return [slice(c * size, (c + 1) * size) for c in range(n)]


def _rows_of(mod_ref, rows):
    return mod_ref[...] if mod_ref.shape[0] == 1 else mod_ref[rows, :]


def _bdot(a, b):
    return jnp.dot(a.astype(BF16), b.astype(BF16), preferred_element_type=F32)


def _seg_sum(x, ones_bd):
    hi = x.astype(BF16)
    lo = (x - hi.astype(F32)).astype(BF16)
    return (jnp.dot(hi, ones_bd, preferred_element_type=F32)
            + jnp.dot(lo, ones_bd, preferred_element_type=F32))


def _mod_spec(mod, tm, chunk):
    per_row = mod.shape[1] != 1
    rows = tm if per_row else 1
    return pl.BlockSpec((None, rows, D_MODEL), lambda g, i, j=0: (g, i if per_row else 0, chunk))


def _ada_kernel(c_ref, w_ref, b_ref, o_ref):
    c = c_ref[...]
    o_ref[...] = _bdot(c * _sigmoid(c), w_ref[...]) + b_ref[...]


def _ada(c_all, ada_w, ada_b):
    depth, _, width = ada_w.shape
    rows = c_all.shape[0]
    tn = 1024
    return pl.pallas_call(
        _ada_kernel,
        grid=(depth, width // tn),
        in_specs=[pl.BlockSpec((rows, D_MODEL), lambda l, j: (0, 0)),
                  pl.BlockSpec((None, D_MODEL, tn), lambda l, j: (l, 0, j)),
                  pl.BlockSpec((None, 1, tn), lambda l, j: (l, 0, j))],
        out_specs=pl.BlockSpec((None, rows, tn), lambda l, j: (l, 0, j)),
        out_shape=jax.ShapeDtypeStruct((depth, rows, width), F32),
        compiler_params=_cparams("parallel", "parallel"),
        name="ada",
    )(c_all, ada_w, ada_b.reshape(depth, 1, width))


IN_TN = 512
IN_MAIN_TILES = GL_OFF // IN_TN


def _in_proj_kernel(x_ref, g_ref, sh_ref, sc_ref, cos_ref, sin_ref, w_ref, o_ref, h_ref):
    j = pl.program_id(2)
    tm = x_ref.shape[0]

    def project(rows):
        return lax.dot_general(h_ref[rows, :], w_ref[0].astype(BF16), (((1,), (1,)), ((), ())),
                               preferred_element_type=F32)

    def rope_tile(rows):
        acc = project(rows)
        reps = IN_TN // LANES
        cos = jnp.concatenate([cos_ref[rows, :]] * reps, axis=1)
        sin = jnp.concatenate([sin_ref[rows, :]] * reps, axis=1)
        lane = lax.broadcasted_iota(jnp.int32, acc.shape, 1)
        first_half = (lane % HEAD_DIM) < (HEAD_DIM // 2)
        rot = jnp.where(first_half, pltpu.roll(acc, IN_TN - HEAD_DIM // 2, 1),
                        pltpu.roll(acc, HEAD_DIM // 2, 1))
        roped = acc * cos + rot * sin
        o_ref[rows, :] = jnp.where(j * IN_TN + lane < ROPE_END, roped, acc).astype(o_ref.dtype)

    @pl.when(j == 0)
    def _():
        for rows in _row_chunks(tm, NORM_CHUNKS):
            h_ref[rows, :] = _mod_norm(x_ref[rows, :], g_ref[...], _rows_of(sh_ref, rows),
                                       _rows_of(sc_ref, rows)).astype(BF16)
            rope_tile(rows)

    @pl.when((j > 0) & (j * IN_TN < ROPE_END))
    def _():
        rope_tile(slice(None))

    @pl.when(j * IN_TN >= ROPE_END)
    def _():
        o_ref[...] = project(slice(None)).astype(o_ref.dtype)


def _in_proj(x, g1, mod, w_in_t, l, cos, sin, tm):
    groups, t, _ = x.shape
    gate_shift = RWKV_PROJ_PAD - RWKV_PROJ

    def weight_rows(g, i, j):
        start = jnp.where(j < IN_MAIN_TILES, j * IN_TN, j * IN_TN - gate_shift)
        return (l, pl.multiple_of(start, SUBLANES), 0)

    return pl.pallas_call(
        _in_proj_kernel,
        grid=(groups, t // tm, IN_PAD // IN_TN),
        in_specs=[pl.BlockSpec((None, tm, D_MODEL), lambda g, i, j: (g, i, 0), pipeline_mode=pl.Buffered(1)),
                  pl.BlockSpec((1, D_MODEL), lambda g, i, j: (0, 0)),
                  _mod_spec(mod, tm, 0), _mod_spec(mod, tm, 1),
                  pl.BlockSpec((tm, LANES), lambda g, i, j: (i, 0), pipeline_mode=pl.Buffered(1)),
                  pl.BlockSpec((tm, LANES), lambda g, i, j: (i, 0), pipeline_mode=pl.Buffered(1)),
                  pl.BlockSpec((pl.Element(1), pl.Element(IN_TN), pl.Element(D_MODEL)), weight_rows)],
        out_specs=pl.BlockSpec((None, tm, IN_TN), lambda g, i, j: (g, i, j)),
        out_shape=jax.ShapeDtypeStruct((groups, t, IN_PAD), BF16),
        scratch_shapes=[pltpu.VMEM((tm, D_MODEL), BF16)],
        compiler_params=_cparams("parallel", "parallel", "arbitrary"),
        name="in_proj",
    )(x, g1, mod, mod, cos, sin, w_in_t)


def _rope_tables(pos):
    half = HEAD_DIM // 2
    inv = ROPE_THETA ** (-jnp.arange(half, dtype=F32) * 2.0 / HEAD_DIM)
    ang = pos.astype(F32)[:, None] * inv[None, :]
    cos, sin = jnp.cos(ang), jnp.sin(ang)
    reps = LANES // HEAD_DIM
    cos_t = jnp.concatenate([cos, cos] * reps, axis=1)
    sin_t = jnp.concatenate([-sin, sin] * reps, axis=1)
    return cos_t, sin_t


def _attn_prompt_kernel(sink_ref, q_ref, k_ref, v_ref, kp_ref, vp_ref, o_ref):
    n = pl.program_id(1)
    blk = q_ref.shape[0]
    q = q_ref[...]
    k2 = jnp.concatenate([kp_ref[...], k_ref[...]], axis=0).astype(BF16)
    v2 = jnp.concatenate([vp_ref[...], v_ref[...]], axis=0).astype(BF16)
    qi = lax.broadcasted_iota(jnp.int32, (blk, 2 * blk), 0)
    kc = lax.broadcasted_iota(jnp.int32, (blk, 2 * blk), 1)
    rel = qi - kc + blk
    first_key = jnp.where(n > 0, 0, blk)
    mask = (rel >= 0) & (rel <= WINDOW) & (kc >= first_key)
    outs = []
    for hd in range(N_Q_HEADS):
        h = hd // GROUP
        kh = k2[:, h * HEAD_DIM:(h + 1) * HEAD_DIM]
        vh = v2[:, h * HEAD_DIM:(h + 1) * HEAD_DIM]
        qh = q[:, hd * HEAD_DIM:(hd + 1) * HEAD_DIM].astype(BF16)
        s = lax.dot_general(qh, kh, (((1,), (1,)), ((), ())), preferred_element_type=F32) * (HEAD_DIM ** -0.5)
        s = jnp.where(mask, s, NEG_INF)
        sk = sink_ref[hd]
        m = jnp.maximum(jnp.max(s, axis=1, keepdims=True), sk)
        p = jnp.exp(s - m)
        den = jnp.sum(p, axis=1, keepdims=True) + jnp.exp(sk - m)
        outs.append(jnp.dot(p.astype(BF16), vh, preferred_element_type=F32) / den)
    o_ref[...] = jnp.concatenate(outs, axis=1).astype(o_ref.dtype)


def _attn_prompt(z, sink):
    groups, t, _ = z.shape
    blk = WINDOW
    kcol, vcol = K_OFF // KV_WIDTH, V_OFF // KV_WIDTH
    prev = lambda col: (lambda g, n: (g, jnp.maximum(n - 1, 0), col))
    return pl.pallas_call(
        _attn_prompt_kernel,
        grid=(groups, t // blk),
        in_specs=[pl.BlockSpec(memory_space=pltpu.SMEM),
                  pl.BlockSpec((None, blk, ATTN_WIDTH), lambda g, n: (g, n, 0)),
                  pl.BlockSpec((None, blk, KV_WIDTH), lambda g, n: (g, n, kcol)),
                  pl.BlockSpec((None, blk, KV_WIDTH), lambda g, n: (g, n, vcol)),
                  pl.BlockSpec((None, blk, KV_WIDTH), prev(kcol)),
                  pl.BlockSpec((None, blk, KV_WIDTH), prev(vcol))],
        out_specs=pl.BlockSpec((None, blk, ATTN_WIDTH), lambda g, n: (g, n, 0)),
        out_shape=jax.ShapeDtypeStruct((groups, t, ATTN_WIDTH), BF16),
        compiler_params=_cparams("parallel", "parallel"),
        name="attn_prompt",
    )(sink, z, z, z, z, z)


def _attn_sample_kernel(sink_ref, q_ref, kn_ref, vn_ref, knc_ref, vnc_ref, kt_ref, vt_ref, o_ref, kto_ref, vto_ref):
    bb = q_ref.shape[0]
    win = kt_ref.shape[2]
    row = lax.broadcasted_iota(jnp.int32, (N_Q_HEADS, KV_WIDTH), 0)
    lane = lax.broadcasted_iota(jnp.int32, (N_Q_HEADS, KV_WIDTH), 1)
    own = (row // GROUP) == (lane // HEAD_DIM)
    newest = lax.broadcasted_iota(jnp.int32, (KV_WIDTH, win), 1) == win - 1
    sink = sink_ref[...]
    scale = HEAD_DIM ** -0.5

    rows = range(bb)
    qbd = [jnp.where(own, jnp.concatenate([q_ref[b]] * N_KV_HEADS, axis=1), 0.0) for b in rows]
    s = [jnp.dot(qbd[b].astype(BF16), kt_ref[b].astype(BF16), preferred_element_type=F32) * scale for b in rows]
    s_new = [jnp.sum(qbd[b] * kn_ref[b], axis=1, keepdims=True) * scale for b in rows]
    m = [jnp.maximum(jnp.maximum(jnp.max(s[b], axis=1, keepdims=True), s_new[b]), sink) for b in rows]
    p = [jnp.exp(s[b] - m[b]) for b in rows]
    p_new = [jnp.exp(s_new[b] - m[b]) for b in rows]
    den = [jnp.sum(p[b], axis=1, keepdims=True) + p_new[b] + jnp.exp(sink - m[b]) for b in rows]
    o = [lax.dot_general(p[b].astype(BF16), vt_ref[b].astype(BF16), (((1,), (1,)), ((), ())),
                         preferred_element_type=F32) + p_new[b] * vn_ref[b] for b in rows]
    for b in rows:
        ob = jnp.where(own, o[b], 0.0)
        o64 = ob[:, 0:HEAD_DIM]
        for h in range(1, N_KV_HEADS):
            o64 = o64 + ob[:, h * HEAD_DIM:(h + 1) * HEAD_DIM]
        o_ref[b] = o64 / den[b]
        kto_ref[b] = jnp.where(newest, knc_ref[b], pltpu.roll(kt_ref[b], win - 1, 1))
        vto_ref[b] = jnp.where(newest, vnc_ref[b], pltpu.roll(vt_ref[b], win - 1, 1))


def _attn_sample(q3, k_new, v_new, kt_all, vt_all, l, sink):
    _, nb, _, win = kt_all.shape
    assert win == LANES
    bb = _row_tile(nb, 8)
    row = pl.BlockSpec((bb, 1, KV_WIDTH), lambda i: (i, 0, 0))
    col = pl.BlockSpec((bb, KV_WIDTH, 1), lambda i: (i, 0, 0))
    cache_in = pl.BlockSpec((None, bb, KV_WIDTH, win), lambda i: (l, i, 0, 0))
    cache_out = pl.BlockSpec((bb, KV_WIDTH, win), lambda i: (i, 0, 0))
    heads = pl.BlockSpec((bb, N_Q_HEADS, HEAD_DIM), lambda i: (i, 0, 0))
    return pl.pallas_call(
        _attn_sample_kernel,
        grid=(nb // bb,),
        in_specs=[pl.BlockSpec((N_Q_HEADS, 1), lambda i: (0, 0)), heads, row, row, col, col, cache_in, cache_in],
        out_specs=[heads, cache_out, cache_out],
        out_shape=[jax.ShapeDtypeStruct((nb, N_Q_HEADS, HEAD_DIM), F32),
                   jax.ShapeDtypeStruct((nb, KV_WIDTH, win), F32),
                   jax.ShapeDtypeStruct((nb, KV_WIDTH, win), F32)],
        compiler_params=_cparams("parallel"),
        name="attn_sample",
    )(sink.reshape(N_Q_HEADS, 1), q3, k_new.reshape(nb, 1, KV_WIDTH), v_new.reshape(nb, 1, KV_WIDTH),
      k_new.reshape(nb, KV_WIDTH, 1), v_new.reshape(nb, KV_WIDTH, 1), kt_all, vt_all)


POOL_HALO = 16


def _pool_prompt_kernel(u_ref, halo_ref, w_ref, sc_ref, o_ref, ext_ref):
    i = pl.program_id(1)
    tm = u_ref.shape[0]
    ext_ref[0:POOL_HALO, :] = jnp.where(i > 0, halo_ref[...].astype(F32), 0.0)
    ext_ref[POOL_HALO:POOL_HALO + tm, :] = u_ref[...].astype(F32)
    pos = i * tm + lax.broadcasted_iota(jnp.int32, (tm, 1), 0)
    for gi, w in enumerate(POOL_WINDOWS):
        cols = slice(gi * POOL_GROUP_WIDTH, (gi + 1) * POOL_GROUP_WIDTH)
        u = ext_ref[POOL_HALO:POOL_HALO + tm, cols]
        acc = u
        for back in range(1, w):
            acc = acc + ext_ref[POOL_HALO - back:POOL_HALO - back + tm, cols]
        cnt = jnp.minimum(pos + 1, w).astype(F32)
        d = acc / cnt - u
        o_ref[:, cols] = _bdot(d, w_ref[gi]) * sc_ref[:, cols]


def _pool_prompt(z, pool_w, pool_scale, tm):
    groups, t, _ = z.shape
    ucol = U_OFF // POOL_WIDTH
    per = tm // POOL_HALO
    return pl.pallas_call(
        _pool_prompt_kernel,
        grid=(groups, t // tm),
        in_specs=[pl.BlockSpec((None, tm, POOL_WIDTH), lambda g, i: (g, i, ucol)),
                  pl.BlockSpec((None, POOL_HALO, POOL_WIDTH), lambda g, i: (g, jnp.maximum(i * per - 1, 0), ucol)),
                  pl.BlockSpec((len(POOL_WINDOWS), POOL_GROUP_WIDTH, POOL_GROUP_WIDTH), lambda g, i: (0, 0, 0)),
                  pl.BlockSpec((1, POOL_WIDTH), lambda g, i: (0, 0))],
        out_specs=pl.BlockSpec((None, tm, POOL_WIDTH), lambda g, i: (g, i, 0)),
        out_shape=jax.ShapeDtypeStruct((groups, t, POOL_WIDTH), F32),
        scratch_shapes=[pltpu.VMEM((tm + POOL_HALO, POOL_WIDTH), F32)],
        compiler_params=_cparams("parallel", "parallel"),
        name="pool_prompt",
    )(z, z, pool_w, pool_scale)


def _pool_sample_kernel(u_ref, st_ref, w_ref, sc_ref, o_ref):
    for gi, w in enumerate(POOL_WINDOWS):
        cols = slice(gi * POOL_GROUP_WIDTH, (gi + 1) * POOL_GROUP_WIDTH)
        u = u_ref[:, cols].astype(F32)
        acc = u
        for back in range(1, w):
            acc = acc + st_ref[POOL_PAD - back, :, cols]
        cnt = float(min(PAST_LEN + 1, w))
        d = acc / cnt - u
        o_ref[:, cols] = _bdot(d, w_ref[gi]) * sc_ref[:, cols]


def _pool_sample(z, state_t, pool_w, pool_scale):
    rows = z.shape[1]
    ucol = U_OFF // POOL_WIDTH
    return pl.pallas_call(
        _pool_sample_kernel,
        grid=(1,),
        in_specs=[pl.BlockSpec((None, rows, POOL_WIDTH), lambda i: (0, 0, ucol)),
                  pl.BlockSpec((POOL_PAD, rows, POOL_WIDTH), lambda i: (0, 0, 0)),
                  pl.BlockSpec((len(POOL_WINDOWS), POOL_GROUP_WIDTH, POOL_GROUP_WIDTH), lambda i: (0, 0, 0)),
                  pl.BlockSpec((1, POOL_WIDTH), lambda i: (0, 0))],
        out_specs=pl.BlockSpec((None, rows, POOL_WIDTH), lambda i: (0, 0, 0)),
        out_shape=jax.ShapeDtypeStruct((1, rows, POOL_WIDTH), F32),
        compiler_params=_cparams("arbitrary"),
        name="pool_sample",
    )(z, state_t, pool_w, pool_scale)


PREP_HALO = 16


def _rwkv_prep_kernel(*refs, halo, has_vres):
    it = iter(refs)
    pr_ref, prev_ref = next(it), next(it)
    vf_ref = next(it) if has_vres else None
    mu_ref, w0_ref, a0_ref, kk_ref, ka_ref, rk_ref = (next(it) for _ in range(6))
    w2_ref, a2_ref, g2_ref, ones_ref = (next(it) for _ in range(4))
    if has_vres:
        vw1_ref, vw2_ref, vb_ref = next(it), next(it), next(it)
    r_o, w_o, k_o, v_o, a_o, b_o, g_o, bonus_o = (next(it) for _ in range(8))
    ext_ref = next(it) if halo else None

    pr = pr_ref[...].astype(F32)
    tm = pr.shape[0]
    if halo:
        i = pl.program_id(1)
        ext_ref[0:PREP_HALO, :] = jnp.where(i > 0, prev_ref[...].astype(F32), 0.0)
        ext_ref[PREP_HALO:PREP_HALO + tm, :] = pr
        prev = ext_ref[PREP_HALO - 1:PREP_HALO - 1 + tm, :]
    else:
        prev = prev_ref[...]
    xm = pr + (prev - pr) * mu_ref[...]
    wd = RWKV_WIDTH
    r, k, v, lora_in = xm[:, 0:wd], xm[:, wd:2 * wd], xm[:, 2 * wd:3 * wd], xm[:, 3 * wd:4 * wd]
    ones_bd = ones_ref[...]
    lw = _bdot(jnp.tanh(lora_in), w2_ref[...])
    la = _bdot(lora_in, a2_ref[...])
    g = _bdot(_sigmoid(lora_in), g2_ref[...])
    y = -(w0_ref[...] + lw)
    softplus = jnp.maximum(y, 0.0) + jnp.log(1.0 + jnp.exp(-jnp.abs(y)))
    decay = jnp.exp(-jnp.exp(-softplus - 0.5))
    a = _sigmoid(a0_ref[...] + la)
    if has_vres:
        t2 = _bdot(_bdot(v, vw1_ref[...]), vw2_ref[...])
        v = v + (vf_ref[...] - v) * _sigmoid(vb_ref[...] + t2)
    kk = k * kk_ref[...]
    kk = kk / jnp.maximum(jnp.sqrt(_seg_sum(kk * kk, ones_bd)), 1e-12)
    k = k * (1.0 + (a - 1.0) * ka_ref[...])
    r_o[...] = r
    w_o[...] = decay
    k_o[...] = k
    v_o[...] = v
    a_o[...] = -kk
    b_o[...] = kk * a
    g_o[...] = g
    bonus_o[...] = _seg_sum(r * k * rk_ref[...], ones_bd) * v


def _rwkv_prep(z, prev, v_first, rp, tm):
    groups, t, _ = z.shape
    halo = prev is None
    has_vres = v_first is not None
    prcol = PR_OFF // RWKV_PROJ_PAD
    wd = RWKV_WIDTH
    tok = lambda width: pl.BlockSpec((None, tm, width), lambda g, i: (g, i, 0))
    vec = lambda width: pl.BlockSpec((1, width), lambda g, i: (0, 0))
    mat = lambda a, b: pl.BlockSpec((a, b), lambda g, i: (0, 0))
    args = [z]
    in_specs = [pl.BlockSpec((None, tm, RWKV_PROJ_PAD), lambda g, i: (g, i, prcol))]
    if halo:
        per = tm // PREP_HALO
        args.append(z)
        in_specs.append(pl.BlockSpec((None, PREP_HALO, RWKV_PROJ_PAD),
                                     lambda g, i: (g, jnp.maximum(i * per - 1, 0), prcol)))
    else:
        args.append(prev)
        in_specs.append(tok(RWKV_PROJ_PAD))
    if has_vres:
        args.append(v_first)
        in_specs.append(tok(wd))
    args += [rp["mu"], rp["w0"], rp["a0"], rp["k_k"], rp["k_a"], rp["r_k"],
             rp["w2"], rp["a2"], rp["g2"], rp["ones_bd"]]
    in_specs += [vec(RWKV_PROJ_PAD)] + [vec(wd)] * 5 + [mat(wd, wd)] * 4
    if has_vres:
        args += [rp["vw1"], rp["vw2"], rp["vb"]]
        in_specs += [mat(wd, LANES), mat(LANES, wd), vec(wd)]
    return pl.pallas_call(
        functools.partial(_rwkv_prep_kernel, halo=halo, has_vres=has_vres),
        grid=(groups, t // tm),
        in_specs=in_specs,
        out_specs=[tok(wd)] * 8,
        out_shape=[jax.ShapeDtypeStruct((groups, t, wd), F32)] * 8,
        scratch_shapes=[pltpu.VMEM((tm + PREP_HALO, RWKV_PROJ_PAD), F32)] if halo else [],
        compiler_params=_cparams("parallel", "parallel"),
        name="rwkv_prep",
    )(*args)


SCAN_STEPS_PER_TRIP = 64


def _wkv_scan_kernel(r_ref, w_ref, k_ref, v_ref, a_ref, b_ref, y_ref, s_ref):
    nb, tc, _ = r_ref.shape
    pairs = RWKV_HEADS // 2
    n = RWKV_HEAD_DIM

    @pl.when(pl.program_id(0) == 0)
    def _():
        s_ref[...] = jnp.zeros(s_ref.shape, F32)

    row = lax.broadcasted_iota(jnp.int32, (n, 2 * n), 0)
    lane = lax.broadcasted_iota(jnp.int32, (n, 2 * n), 1)
    diag = (lane % n) == row
    chains = [(bi, slice(p * 2 * n, (p + 1) * 2 * n)) for bi in range(nb) for p in range(pairs)]
    nc = len(chains)
    seg_r = lax.broadcasted_iota(jnp.int32, (2 * n, 2 * n), 0) // n
    seg_c = lax.broadcasted_iota(jnp.int32, (2 * n, 2 * n), 1) // n
    ones_bd = (seg_r == seg_c).astype(BF16)
    stack = lambda xs: jnp.concatenate(xs, axis=0)
    unstack = lambda x: [x[c * n:(c + 1) * n] for c in range(nc)]

    def seg_bcast(x):
        return jnp.dot(x.astype(BF16), ones_bd, preferred_element_type=F32)

    span = SCAN_STEPS_PER_TRIP

    def steps(trip, carry):
        base = pl.multiple_of(trip * span, span)
        rows = pl.ds(base, span)
        tiles = [[ref[bi, rows, sl] for ref in (r_ref, w_ref, k_ref, v_ref, a_ref, b_ref)] for bi, sl in chains]
        s = [s_ref[c] for c in range(nc)]
        ys = [[] for _ in chains]

        def y_rows(ycol):
            for c in range(nc):
                ys[c].append(jnp.sum(jnp.where(diag, ycol[c * n:(c + 1) * n], 0.0), axis=0, keepdims=True))

        vcol = seg_bcast(stack([jnp.where(diag, t[3][slice(u, u + 1)], 0.0) for u in range(span) for t in tiles]))
        sr = None
        for u in range(span):
            one = slice(u, u + 1)
            red = seg_bcast(stack([s[c] * tiles[c][4][one] for c in range(nc)] + (sr or [])))
            if sr:
                y_rows(red[nc * n:])
            for c, (r8, w8, k8, v8, a8, b8) in enumerate(tiles):
                s[c] = (s[c] * w8[one] + red[c * n:(c + 1) * n] * b8[one]
                        + vcol[(u * nc + c) * n:(u * nc + c + 1) * n] * k8[one])
            sr = [s[c] * tiles[c][0][one] for c in range(nc)]
        y_rows(seg_bcast(stack(sr)))
        for c, (bi, sl) in enumerate(chains):
            s_ref[c] = s[c]
            y_ref[bi, rows, sl] = jnp.concatenate(ys[c], axis=0)
        return carry

    lax.fori_loop(0, tc // span, steps, 0)


def _wkv_scan(r, w, k, v, a, b, tc):
    nb, t, wd = r.shape
    pairs = RWKV_HEADS // 2
    tok = pl.BlockSpec((nb, tc, wd), lambda i: (0, i, 0))
    return pl.pallas_call(
        _wkv_scan_kernel,
        grid=(t // tc,),
        in_specs=[tok] * 6,
        out_specs=[tok, pl.BlockSpec((nb * pairs, RWKV_HEAD_DIM, 2 * RWKV_HEAD_DIM), lambda i: (0, 0, 0))],
        out_shape=[jax.ShapeDtypeStruct((nb, t, wd), F32),
                   jax.ShapeDtypeStruct((nb * pairs, RWKV_HEAD_DIM, 2 * RWKV_HEAD_DIM), F32)],
        compiler_params=_cparams("arbitrary"),
        name="wkv_scan",
    )(r, w, k, v, a, b)


def _wkv_step_kernel(x_ref, s_ref, y_ref, so_ref):
    n = RWKV_HEAD_DIM
    r, w, k, v, a, b = (x_ref[q] for q in range(6))
    for i0 in range(0, n, SUBLANES):
        rows = range(i0, i0 + SUBLANES)
        s = [s_ref[i] for i in rows]
        sa = [jnp.sum(si * a, axis=0, keepdims=True) for si in s]
        s = [si * w + sai * b + v[i:i + 1] * k for si, sai, i in zip(s, sa, rows)]
        for si, i in zip(s, rows):
            so_ref[i] = si
        y_ref[i0:i0 + SUBLANES, :] = jnp.concatenate([jnp.sum(si * r, axis=0, keepdims=True) for si in s], axis=0)


def _wkv_step(x, state_t, l):
    _, heads, n, nb = x.shape
    return pl.pallas_call(
        _wkv_step_kernel,
        grid=(heads,),
        in_specs=[pl.BlockSpec((6, None, n, nb), lambda h: (0, h, 0, 0)),
                  pl.BlockSpec((None, None, n, n, nb), lambda h: (l, h, 0, 0, 0))],
        out_specs=[pl.BlockSpec((None, n, nb), lambda h: (h, 0, 0)),
                   pl.BlockSpec((None, n, n, nb), lambda h: (h, 0, 0, 0))],
        out_shape=[jax.ShapeDtypeStruct((heads, n, nb), F32), jax.ShapeDtypeStruct((heads, n, n, nb), F32)],
        compiler_params=_cparams("parallel"),
        name="wkv_step",
    )(x, state_t)


def _rwkv_post_kernel(y_ref, bonus_ref, g_ref, lg_ref, lb_ref, ones_ref, o_ref):
    y = y_ref[...]
    ones_bd = ones_ref[...]
    inv = 1.0 / RWKV_HEAD_DIM
    d = y - _seg_sum(y, ones_bd) * inv
    var = _seg_sum(d * d, ones_bd) * inv
    yn = d * lax.rsqrt(var + LNX_EPS) * lg_ref[...] + lb_ref[...]
    o_ref[...] = ((yn + bonus_ref[...]) * g_ref[...]).astype(o_ref.dtype)


def _rwkv_post(y, bonus, g, rp, tm):
    groups, t, wd = y.shape
    tok = pl.BlockSpec((None, tm, wd), lambda gi, i: (gi, i, 0))
    vec = pl.BlockSpec((1, wd), lambda gi, i: (0, 0))
    return pl.pallas_call(
        _rwkv_post_kernel,
        grid=(groups, t // tm),
        in_specs=[tok, tok, tok, vec, vec, pl.BlockSpec((wd, wd), lambda gi, i: (0, 0))],
        out_specs=tok,
        out_shape=jax.ShapeDtypeStruct((groups, t, wd), BF16),
        compiler_params=_cparams("parallel", "parallel"),
        name="rwkv_post",
    )(y, bonus, g, rp["lnx_g"], rp["lnx_b"], rp["ones_bd"])


MERGE_TN = 512


def _merge_kernel(oa_ref, zp_ref, yr_ref, ga_ref, gp_ref, gr_ref, wa_ref, wp_ref, wr_ref, o_ref):
    gate = lambda ref: _sigmoid(ref[...].astype(F32))
    merged = (gate(ga_ref) * _bdot(oa_ref[...], wa_ref[...])
              + gate(gp_ref) * _bdot(zp_ref[...], wp_ref[...])
              + gate(gr_ref) * _bdot(yr_ref[...], wr_ref[...]))
    o_ref[...] = merged.astype(o_ref.dtype)


def _merge(o_attn, z_pool, y_rwkv, z, w_attn_out, w_pool_out, w_rwkv_out, l, tm):
    groups, t, _ = z.shape
    tn = MERGE_TN
    gate = lambda br: pl.BlockSpec((None, tm, tn), lambda g, i, j: (g, i, (GL_OFF + br * D_MODEL) // tn + j))
    tok = lambda width: pl.BlockSpec((None, tm, width), lambda g, i, j: (g, i, 0))
    wt = lambda rows: pl.BlockSpec((None, rows, tn), lambda g, i, j: (l, 0, j))
    return pl.pallas_call(
        _merge_kernel,
        grid=(groups, t // tm, D_MODEL // tn),
        in_specs=[tok(ATTN_WIDTH), tok(POOL_WIDTH), tok(RWKV_WIDTH), gate(0), gate(1), gate(2),
                  wt(ATTN_WIDTH), wt(POOL_WIDTH), wt(RWKV_WIDTH)],
        out_specs=pl.BlockSpec((None, tm, tn), lambda g, i, j: (g, i, j)),
        out_shape=jax.ShapeDtypeStruct((groups, t, D_MODEL), BF16),
        compiler_params=_cparams("parallel", "parallel", "arbitrary"),
        name="merge",
    )(o_attn, z_pool, y_rwkv, z, z, z, w_attn_out, w_pool_out, w_rwkv_out)


def _proj_residual_kernel(a_ref, w_ref, x_ref, gt_ref, o_ref):
    o_ref[...] = x_ref[...] + (1.0 + gt_ref[...]) * _bdot(a_ref[...], w_ref[...])


def _proj_residual(a, w, l, x, mod, chunk, tm, tn):
    groups, t, kdim = a.shape
    per_row = mod.shape[1] != 1
    gate = pl.BlockSpec((None, tm if per_row else 1, tn),
                        lambda g, i, j: (g, i if per_row else 0, chunk * (D_MODEL // tn) + j))
    return pl.pallas_call(
        _proj_residual_kernel,
        grid=(groups, t // tm, D_MODEL // tn),
        in_specs=[pl.BlockSpec((None, tm, kdim), lambda g, i, j: (g, i, 0)),
                  pl.BlockSpec((None, kdim, tn), lambda g, i, j: (l, 0, j)),
                  pl.BlockSpec((None, tm, tn), lambda g, i, j: (g, i, j)),
                  gate],
        out_specs=pl.BlockSpec((None, tm, tn), lambda g, i, j: (g, i, j)),
        out_shape=jax.ShapeDtypeStruct((groups, t, D_MODEL), F32),
        compiler_params=_cparams("parallel", "parallel", "arbitrary"),
        name="proj_residual",
    )(a, w, x, mod)


FFN_TN = 256


def _ffn_up_kernel(x_ref, g_ref, sh_ref, sc_ref, wg_ref, wu_ref, o_ref, h_ref):
    j = pl.program_id(2)
    tm = x_ref.shape[0]

    def swiglu(rows):
        h = h_ref[rows, :]
        gate = jnp.dot(h, wg_ref[...].astype(BF16), preferred_element_type=F32)
        up = jnp.dot(h, wu_ref[...].astype(BF16), preferred_element_type=F32)
        o_ref[rows, :] = (gate * _sigmoid(gate) * up).astype(o_ref.dtype)

    @pl.when(j == 0)
    def _():
        for rows in _row_chunks(tm, NORM_CHUNKS):
            h_ref[rows, :] = _mod_norm(x_ref[rows, :], g_ref[...], _rows_of(sh_ref, rows),
                                       _rows_of(sc_ref, rows)).astype(BF16)
            swiglu(rows)

    @pl.when(j > 0)
    def _():
        swiglu(slice(None))


def _ffn_up(x, g2, mod, w_gate, w_up, l, tm):
    groups, t, _ = x.shape
    tn = FFN_TN
    wt = pl.BlockSpec((None, D_MODEL, tn), lambda g, i, j: (l, 0, j))
    return pl.pallas_call(
        _ffn_up_kernel,
        grid=(groups, t // tm, D_FF // tn),
        in_specs=[pl.BlockSpec((None, tm, D_MODEL), lambda g, i, j: (g, i, 0), pipeline_mode=pl.Buffered(1)),
                  pl.BlockSpec((1, D_MODEL), lambda g, i, j: (0, 0)),
                  _mod_spec(mod, tm, 3), _mod_spec(mod, tm, 4), wt, wt],
        out_specs=pl.BlockSpec((None, tm, tn), lambda g, i, j: (g, i, j)),
        out_shape=jax.ShapeDtypeStruct((groups, t, D_FF), BF16),
        scratch_shapes=[pltpu.VMEM((tm, D_MODEL), BF16)],
        compiler_params=_cparams("parallel", "parallel", "arbitrary"),
        name="ffn_up",
    )(x, g2, mod, mod, w_gate, w_up)


def _final_norm_kernel(x_ref, g_ref, o_ref):
    x = x_ref[...]
    ms = jnp.mean(x * x, axis=-1, keepdims=True)
    o_ref[...] = x * lax.rsqrt(ms + RMS_EPS) * g_ref[...]


def _final_norm(x, g, tm):
    groups, t, _ = x.shape
    tok = pl.BlockSpec((None, tm, D_MODEL), lambda gi, i: (gi, i, 0))
    return pl.pallas_call(
        _final_norm_kernel,
        grid=(groups, t // tm),
        in_specs=[tok, pl.BlockSpec((1, D_MODEL), lambda gi, i: (0, 0))],
        out_specs=tok,
        out_shape=jax.ShapeDtypeStruct(x.shape, F32),
        compiler_params=_cparams("parallel", "parallel"),
        name="final_norm",
    )(x, g)


def _pad_cols(a, width):
    return jnp.pad(a, [(0, 0)] * (a.ndim - 1) + [(0, width - a.shape[-1])])


def _rwkv_params(l, rwkv_mu, rwkv_w0, rwkv_w2, rwkv_a0, rwkv_a2, rwkv_g2, rwkv_k_k, rwkv_k_a, rwkv_r_k,
                 rwkv_lnx_g, rwkv_lnx_b, vres_w1, vres_w2, vres_b):
    wd = RWKV_WIDTH
    row = lambda a: a.reshape(1, -1)
    lora_rows = lambda w, off: jnp.zeros((wd, wd), F32).at[off:off + w.shape[0]].set(w).astype(BF16)
    seg = np.arange(wd) // RWKV_HEAD_DIM
    rp = dict(
        mu=_pad_cols(row(rwkv_mu[l]), RWKV_PROJ_PAD), w0=row(rwkv_w0[l]), a0=row(rwkv_a0[l]),
        k_k=row(rwkv_k_k[l]), k_a=row(rwkv_k_a[l]), r_k=row(rwkv_r_k[l]),
        w2=lora_rows(rwkv_w2[l], 0), a2=lora_rows(rwkv_a2[l], DECAY_LORA),
        g2=lora_rows(rwkv_g2[l], DECAY_LORA + ICLR_LORA),
        ones_bd=jnp.asarray(seg[:, None] == seg[None, :], BF16),
        lnx_g=row(rwkv_lnx_g[l]), lnx_b=row(rwkv_lnx_b[l]))
    if l > 0:
        rp.update(vw1=_pad_cols(vres_w1[l - 1], LANES).astype(BF16),
                  vw2=jnp.pad(vres_w2[l - 1], ((0, LANES - VRES_LORA), (0, 0))).astype(BF16),
                  vb=row(vres_b[l - 1]))
    return rp


def _layer_common(x, mod, z, o_attn, z_pool, y, bonus, g, lw, l, rp, tm):
    y_rwkv = _rwkv_post(y, bonus, g, rp, _row_tile(x.shape[1], 512))
    merged = _merge(o_attn, z_pool, y_rwkv, z, lw["w_attn_out"], lw["w_pool_out"], lw["w_rwkv_out"], l, tm)
    x = _proj_residual(merged, lw["w_o"], l, x, mod, 2, tm, 512)
    hidden = _ffn_up(x, lw["norm2_g"][l].reshape(1, -1), mod, lw["w_ffn_gate"], lw["w_ffn_up"], l, tm)
    return _proj_residual(hidden, lw["w_ffn_down"], l, x, mod, 5, _row_tile(x.shape[1], NORM_ROWS), 256)


def kernel(x_prompt, x_sample, c_prompt, c_sample, cache_k_win, cache_v_win, state_pool, state_shift, state_wkv, ada_w, ada_b, norm1_g, norm2_g, w_in, attn_sink, w_attn_out, pool_w, pool_scale, w_pool_out, rwkv_mu, rwkv_w0, rwkv_w2, rwkv_a0, rwkv_a2, rwkv_g2, rwkv_k_k, rwkv_k_a, rwkv_r_k, rwkv_lnx_g, rwkv_lnx_b, w_rwkv_out, vres_w1, vres_w2, vres_b, w_o, w_ffn_gate, w_ffn_up, w_ffn_down, final_norm_g):
    depth = ada_w.shape[0]
    nb, t, _ = x_prompt.shape
    ns = x_sample.shape[0]
    assert x_sample.shape[1] == 1 and t % WINDOW == 0
    win = cache_k_win.shape[2]
    assert win <= WINDOW
    heads, n = RWKV_HEADS, RWKV_HEAD_DIM

    pad_rows = -nb % SUBLANES
    c_all = jnp.concatenate([c_prompt, jnp.zeros((pad_rows, D_MODEL), F32), c_sample], axis=0)
    mod_all = _ada(c_all, ada_w, ada_b)

    cos_p, sin_p = _rope_tables(jnp.arange(t))
    cos_s, sin_s = _rope_tables(jnp.full((ns,), PAST_LEN))

    xp = x_prompt
    xs = x_sample.reshape(1, ns, D_MODEL)
    tm_p = _row_tile(t, MATMUL_ROWS)
    tm_s = _row_tile(ns, MATMUL_ROWS)
    vf_p = vf_s = None
    st_p, st_s = [], []
    lw = dict(norm2_g=norm2_g, w_attn_out=w_attn_out, w_pool_out=w_pool_out, w_rwkv_out=w_rwkv_out, w_o=w_o,
              w_ffn_gate=w_ffn_gate, w_ffn_up=w_ffn_up, w_ffn_down=w_ffn_down)
    w_in_t = w_in.transpose(0, 2, 1)
    to_feature_major = lambda c: c.transpose(0, 1, 3, 4, 2).reshape(depth, ns, KV_WIDTH, win)
    from_feature_major = lambda c: c.reshape(ns, N_KV_HEADS, HEAD_DIM, win).transpose(0, 3, 1, 2)
    cache_kt, cache_vt = to_feature_major(cache_k_win), to_feature_major(cache_v_win)
    state_t = state_wkv.transpose(0, 2, 3, 4, 1)
    for l in range(depth):
        rp = _rwkv_params(l, rwkv_mu, rwkv_w0, rwkv_w2, rwkv_a0, rwkv_a2, rwkv_g2, rwkv_k_k, rwkv_k_a, rwkv_r_k,
                          rwkv_lnx_g, rwkv_lnx_b, vres_w1, vres_w2, vres_b)
        g1 = norm1_g[l].reshape(1, -1)
        mod_p = mod_all[l, :nb].reshape(nb, 1, -1)
        mod_s = mod_all[l, nb + pad_rows:].reshape(1, ns, -1)

        z = _in_proj(xp, g1, mod_p, w_in_t, l, cos_p, sin_p, tm_p)
        o_attn = _attn_prompt(z, attn_sink[l])
        z_pool = _pool_prompt(z, pool_w[l], pool_scale[l].reshape(1, -1), _row_tile(t, 512))
        r, w, k, v, a, b, g, bonus = _rwkv_prep(z, None, vf_p, rp, _row_tile(t, 256))
        if l == 0:
            vf_p = v
        y, s_pair = _wkv_scan(r, w, k, v, a, b, _row_tile(t, 256))
        wkv_new = s_pair.reshape(nb, heads // 2, n, 2, n).transpose(0, 1, 3, 2, 4).reshape(nb, heads, n, n)
        tail = z[:, t - WINDOW:, :GL_OFF].astype(F32)
        st_p.append((tail[:, :, K_OFF:V_OFF].reshape(nb, WINDOW, N_KV_HEADS, HEAD_DIM),
                     tail[:, :, V_OFF:U_OFF].reshape(nb, WINDOW, N_KV_HEADS, HEAD_DIM),
                     tail[:, WINDOW - POOL_PAD:, U_OFF:PR_OFF],
                     tail[:, WINDOW - 1, PR_OFF:PR_OFF + RWKV_PROJ],
                     wkv_new))
        xp = _layer_common(xp, mod_p, z, o_attn, z_pool, y, bonus, g, lw, l, rp, tm_p)

        z = _in_proj(xs, g1, mod_s, w_in_t, l, cos_s, sin_s, tm_s)
        zs = z[0, :, :GL_OFF].astype(F32)
        k_new, v_new = zs[:, K_OFF:V_OFF], zs[:, V_OFF:U_OFF]
        o3, kt_new, vt_new = _attn_sample(zs[:, :ATTN_WIDTH].reshape(ns, N_Q_HEADS, HEAD_DIM), k_new, v_new,
                                          cache_kt, cache_vt, l, attn_sink[l])
        o_attn = o3.reshape(1, ns, ATTN_WIDTH).astype(BF16)
        z_pool = _pool_sample(z, state_pool[l].transpose(1, 0, 2), pool_w[l], pool_scale[l].reshape(1, -1))
        prev = _pad_cols(state_shift[l], RWKV_PROJ_PAD).reshape(1, ns, RWKV_PROJ_PAD)
        r, w, k, v, a, b, g, bonus = _rwkv_prep(z, prev, vf_s, rp, tm_s)
        if l == 0:
            vf_s = v
        step_in = jnp.stack([r, w, k, v, a, b]).reshape(6, ns, heads, n).transpose(0, 2, 3, 1)
        y_t, wkv_t = _wkv_step(step_in, state_t, l)
        y = y_t.transpose(2, 0, 1).reshape(1, ns, RWKV_WIDTH)
        u_new = zs[:, None, U_OFF:PR_OFF]
        st_s.append((from_feature_major(kt_new), from_feature_major(vt_new),
                     jnp.concatenate([state_pool[l][:, 1:], u_new], axis=1),
                     zs[:, PR_OFF:PR_OFF + RWKV_PROJ],
                     wkv_t.transpose(3, 0, 1, 2)))
        xs = _layer_common(xs, mod_s, z, o_attn, z_pool, y, bonus, g, lw, l, rp, tm_s)

    g_fin = final_norm_g.reshape(1, -1)
    y_prompt = _final_norm(xp, g_fin, _row_tile(t, NORM_ROWS))
    y_sample = _final_norm(xs, g_fin, _row_tile(ns, NORM_ROWS)).reshape(ns, 1, D_MODEL)
    stack = lambda states, i: jnp.stack([s[i] for s in states])
    return (y_prompt, y_sample) + tuple(stack(st_p, i) for i in range(5)) + tuple(stack(st_s, i) for i in range(5))
```

```python
import functools

import numpy as np
import jax
import jax.numpy as jnp
from jax import lax
from jax.experimental import pallas as pl
from jax.experimental.pallas import tpu as pltpu

F32 = jnp.float32
BF16 = jnp.bfloat16

D_MODEL = 2048
HEAD_DIM = 64
N_Q_HEADS = 16
N_KV_HEADS = 4
GROUP = N_Q_HEADS // N_KV_HEADS
ATTN_WIDTH = N_Q_HEADS * HEAD_DIM
KV_WIDTH = N_KV_HEADS * HEAD_DIM
WINDOW = 128
ROPE_THETA = 10000.0
POOL_WIDTH = 512
POOL_WINDOWS = (2, 4, 8, 16)
POOL_GROUP_WIDTH = 128
POOL_PAD = 15
RWKV_HEAD_DIM = 64
RWKV_WIDTH = 512
RWKV_HEADS = 8
DECAY_LORA = 96
ICLR_LORA = 96
GATE_LORA = 256
VRES_LORA = 32
RWKV_PROJ = 3 * RWKV_WIDTH + DECAY_LORA + ICLR_LORA + GATE_LORA
RWKV_PROJ_PAD = 2048
D_FF = 5632
PAST_LEN = 8192
RMS_EPS = 1e-6
LNX_EPS = 64e-5
NEG_INF = -1e30

Q_OFF = 0
K_OFF = ATTN_WIDTH
V_OFF = K_OFF + KV_WIDTH
U_OFF = V_OFF + KV_WIDTH
PR_OFF = U_OFF + POOL_WIDTH
GL_OFF = PR_OFF + RWKV_PROJ_PAD
IN_PAD = GL_OFF + 3 * D_MODEL
ROPE_END = V_OFF

LANES = 128
SUBLANES = 8
VMEM_LIMIT = 56 * 2**20

MATMUL_ROWS = 2048
NORM_ROWS = 1024


def _cparams(*sem):
    return pltpu.CompilerParams(dimension_semantics=sem, vmem_limit_bytes=VMEM_LIMIT)


def _row_tile(t, pref):
    tm = min(t, pref)
    while t % tm:
        tm -= SUBLANES
    return tm


def _sigmoid(x):
    return 0.5 + 0.5 * jnp.tanh(0.5 * x)


def _mod_norm(x, g, sh, sc):
    ms = jnp.mean(x * x, axis=-1, keepdims=True)
    return (x * lax.rsqrt(ms + RMS_EPS) * g) * (1.0 + sc) + sh


NORM_CHUNKS = 4


def _row_chunks(tm, n):
    size = tm // n
    return [slice(c * size, (c + 1) * size) for c in range(n)]


def _rows_of(mod_ref, rows):
    return mod_ref[...] if mod_ref.shape[0] == 1 else mod_ref[rows, :]


def _bdot(a, b):
    return jnp.dot(a.astype(BF16), b.astype(BF16), preferred_element_type=F32)


def _seg_sum(x, ones_bd):
    hi = x.astype(BF16)
    lo = (x - hi.astype(F32)).astype(BF16)
    return (jnp.dot(hi, ones_bd, preferred_element_type=F32)
            + jnp.dot(lo, ones_bd, preferred_element_type=F32))


def _mod_spec(mod, tm, chunk):
    per_row = mod.shape[1] != 1
    rows = tm if per_row else 1
    return pl.BlockSpec((None, rows, D_MODEL), lambda g, i, j=0: (g, i if per_row else 0, chunk))


def _ada_kernel(c_ref, w_ref, b_ref, o_ref):
    c = c_ref[...]
    o_ref[...] = _bdot(c * _sigmoid(c), w_ref[...]) + b_ref[...]


def _ada(c_all, ada_w, ada_b):
    depth, _, width = ada_w.shape
    rows = c_all.shape[0]
    tn = 1024
    return pl.pallas_call(
        _ada_kernel,
        grid=(depth, width // tn),
        in_specs=[pl.BlockSpec((rows, D_MODEL), lambda l, j: (0, 0)),
                  pl.BlockSpec((None, D_MODEL, tn), lambda l, j: (l, 0, j)),
                  pl.BlockSpec((None, 1, tn), lambda l, j: (l, 0, j))],
        out_specs=pl.BlockSpec((None, rows, tn), lambda l, j: (l, 0, j)),
        out_shape=jax.ShapeDtypeStruct((depth, rows, width), F32),
        compiler_params=_cparams("parallel", "parallel"),
        name="ada",
    )(c_all, ada_w, ada_b.reshape(depth, 1, width))


IN_TN = 512
IN_MAIN_TILES = GL_OFF // IN_TN


def _in_proj_kernel(x_ref, g_ref, sh_ref, sc_ref, cos_ref, sin_ref, w_ref, o_ref, h_ref):
    j = pl.program_id(2)
    tm = x_ref.shape[0]

    def project(rows):
        return lax.dot_general(h_ref[rows, :], w_ref[0].astype(BF16), (((1,), (1,)), ((), ())),
                               preferred_element_type=F32)

    def rope_tile(rows):
        acc = project(rows)
        reps = IN_TN // LANES
        cos = jnp.concatenate([cos_ref[rows, :]] * reps, axis=1)
        sin = jnp.concatenate([sin_ref[rows, :]] * reps, axis=1)
        lane = lax.broadcasted_iota(jnp.int32, acc.shape, 1)
        first_half = (lane % HEAD_DIM) < (HEAD_DIM // 2)
        rot = jnp.where(first_half, pltpu.roll(acc, IN_TN - HEAD_DIM // 2, 1),
                        pltpu.roll(acc, HEAD_DIM // 2, 1))
        roped = acc * cos + rot * sin
        o_ref[rows, :] = jnp.where(j * IN_TN + lane < ROPE_END, roped, acc).astype(o_ref.dtype)

    @pl.when(j == 0)
    def _():
        for rows in _row_chunks(tm, NORM_CHUNKS):
            h_ref[rows, :] = _mod_norm(x_ref[rows, :], g_ref[...], _rows_of(sh_ref, rows),
                                       _rows_of(sc_ref, rows)).astype(BF16)
            rope_tile(rows)

    @pl.when((j > 0) & (j * IN_TN < ROPE_END))
    def _():
        rope_tile(slice(None))

    @pl.when(j * IN_TN >= ROPE_END)
    def _():
        o_ref[...] = project(slice(None)).astype(o_ref.dtype)


def _in_proj(x, g1, mod, w_in_t, l, cos, sin, tm):
    groups, t, _ = x.shape
    gate_shift = RWKV_PROJ_PAD - RWKV_PROJ

    def weight_rows(g, i, j):
        start = jnp.where(j < IN_MAIN_TILES, j * IN_TN, j * IN_TN - gate_shift)
        return (l, pl.multiple_of(start, SUBLANES), 0)

    return pl.pallas_call(
        _in_proj_kernel,
        grid=(groups, t // tm, IN_PAD // IN_TN),
        in_specs=[pl.BlockSpec((None, tm, D_MODEL), lambda g, i, j: (g, i, 0), pipeline_mode=pl.Buffered(1)),
                  pl.BlockSpec((1, D_MODEL), lambda g, i, j: (0, 0)),
                  _mod_spec(mod, tm, 0), _mod_spec(mod, tm, 1),
                  pl.BlockSpec((tm, LANES), lambda g, i, j: (i, 0), pipeline_mode=pl.Buffered(1)),
                  pl.BlockSpec((tm, LANES), lambda g, i, j: (i, 0), pipeline_mode=pl.Buffered(1)),
                  pl.BlockSpec((pl.Element(1), pl.Element(IN_TN), pl.Element(D_MODEL)), weight_rows)],
        out_specs=pl.BlockSpec((None, tm, IN_TN), lambda g, i, j: (g, i, j)),
        out_shape=jax.ShapeDtypeStruct((groups, t, IN_PAD), BF16),
        scratch_shapes=[pltpu.VMEM((tm, D_MODEL), BF16)],
        compiler_params=_cparams("parallel", "parallel", "arbitrary"),
        name="in_proj",
    )(x, g1, mod, mod, cos, sin, w_in_t)


def _rope_tables(pos):
    half = HEAD_DIM // 2
    inv = ROPE_THETA ** (-jnp.arange(half, dtype=F32) * 2.0 / HEAD_DIM)
    ang = pos.astype(F32)[:, None] * inv[None, :]
    cos, sin = jnp.cos(ang), jnp.sin(ang)
    reps = LANES // HEAD_DIM
    cos_t = jnp.concatenate([cos, cos] * reps, axis=1)
    sin_t = jnp.concatenate([-sin, sin] * reps, axis=1)
    return cos_t, sin_t


def _attn_prompt_kernel(sink_ref, q_ref, k_ref, v_ref, kp_ref, vp_ref, o_ref):
    n = pl.program_id(1)
    blk = q_ref.shape[0]
    q = q_ref[...]
    k2 = jnp.concatenate([kp_ref[...], k_ref[...]], axis=0).astype(BF16)
    v2 = jnp.concatenate([vp_ref[...], v_ref[...]], axis=0).astype(BF16)
    qi = lax.broadcasted_iota(jnp.int32, (blk, 2 * blk), 0)
    kc = lax.broadcasted_iota(jnp.int32, (blk, 2 * blk), 1)
    rel = qi - kc + blk
    first_key = jnp.where(n > 0, 0, blk)
    mask = (rel >= 0) & (rel <= WINDOW) & (kc >= first_key)
    outs = []
    for hd in range(N_Q_HEADS):
        h = hd // GROUP
        kh = k2[:, h * HEAD_DIM:(h + 1) * HEAD_DIM]
        vh = v2[:, h * HEAD_DIM:(h + 1) * HEAD_DIM]
        qh = q[:, hd * HEAD_DIM:(hd + 1) * HEAD_DIM].astype(BF16)
        s = lax.dot_general(qh, kh, (((1,), (1,)), ((), ())), preferred_element_type=F32) * (HEAD_DIM ** -0.5)
        s = jnp.where(mask, s, NEG_INF)
        sk = sink_ref[hd]
        m = jnp.maximum(jnp.max(s, axis=1, keepdims=True), sk)
        p = jnp.exp(s - m)
        den = jnp.sum(p, axis=1, keepdims=True) + jnp.exp(sk - m)
        outs.append(jnp.dot(p.astype(BF16), vh, preferred_element_type=F32) / den)
    o_ref[...] = jnp.concatenate(outs, axis=1).astype(o_ref.dtype)


def _attn_prompt(z, sink):
    groups, t, _ = z.shape
    blk = WINDOW
    kcol, vcol = K_OFF // KV_WIDTH, V_OFF // KV_WIDTH
    prev = lambda col: (lambda g, n: (g, jnp.maximum(n - 1, 0), col))
    return pl.pallas_call(
        _attn_prompt_kernel,
        grid=(groups, t // blk),
        in_specs=[pl.BlockSpec(memory_space=pltpu.SMEM),
                  pl.BlockSpec((None, blk, ATTN_WIDTH), lambda g, n: (g, n, 0)),
                  pl.BlockSpec((None, blk, KV_WIDTH), lambda g, n: (g, n, kcol)),
                  pl.BlockSpec((None, blk, KV_WIDTH), lambda g, n: (g, n, vcol)),
                  pl.BlockSpec((None, blk, KV_WIDTH), prev(kcol)),
                  pl.BlockSpec((None, blk, KV_WIDTH), prev(vcol))],
        out_specs=pl.BlockSpec((None, blk, ATTN_WIDTH), lambda g, n: (g, n, 0)),
        out_shape=jax.ShapeDtypeStruct((groups, t, ATTN_WIDTH), BF16),
        compiler_params=_cparams("parallel", "parallel"),
        name="attn_prompt",
    )(sink, z, z, z, z, z)


def _attn_sample_kernel(sink_ref, q_ref, kn_ref, vn_ref, knc_ref, vnc_ref, kt_ref, vt_ref, o_ref, kto_ref, vto_ref):
    bb = q_ref.shape[0]
    win = kt_ref.shape[2]
    row = lax.broadcasted_iota(jnp.int32, (N_Q_HEADS, KV_WIDTH), 0)
    lane = lax.broadcasted_iota(jnp.int32, (N_Q_HEADS, KV_WIDTH), 1)
    own = (row // GROUP) == (lane // HEAD_DIM)
    newest = lax.broadcasted_iota(jnp.int32, (KV_WIDTH, win), 1) == win - 1
    sink = sink_ref[...]
    scale = HEAD_DIM ** -0.5

    rows = range(bb)
    qbd = [jnp.where(own, jnp.concatenate([q_ref[b]] * N_KV_HEADS, axis=1), 0.0) for b in rows]
    s = [jnp.dot(qbd[b].astype(BF16), kt_ref[b].astype(BF16), preferred_element_type=F32) * scale for b in rows]
    s_new = [jnp.sum(qbd[b] * kn_ref[b], axis=1, keepdims=True) * scale for b in rows]
    m = [jnp.maximum(jnp.maximum(jnp.max(s[b], axis=1, keepdims=True), s_new[b]), sink) for b in rows]
    p = [jnp.exp(s[b] - m[b]) for b in rows]
    p_new = [jnp.exp(s_new[b] - m[b]) for b in rows]
    den = [jnp.sum(p[b], axis=1, keepdims=True) + p_new[b] + jnp.exp(sink - m[b]) for b in rows]
    o = [lax.dot_general(p[b].astype(BF16), vt_ref[b].astype(BF16), (((1,), (1,)), ((), ())),
                         preferred_element_type=F32) + p_new[b] * vn_ref[b] for b in rows]
    for b in rows:
        ob = jnp.where(own, o[b], 0.0)
        o64 = ob[:, 0:HEAD_DIM]
        for h in range(1, N_KV_HEADS):
            o64 = o64 + ob[:, h * HEAD_DIM:(h + 1) * HEAD_DIM]
        o_ref[b] = o64 / den[b]
        kto_ref[b] = jnp.where(newest, knc_ref[b], pltpu.roll(kt_ref[b], win - 1, 1))
        vto_ref[b] = jnp.where(newest, vnc_ref[b], pltpu.roll(vt_ref[b], win - 1, 1))


def _attn_sample(q3, k_new, v_new, kt_all, vt_all, l, sink):
    _, nb, _, win = kt_all.shape
    assert win == LANES
    bb = _row_tile(nb, 8)
    row = pl.BlockSpec((bb, 1, KV_WIDTH), lambda i: (i, 0, 0))
    col = pl.BlockSpec((bb, KV_WIDTH, 1), lambda i: (i, 0, 0))
    cache_in = pl.BlockSpec((None, bb, KV_WIDTH, win), lambda i: (l, i, 0, 0))
    cache_out = pl.BlockSpec((bb, KV_WIDTH, win), lambda i: (i, 0, 0))
    heads = pl.BlockSpec((bb, N_Q_HEADS, HEAD_DIM), lambda i: (i, 0, 0))
    return pl.pallas_call(
        _attn_sample_kernel,
        grid=(nb // bb,),
        in_specs=[pl.BlockSpec((N_Q_HEADS, 1), lambda i: (0, 0)), heads, row, row, col, col, cache_in, cache_in],
        out_specs=[heads, cache_out, cache_out],
        out_shape=[jax.ShapeDtypeStruct((nb, N_Q_HEADS, HEAD_DIM), F32),
                   jax.ShapeDtypeStruct((nb, KV_WIDTH, win), F32),
                   jax.ShapeDtypeStruct((nb, KV_WIDTH, win), F32)],
        compiler_params=_cparams("parallel"),
        name="attn_sample",
    )(sink.reshape(N_Q_HEADS, 1), q3, k_new.reshape(nb, 1, KV_WIDTH), v_new.reshape(nb, 1, KV_WIDTH),
      k_new.reshape(nb, KV_WIDTH, 1), v_new.reshape(nb, KV_WIDTH, 1), kt_all, vt_all)


POOL_HALO = 16


def _pool_prompt_kernel(u_ref, halo_ref, w_ref, sc_ref, o_ref, ext_ref):
    i = pl.program_id(1)
    tm = u_ref.shape[0]
    ext_ref[0:POOL_HALO, :] = jnp.where(i > 0, halo_ref[...].astype(F32), 0.0)
    ext_ref[POOL_HALO:POOL_HALO + tm, :] = u_ref[...].astype(F32)
    pos = i * tm + lax.broadcasted_iota(jnp.int32, (tm, 1), 0)
    for gi, w in enumerate(POOL_WINDOWS):
        cols = slice(gi * POOL_GROUP_WIDTH, (gi + 1) * POOL_GROUP_WIDTH)
        u = ext_ref[POOL_HALO:POOL_HALO + tm, cols]
        acc = u
        for back in range(1, w):
            acc = acc + ext_ref[POOL_HALO - back:POOL_HALO - back + tm, cols]
        cnt = jnp.minimum(pos + 1, w).astype(F32)
        d = acc / cnt - u
        o_ref[:, cols] = _bdot(d, w_ref[gi]) * sc_ref[:, cols]


def _pool_prompt(z, pool_w, pool_scale, tm):
    groups, t, _ = z.shape
    ucol = U_OFF // POOL_WIDTH
    per = tm // POOL_HALO
    return pl.pallas_call(
        _pool_prompt_kernel,
        grid=(groups, t // tm),
        in_specs=[pl.BlockSpec((None, tm, POOL_WIDTH), lambda g, i: (g, i, ucol)),
                  pl.BlockSpec((None, POOL_HALO, POOL_WIDTH), lambda g, i: (g, jnp.maximum(i * per - 1, 0), ucol)),
                  pl.BlockSpec((len(POOL_WINDOWS), POOL_GROUP_WIDTH, POOL_GROUP_WIDTH), lambda g, i: (0, 0, 0)),
                  pl.BlockSpec((1, POOL_WIDTH), lambda g, i: (0, 0))],
        out_specs=pl.BlockSpec((None, tm, POOL_WIDTH), lambda g, i: (g, i, 0)),
        out_shape=jax.ShapeDtypeStruct((groups, t, POOL_WIDTH), F32),
        scratch_shapes=[pltpu.VMEM((tm + POOL_HALO, POOL_WIDTH), F32)],
        compiler_params=_cparams("parallel", "parallel"),
        name="pool_prompt",
    )(z, z, pool_w, pool_scale)


def _pool_sample_kernel(u_ref, st_ref, w_ref, sc_ref, o_ref):
    for gi, w in enumerate(POOL_WINDOWS):
        cols = slice(gi * POOL_GROUP_WIDTH, (gi + 1) * POOL_GROUP_WIDTH)
        u = u_ref[:, cols].astype(F32)
        acc = u
        for back in range(1, w):
            acc = acc + st_ref[POOL_PAD - back, :, cols]
        cnt = float(min(PAST_LEN + 1, w))
        d = acc / cnt - u
        o_ref[:, cols] = _bdot(d, w_ref[gi]) * sc_ref[:, cols]


def _pool_sample(z, state_t, pool_w, pool_scale):
    rows = z.shape[1]
    ucol = U_OFF // POOL_WIDTH
    return pl.pallas_call(
        _pool_sample_kernel,
        grid=(1,),
        in_specs=[pl.BlockSpec((None, rows, POOL_WIDTH), lambda i: (0, 0, ucol)),
                  pl.BlockSpec((POOL_PAD, rows, POOL_WIDTH), lambda i: (0, 0, 0)),
                  pl.BlockSpec((len(POOL_WINDOWS), POOL_GROUP_WIDTH, POOL_GROUP_WIDTH), lambda i: (0, 0, 0)),
                  pl.BlockSpec((1, POOL_WIDTH), lambda i: (0, 0))],
        out_specs=pl.BlockSpec((None, rows, POOL_WIDTH), lambda i: (0, 0, 0)),
        out_shape=jax.ShapeDtypeStruct((1, rows, POOL_WIDTH), F32),
        compiler_params=_cparams("arbitrary"),
        name="pool_sample",
    )(z, state_t, pool_w, pool_scale)


PREP_HALO = 16


def _rwkv_prep_kernel(*refs, halo, has_vres):
    it = iter(refs)
    pr_ref, prev_ref = next(it), next(it)
    vf_ref = next(it) if has_vres else None
    mu_ref, w0_ref, a0_ref, kk_ref, ka_ref, rk_ref = (next(it) for _ in range(6))
    w2_ref, a2_ref, g2_ref, ones_ref = (next(it) for _ in range(4))
    if has_vres:
        vw1_ref, vw2_ref, vb_ref = next(it), next(it), next(it)
    r_o, w_o, k_o, v_o, a_o, b_o, g_o, bonus_o = (next(it) for _ in range(8))
    ext_ref = next(it) if halo else None

    pr = pr_ref[...].astype(F32)
    tm = pr.shape[0]
    if halo:
        i = pl.program_id(1)
        ext_ref[0:PREP_HALO, :] = jnp.where(i > 0, prev_ref[...].astype(F32), 0.0)
        ext_ref[PREP_HALO:PREP_HALO + tm, :] = pr
        prev = ext_ref[PREP_HALO - 1:PREP_HALO - 1 + tm, :]
    else:
        prev = prev_ref[...]
    xm = pr + (prev - pr) * mu_ref[...]
    wd = RWKV_WIDTH
    r, k, v, lora_in = xm[:, 0:wd], xm[:, wd:2 * wd], xm[:, 2 * wd:3 * wd], xm[:, 3 * wd:4 * wd]
    ones_bd = ones_ref[...]
    lw = _bdot(jnp.tanh(lora_in), w2_ref[...])
    la = _bdot(lora_in, a2_ref[...])
    g = _bdot(_sigmoid(lora_in), g2_ref[...])
    y = -(w0_ref[...] + lw)
    softplus = jnp.maximum(y, 0.0) + jnp.log(1.0 + jnp.exp(-jnp.abs(y)))
    decay = jnp.exp(-jnp.exp(-softplus - 0.5))
    a = _sigmoid(a0_ref[...] + la)
    if has_vres:
        t2 = _bdot(_bdot(v, vw1_ref[...]), vw2_ref[...])
        v = v + (vf_ref[...] - v) * _sigmoid(vb_ref[...] + t2)
    kk = k * kk_ref[...]
    kk = kk / jnp.maximum(jnp.sqrt(_seg_sum(kk * kk, ones_bd)), 1e-12)
    k = k * (1.0 + (a - 1.0) * ka_ref[...])
    r_o[...] = r
    w_o[...] = decay
    k_o[...] = k
    v_o[...] = v
    a_o[...] = -kk
    b_o[...] = kk * a
    g_o[...] = g
    bonus_o[...] = _seg_sum(r * k * rk_ref[...], ones_bd) * v


def _rwkv_prep(z, prev, v_first, rp, tm):
    groups, t, _ = z.shape
    halo = prev is None
    has_vres = v_first is not None
    prcol = PR_OFF // RWKV_PROJ_PAD
    wd = RWKV_WIDTH
    tok = lambda width: pl.BlockSpec((None, tm, width), lambda g, i: (g, i, 0))
    vec = lambda width: pl.BlockSpec((1, width), lambda g, i: (0, 0))
    mat = lambda a, b: pl.BlockSpec((a, b), lambda g, i: (0, 0))
    args = [z]
    in_specs = [pl.BlockSpec((None, tm, RWKV_PROJ_PAD), lambda g, i: (g, i, prcol))]
    if halo:
        per = tm // PREP_HALO
        args.append(z)
        in_specs.append(pl.BlockSpec((None, PREP_HALO, RWKV_PROJ_PAD),
                                     lambda g, i: (g, jnp.maximum(i * per - 1, 0), prcol)))
    else:
        args.append(prev)
        in_specs.append(tok(RWKV_PROJ_PAD))
    if has_vres:
        args.append(v_first)
        in_specs.append(tok(wd))
    args += [rp["mu"], rp["w0"], rp["a0"], rp["k_k"], rp["k_a"], rp["r_k"],
             rp["w2"], rp["a2"], rp["g2"], rp["ones_bd"]]
    in_specs += [vec(RWKV_PROJ_PAD)] + [vec(wd)] * 5 + [mat(wd, wd)] * 4
    if has_vres:
        args += [rp["vw1"], rp["vw2"], rp["vb"]]
        in_specs += [mat(wd, LANES), mat(LANES, wd), vec(wd)]
    return pl.pallas_call(
        functools.partial(_rwkv_prep_kernel, halo=halo, has_vres=has_vres),
        grid=(groups, t // tm),
        in_specs=in_specs,
        out_specs=[tok(wd)] * 8,
        out_shape=[jax.ShapeDtypeStruct((groups, t, wd), F32)] * 8,
        scratch_shapes=[pltpu.VMEM((tm + PREP_HALO, RWKV_PROJ_PAD), F32)] if halo else [],
        compiler_params=_cparams("parallel", "parallel"),
        name="rwkv_prep",
    )(*args)


SCAN_STEPS_PER_TRIP = 64


def _wkv_scan_kernel(r_ref, w_ref, k_ref, v_ref, a_ref, b_ref, y_ref, s_ref):
    nb, tc, _ = r_ref.shape
    pairs = RWKV_HEADS // 2
    n = RWKV_HEAD_DIM

    @pl.when(pl.program_id(0) == 0)
    def _():
        s_ref[...] = jnp.zeros(s_ref.shape, F32)

    row = lax.broadcasted_iota(jnp.int32, (n, 2 * n), 0)
    lane = lax.broadcasted_iota(jnp.int32, (n, 2 * n), 1)
    diag = (lane % n) == row
    chains = [(bi, slice(p * 2 * n, (p + 1) * 2 * n)) for bi in range(nb) for p in range(pairs)]
    nc = len(chains)
    seg_r = lax.broadcasted_iota(jnp.int32, (2 * n, 2 * n), 0) // n
    seg_c = lax.broadcasted_iota(jnp.int32, (2 * n, 2 * n), 1) // n
    ones_bd = (seg_r == seg_c).astype(BF16)
    stack = lambda xs: jnp.concatenate(xs, axis=0)
    unstack = lambda x: [x[c * n:(c + 1) * n] for c in range(nc)]

    def seg_bcast(x):
        return jnp.dot(x.astype(BF16), ones_bd, preferred_element_type=F32)

    span = SCAN_STEPS_PER_TRIP

    def steps(trip, carry):
        base = pl.multiple_of(trip * span, span)
        rows = pl.ds(base, span)
        tiles = [[ref[bi, rows, sl] for ref in (r_ref, w_ref, k_ref, v_ref, a_ref, b_ref)] for bi, sl in chains]
        s = [s_ref[c] for c in range(nc)]
        ys = [[] for _ in chains]

        def y_rows(ycol):
            for c in range(nc):
                ys[c].append(jnp.sum(jnp.where(diag, ycol[c * n:(c + 1) * n], 0.0), axis=0, keepdims=True))

        vcol = seg_bcast(stack([jnp.where(diag, t[3][slice(u, u + 1)], 0.0) for u in range(span) for t in tiles]))
        sr = None
        for u in range(span):
            one = slice(u, u + 1)
            red = seg_bcast(stack([s[c] * tiles[c][4][one] for c in range(nc)] + (sr or [])))
            if sr:
                y_rows(red[nc * n:])
            for c, (r8, w8, k8, v8, a8, b8) in enumerate(tiles):
                s[c] = (s[c] * w8[one] + red[c * n:(c + 1) * n] * b8[one]
                        + vcol[(u * nc + c) * n:(u * nc + c + 1) * n] * k8[one])
            sr = [s[c] * tiles[c][0][one] for c in range(nc)]
        y_rows(seg_bcast(stack(sr)))
        for c, (bi, sl) in enumerate(chains):
            s_ref[c] = s[c]
            y_ref[bi, rows, sl] = jnp.concatenate(ys[c], axis=0)
        return carry

    lax.fori_loop(0, tc // span, steps, 0)


def _wkv_scan(r, w, k, v, a, b, tc):
    nb, t, wd = r.shape
    pairs = RWKV_HEADS // 2
    tok = pl.BlockSpec((nb, tc, wd), lambda i: (0, i, 0))
    return pl.pallas_call(
        _wkv_scan_kernel,
        grid=(t // tc,),
        in_specs=[tok] * 6,
        out_specs=[tok, pl.BlockSpec((nb * pairs, RWKV_HEAD_DIM, 2 * RWKV_HEAD_DIM), lambda i: (0, 0, 0))],
        out_shape=[jax.ShapeDtypeStruct((nb, t, wd), F32),
                   jax.ShapeDtypeStruct((nb * pairs, RWKV_HEAD_DIM, 2 * RWKV_HEAD_DIM), F32)],
        compiler_params=_cparams("arbitrary"),
        name="wkv_scan",
    )(r, w, k, v, a, b)


def _wkv_step_kernel(x_ref, s_ref, y_ref, so_ref):
    n = RWKV_HEAD_DIM
    r, w, k, v, a, b = (x_ref[q] for q in range(6))
    for i0 in range(0, n, SUBLANES):
        rows = range(i0, i0 + SUBLANES)
        s = [s_ref[i] for i in rows]
        sa = [jnp.sum(si * a, axis=0, keepdims=True) for si in s]
        s = [si * w + sai * b + v[i:i + 1] * k for si, sai, i in zip(s, sa, rows)]
        for si, i in zip(s, rows):
            so_ref[i] = si
        y_ref[i0:i0 + SUBLANES, :] = jnp.concatenate([jnp.sum(si * r, axis=0, keepdims=True) for si in s], axis=0)


def _wkv_step(x, state_t, l):
    _, heads, n, nb = x.shape
    return pl.pallas_call(
        _wkv_step_kernel,
        grid=(heads,),
        in_specs=[pl.BlockSpec((6, None, n, nb), lambda h: (0, h, 0, 0)),
                  pl.BlockSpec((None, None, n, n, nb), lambda h: (l, h, 0, 0, 0))],
        out_specs=[pl.BlockSpec((None, n, nb), lambda h: (h, 0, 0)),
                   pl.BlockSpec((None, n, n, nb), lambda h: (h, 0, 0, 0))],
        out_shape=[jax.ShapeDtypeStruct((heads, n, nb), F32), jax.ShapeDtypeStruct((heads, n, n, nb), F32)],
        compiler_params=_cparams("parallel"),
        name="wkv_step",
    )(x, state_t)


def _rwkv_post_kernel(y_ref, bonus_ref, g_ref, lg_ref, lb_ref, ones_ref, o_ref):
    y = y_ref[...]
    ones_bd = ones_ref[...]
    inv = 1.0 / RWKV_HEAD_DIM
    d = y - _seg_sum(y, ones_bd) * inv
    var = _seg_sum(d * d, ones_bd) * inv
    yn = d * lax.rsqrt(var + LNX_EPS) * lg_ref[...] + lb_ref[...]
    o_ref[...] = ((yn + bonus_ref[...]) * g_ref[...]).astype(o_ref.dtype)


def _rwkv_post(y, bonus, g, rp, tm):
    groups, t, wd = y.shape
    tok = pl.BlockSpec((None, tm, wd), lambda gi, i: (gi, i, 0))
    vec = pl.BlockSpec((1, wd), lambda gi, i: (0, 0))
    return pl.pallas_call(
        _rwkv_post_kernel,
        grid=(groups, t // tm),
        in_specs=[tok, tok, tok, vec, vec, pl.BlockSpec((wd, wd), lambda gi, i: (0, 0))],
        out_specs=tok,
        out_shape=jax.ShapeDtypeStruct((groups, t, wd), BF16),
        compiler_params=_cparams("parallel", "parallel"),
        name="rwkv_post",
    )(y, bonus, g, rp["lnx_g"], rp["lnx_b"], rp["ones_bd"])


MERGE_TN = 512


def _merge_kernel(oa_ref, zp_ref, yr_ref, ga_ref, gp_ref, gr_ref, wa_ref, wp_ref, wr_ref, o_ref):
    gate = lambda ref: _sigmoid(ref[...].astype(F32))
    merged = (gate(ga_ref) * _bdot(oa_ref[...], wa_ref[...])
              + gate(gp_ref) * _bdot(zp_ref[...], wp_ref[...])
              + gate(gr_ref) * _bdot(yr_ref[...], wr_ref[...]))
    o_ref[...] = merged.astype(o_ref.dtype)


def _merge(o_attn, z_pool, y_rwkv, z, w_attn_out, w_pool_out, w_rwkv_out, l, tm):
    groups, t, _ = z.shape
    tn = MERGE_TN
    gate = lambda br: pl.BlockSpec((None, tm, tn), lambda g, i, j: (g, i, (GL_OFF + br * D_MODEL) // tn + j))
    tok = lambda width: pl.BlockSpec((None, tm, width), lambda g, i, j: (g, i, 0))
    wt = lambda rows: pl.BlockSpec((None, rows, tn), lambda g, i, j: (l, 0, j))
    return pl.pallas_call(
        _merge_kernel,
        grid=(groups, t // tm, D_MODEL // tn),
        in_specs=[tok(ATTN_WIDTH), tok(POOL_WIDTH), tok(RWKV_WIDTH), gate(0), gate(1), gate(2),
                  wt(ATTN_WIDTH), wt(POOL_WIDTH), wt(RWKV_WIDTH)],
        out_specs=pl.BlockSpec((None, tm, tn), lambda g, i, j: (g, i, j)),
        out_shape=jax.ShapeDtypeStruct((groups, t, D_MODEL), BF16),
        compiler_params=_cparams("parallel", "parallel", "arbitrary"),
        name="merge",
    )(o_attn, z_pool, y_rwkv, z, z, z, w_attn_out, w_pool_out, w_rwkv_out)


def _proj_residual_kernel(a_ref, w_ref, x_ref, gt_ref, o_ref):
    o_ref[...] = x_ref[...] + (1.0 + gt_ref[...]) * _bdot(a_ref[...], w_ref[...])


def _proj_residual(a, w, l, x, mod, chunk, tm, tn):
    groups, t, kdim = a.shape
    per_row = mod.shape[1] != 1
    gate = pl.BlockSpec((None, tm if per_row else 1, tn),
                        lambda g, i, j: (g, i if per_row else 0, chunk * (D_MODEL // tn) + j))
    return pl.pallas_call(
        _proj_residual_kernel,
        grid=(groups, t // tm, D_MODEL // tn),
        in_specs=[pl.BlockSpec((None, tm, kdim), lambda g, i, j: (g, i, 0)),
                  pl.BlockSpec((None, kdim, tn), lambda g, i, j: (l, 0, j)),
                  pl.BlockSpec((None, tm, tn), lambda g, i, j: (g, i, j)),
                  gate],
        out_specs=pl.BlockSpec((None, tm, tn), lambda g, i, j: (g, i, j)),
        out_shape=jax.ShapeDtypeStruct((groups, t, D_MODEL), F32),
        compiler_params=_cparams("parallel", "parallel", "arbitrary"),
        name="proj_residual",
    )(a, w, x, mod)


FFN_TN = 256


def _ffn_up_kernel(x_ref, g_ref, sh_ref, sc_ref, wg_ref, wu_ref, o_ref, h_ref):
    j = pl.program_id(2)
    tm = x_ref.shape[0]

    def swiglu(rows):
        h = h_ref[rows, :]
        gate = jnp.dot(h, wg_ref[...].astype(BF16), preferred_element_type=F32)
        up = jnp.dot(h, wu_ref[...].astype(BF16), preferred_element_type=F32)
        o_ref[rows, :] = (gate * _sigmoid(gate) * up).astype(o_ref.dtype)

    @pl.when(j == 0)
    def _():
        for rows in _row_chunks(tm, NORM_CHUNKS):
            h_ref[rows, :] = _mod_norm(x_ref[rows, :], g_ref[...], _rows_of(sh_ref, rows),
                                       _rows_of(sc_ref, rows)).astype(BF16)
            swiglu(rows)

    @pl.when(j > 0)
    def _():
        swiglu(slice(None))


def _ffn_up(x, g2, mod, w_gate, w_up, l, tm):
    groups, t, _ = x.shape
    tn = FFN_TN
    wt = pl.BlockSpec((None, D_MODEL, tn), lambda g, i, j: (l, 0, j))
    return pl.pallas_call(
        _ffn_up_kernel,
        grid=(groups, t // tm, D_FF // tn),
        in_specs=[pl.BlockSpec((None, tm, D_MODEL), lambda g, i, j: (g, i, 0), pipeline_mode=pl.Buffered(1)),
                  pl.BlockSpec((1, D_MODEL), lambda g, i, j: (0, 0)),
                  _mod_spec(mod, tm, 3), _mod_spec(mod, tm, 4), wt, wt],
        out_specs=pl.BlockSpec((None, tm, tn), lambda g, i, j: (g, i, j)),
        out_shape=jax.ShapeDtypeStruct((groups, t, D_FF), BF16),
        scratch_shapes=[pltpu.VMEM((tm, D_MODEL), BF16)],
        compiler_params=_cparams("parallel", "parallel", "arbitrary"),
        name="ffn_up",
    )(x, g2, mod, mod, w_gate, w_up)


def _final_norm_kernel(x_ref, g_ref, o_ref):
    x = x_ref[...]
    ms = jnp.mean(x * x, axis=-1, keepdims=True)
    o_ref[...] = x * lax.rsqrt(ms + RMS_EPS) * g_ref[...]


def _final_norm(x, g, tm):
    groups, t, _ = x.shape
    tok = pl.BlockSpec((None, tm, D_MODEL), lambda gi, i: (gi, i, 0))
    return pl.pallas_call(
        _final_norm_kernel,
        grid=(groups, t // tm),
        in_specs=[tok, pl.BlockSpec((1, D_MODEL), lambda gi, i: (0, 0))],
        out_specs=tok,
        out_shape=jax.ShapeDtypeStruct(x.shape, F32),
        compiler_params=_cparams("parallel", "parallel"),
        name="final_norm",
    )(x, g)


def _pad_cols(a, width):
    return jnp.pad(a, [(0, 0)] * (a.ndim - 1) + [(0, width - a.shape[-1])])


def _rwkv_params(l, rwkv_mu, rwkv_w0, rwkv_w2, rwkv_a0, rwkv_a2, rwkv_g2, rwkv_k_k, rwkv_k_a, rwkv_r_k,
                 rwkv_lnx_g, rwkv_lnx_b, vres_w1, vres_w2, vres_b):
    wd = RWKV_WIDTH
    row = lambda a: a.reshape(1, -1)
    lora_rows = lambda w, off: jnp.zeros((wd, wd), F32).at[off:off + w.shape[0]].set(w).astype(BF16)
    seg = np.arange(wd) // RWKV_HEAD_DIM
    rp = dict(
        mu=_pad_cols(row(rwkv_mu[l]), RWKV_PROJ_PAD), w0=row(rwkv_w0[l]), a0=row(rwkv_a0[l]),
        k_k=row(rwkv_k_k[l]), k_a=row(rwkv_k_a[l]), r_k=row(rwkv_r_k[l]),
        w2=lora_rows(rwkv_w2[l], 0), a2=lora_rows(rwkv_a2[l], DECAY_LORA),
        g2=lora_rows(rwkv_g2[l], DECAY_LORA + ICLR_LORA),
        ones_bd=jnp.asarray(seg[:, None] == seg[None, :], BF16),
        lnx_g=row(rwkv_lnx_g[l]), lnx_b=row(rwkv_lnx_b[l]))
    if l > 0:
        rp.update(vw1=_pad_cols(vres_w1[l - 1], LANES).astype(BF16),
                  vw2=jnp.pad(vres_w2[l - 1], ((0, LANES - VRES_LORA), (0, 0))).astype(BF16),
                  vb=row(vres_b[l - 1]))
    return rp


def _layer_common(x, mod, z, o_attn, z_pool, y, bonus, g, lw, l, rp, tm):
    y_rwkv = _rwkv_post(y, bonus, g, rp, _row_tile(x.shape[1], 512))
    merged = _merge(o_attn, z_pool, y_rwkv, z, lw["w_attn_out"], lw["w_pool_out"], lw["w_rwkv_out"], l, tm)
    x = _proj_residual(merged, lw["w_o"], l, x, mod, 2, tm, 512)
    hidden = _ffn_up(x, lw["norm2_g"][l].reshape(1, -1), mod, lw["w_ffn_gate"], lw["w_ffn_up"], l, tm)
    return _proj_residual(hidden, lw["w_ffn_down"], l, x, mod, 5, _row_tile(x.shape[1], NORM_ROWS), 512)


def kernel(x_prompt, x_sample, c_prompt, c_sample, cache_k_win, cache_v_win, state_pool, state_shift, state_wkv, ada_w, ada_b, norm1_g, norm2_g, w_in, attn_sink, w_attn_out, pool_w, pool_scale, w_pool_out, rwkv_mu, rwkv_w0, rwkv_w2, rwkv_a0, rwkv_a2, rwkv_g2, rwkv_k_k, rwkv_k_a, rwkv_r_k, rwkv_lnx_g, rwkv_lnx_b, w_rwkv_out, vres_w1, vres_w2, vres_b, w_o, w_ffn_gate, w_ffn_up, w_ffn_down, final_norm_g):
    depth = ada_w.shape[0]
    nb, t, _ = x_prompt.shape
    ns = x_sample.shape[0]
    assert x_sample.shape[1] == 1 and t % WINDOW == 0
    win = cache_k_win.shape[2]
    assert win <= WINDOW
    heads, n = RWKV_HEADS, RWKV_HEAD_DIM

    pad_rows = -nb % SUBLANES
    c_all = jnp.concatenate([c_prompt, jnp.zeros((pad_rows, D_MODEL), F32), c_sample], axis=0)
    mod_all = _ada(c_all, ada_w, ada_b)

    cos_p, sin_p = _rope_tables(jnp.arange(t))
    cos_s, sin_s = _rope_tables(jnp.full((ns,), PAST_LEN))

    xp = x_prompt
    xs = x_sample.reshape(1, ns, D_MODEL)
    tm_p = _row_tile(t, MATMUL_ROWS)
    tm_s = _row_tile(ns, MATMUL_ROWS)
    vf_p = vf_s = None
    st_p, st_s = [], []
    lw = dict(norm2_g=norm2_g, w_attn_out=w_attn_out, w_pool_out=w_pool_out, w_rwkv_out=w_rwkv_out, w_o=w_o,
              w_ffn_gate=w_ffn_gate, w_ffn_up=w_ffn_up,
              w_ffn_down=w_ffn_down.astype(BF16))
    w_in_t = w_in.transpose(0, 2, 1)
    to_feature_major = lambda c: c.transpose(0, 1, 3, 4, 2).reshape(depth, ns, KV_WIDTH, win)
    from_feature_major = lambda c: c.reshape(ns, N_KV_HEADS, HEAD_DIM, win).transpose(0, 3, 1, 2)
    cache_kt, cache_vt = to_feature_major(cache_k_win), to_feature_major(cache_v_win)
    state_t = state_wkv.transpose(0, 2, 3, 4, 1)
    for l in range(depth):
        rp = _rwkv_params(l, rwkv_mu, rwkv_w0, rwkv_w2, rwkv_a0, rwkv_a2, rwkv_g2, rwkv_k_k, rwkv_k_a, rwkv_r_k,
                          rwkv_lnx_g, rwkv_lnx_b, vres_w1, vres_w2, vres_b)
        g1 = norm1_g[l].reshape(1, -1)
        mod_p = mod_all[l, :nb].reshape(nb, 1, -1)
        mod_s = mod_all[l, nb + pad_rows:].reshape(1, ns, -1)

        z = _in_proj(xp, g1, mod_p, w_in_t, l, cos_p, sin_p, tm_p)
        o_attn = _attn_prompt(z, attn_sink[l])
        z_pool = _pool_prompt(z, pool_w[l], pool_scale[l].reshape(1, -1), _row_tile(t, 512))
        r, w, k, v, a, b, g, bonus = _rwkv_prep(z, None, vf_p, rp, _row_tile(t, 256))
        if l == 0:
            vf_p = v
        y, s_pair = _wkv_scan(r, w, k, v, a, b, _row_tile(t, 256))
        wkv_new = s_pair.reshape(nb, heads // 2, n, 2, n).transpose(0, 1, 3, 2, 4).reshape(nb, heads, n, n)
        tail = z[:, t - WINDOW:, :GL_OFF].astype(F32)
        st_p.append((tail[:, :, K_OFF:V_OFF].reshape(nb, WINDOW, N_KV_HEADS, HEAD_DIM),
                     tail[:, :, V_OFF:U_OFF].reshape(nb, WINDOW, N_KV_HEADS, HEAD_DIM),
                     tail[:, WINDOW - POOL_PAD:, U_OFF:PR_OFF],
                     tail[:, WINDOW - 1, PR_OFF:PR_OFF + RWKV_PROJ],
                     wkv_new))
        xp = _layer_common(xp, mod_p, z, o_attn, z_pool, y, bonus, g, lw, l, rp, tm_p)

        z = _in_proj(xs, g1, mod_s, w_in_t, l, cos_s, sin_s, tm_s)
        zs = z[0, :, :GL_OFF].astype(F32)
        k_new, v_new = zs[:, K_OFF:V_OFF], zs[:, V_OFF:U_OFF]
        o3, kt_new, vt_new = _attn_sample(zs[:, :ATTN_WIDTH].reshape(ns, N_Q_HEADS, HEAD_DIM), k_new, v_new,
                                          cache_kt, cache_vt, l, attn_sink[l])
        o_attn = o3.reshape(1, ns, ATTN_WIDTH).astype(BF16)
        z_pool = _pool_sample(z, state_pool[l].transpose(1, 0, 2), pool_w[l], pool_scale[l].reshape(1, -1))
        prev = _pad_cols(state_shift[l], RWKV_PROJ_PAD).reshape(1, ns, RWKV_PROJ_PAD)
        r, w, k, v, a, b, g, bonus = _rwkv_prep(z, prev, vf_s, rp, tm_s)
        if l == 0:
            vf_s = v
        step_in = jnp.stack([r, w, k, v, a, b]).reshape(6, ns, heads, n).transpose(0, 2, 3, 1)
        y_t, wkv_t = _wkv_step(step_in, state_t, l)
        y = y_t.transpose(2, 0, 1).reshape(1, ns, RWKV_WIDTH)
        u_new = zs[:, None, U_OFF:PR_OFF]
        st_s.append((from_feature_major(kt_new), from_feature_major(vt_new),
                     jnp.concatenate([state_pool[l][:, 1:], u_new], axis=1),
                     zs[:, PR_OFF:PR_OFF + RWKV_PROJ],
                     wkv_t.transpose(3, 0, 1, 2)))
        xs = _layer_common(xs, mod_s, z, o_attn, z_pool, y, bonus, g, lw, l, rp, tm_s)

    g_fin = final_norm_g.reshape(1, -1)
    y_prompt = _final_norm(xp, g_fin, _row_tile(t, NORM_ROWS))
    y_sample = _final_norm(xs, g_fin, _row_tile(ns, NORM_ROWS)).reshape(ns, 1, D_MODEL)
    stack = lambda states, i: jnp.stack([s[i] for s in states])
    return (y_prompt, y_sample) + tuple(stack(st_p, i) for i in range(5)) + tuple(stack(st_s, i) for i in range(5))
```

```python
import functools

import numpy as np
import jax
import jax.numpy as jnp
from jax import lax
from jax.experimental import pallas as pl
from jax.experimental.pallas import tpu as pltpu

F32 = jnp.float32
BF16 = jnp.bfloat16

D_MODEL = 2048
HEAD_DIM = 64
N_Q_HEADS = 16
N_KV_HEADS = 4
GROUP = N_Q_HEADS // N_KV_HEADS
ATTN_WIDTH = N_Q_HEADS * HEAD_DIM
KV_WIDTH = N_KV_HEADS * HEAD_DIM
WINDOW = 128
ROPE_THETA = 10000.0
POOL_WIDTH = 512
POOL_WINDOWS = (2, 4, 8, 16)
POOL_GROUP_WIDTH = 128
POOL_PAD = 15
RWKV_HEAD_DIM = 64
RWKV_WIDTH = 512
RWKV_HEADS = 8
DECAY_LORA = 96
ICLR_LORA = 96
GATE_LORA = 256
VRES_LORA = 32
RWKV_PROJ = 3 * RWKV_WIDTH + DECAY_LORA + ICLR_LORA + GATE_LORA
RWKV_PROJ_PAD = 2048
D_FF = 5632
PAST_LEN = 8192
RMS_EPS = 1e-6
LNX_EPS = 64e-5
NEG_INF = -1e30

K_OFF = ATTN_WIDTH
V_OFF = K_OFF + KV_WIDTH
U_OFF = V_OFF + KV_WIDTH
PR_OFF = U_OFF + POOL_WIDTH
GL_OFF = PR_OFF + RWKV_PROJ_PAD
IN_PAD = GL_OFF + 3 * D_MODEL
ROPE_END = V_OFF

LANES = 128
SUBLANES = 8
VMEM_LIMIT = 56 * 2**20

MATMUL_ROWS = 2048
NORM_ROWS = 1024
ELEMENTWISE_ROWS = 512
RWKV_ROWS = 256
ADA_TN = 1024
W_O_TN = 512
FFN_DOWN_TN = 512
SAMPLE_ATTN_ROWS = 8


def _cparams(*sem):
    return pltpu.CompilerParams(dimension_semantics=sem, vmem_limit_bytes=VMEM_LIMIT)


def _row_tile(t, pref):
    tm = min(t, pref)
    while t % tm:
        tm -= SUBLANES
    return tm


def _sigmoid(x):
    return 0.5 + 0.5 * jnp.tanh(0.5 * x)


def _mod_norm(x, g, sh, sc):
    ms = jnp.mean(x * x, axis=-1, keepdims=True)
    return (x * lax.rsqrt(ms + RMS_EPS) * g) * (1.0 + sc) + sh


NORM_CHUNKS = 4


def _row_chunks(tm, n):
    size = tm // n
    return [slice(c * size, (c + 1) * size) for c in range(n)]


def _rows_of(mod_ref, rows):
    return mod_ref[...] if mod_ref.shape[0] == 1 else mod_ref[rows, :]


def _bdot(a, b):
    return jnp.dot(a.astype(BF16), b.astype(BF16), preferred_element_type=F32)


def _seg_sum(x, ones_bd):
    hi = x.astype(BF16)
    lo = (x - hi.astype(F32)).astype(BF16)
    return (jnp.dot(hi, ones_bd, preferred_element_type=F32)
            + jnp.dot(lo, ones_bd, preferred_element_type=F32))


def _mod_spec(mod, tm, chunk):
    per_row = mod.shape[1] != 1
    rows = tm if per_row else 1
    return pl.BlockSpec((None, rows, D_MODEL), lambda g, i, j=0: (g, i if per_row else 0, chunk))


def _ada_kernel(c_ref, w_ref, b_ref, o_ref):
    c = c_ref[...]
    o_ref[...] = _bdot(c * _sigmoid(c), w_ref[...]) + b_ref[...]


def _ada(c_all, ada_w, ada_b):
    depth, _, width = ada_w.shape
    rows = c_all.shape[0]
    tn = ADA_TN
    return pl.pallas_call(
        _ada_kernel,
        grid=(depth, width // tn),
        in_specs=[pl.BlockSpec((rows, D_MODEL), lambda l, j: (0, 0)),
                  pl.BlockSpec((None, D_MODEL, tn), lambda l, j: (l, 0, j)),
                  pl.BlockSpec((None, 1, tn), lambda l, j: (l, 0, j))],
        out_specs=pl.BlockSpec((None, rows, tn), lambda l, j: (l, 0, j)),
        out_shape=jax.ShapeDtypeStruct((depth, rows, width), F32),
        compiler_params=_cparams("parallel", "parallel"),
        name="ada",
    )(c_all, ada_w, ada_b.reshape(depth, 1, width))


IN_TN = 512
IN_MAIN_TILES = GL_OFF // IN_TN


def _in_proj_kernel(x_ref, g_ref, sh_ref, sc_ref, cos_ref, sin_ref, w_ref, o_ref, h_ref):
    j = pl.program_id(2)
    tm = x_ref.shape[0]

    def project(rows):
        return lax.dot_general(h_ref[rows, :], w_ref[0].astype(BF16), (((1,), (1,)), ((), ())),
                               preferred_element_type=F32)

    def rope_tile(rows, whole_tile):
        acc = project(rows)
        reps = IN_TN // LANES
        cos = jnp.concatenate([cos_ref[rows, :]] * reps, axis=1)
        sin = jnp.concatenate([sin_ref[rows, :]] * reps, axis=1)
        lane = lax.broadcasted_iota(jnp.int32, acc.shape, 1)
        first_half = (lane % HEAD_DIM) < (HEAD_DIM // 2)
        rot = jnp.where(first_half, pltpu.roll(acc, IN_TN - HEAD_DIM // 2, 1),
                        pltpu.roll(acc, HEAD_DIM // 2, 1))
        roped = acc * cos + rot * sin
        if not whole_tile:
            roped = jnp.where(j * IN_TN + lane < ROPE_END, roped, acc)
        o_ref[rows, :] = roped.astype(o_ref.dtype)

    assert IN_TN <= ROPE_END

    @pl.when(j == 0)
    def _():
        for rows in _row_chunks(tm, NORM_CHUNKS):
            h_ref[rows, :] = _mod_norm(x_ref[rows, :], g_ref[...], _rows_of(sh_ref, rows),
                                       _rows_of(sc_ref, rows)).astype(BF16)
            rope_tile(rows, True)

    @pl.when((j > 0) & ((j + 1) * IN_TN <= ROPE_END))
    def _():
        rope_tile(slice(None), True)

    @pl.when((j * IN_TN < ROPE_END) & ((j + 1) * IN_TN > ROPE_END))
    def _():
        rope_tile(slice(None), False)

    @pl.when(j * IN_TN >= ROPE_END)
    def _():
        o_ref[...] = project(slice(None)).astype(o_ref.dtype)


def _in_proj(x, g1, mod, w_in_t, l, cos, sin, tm):
    groups, t, _ = x.shape
    gate_shift = RWKV_PROJ_PAD - RWKV_PROJ

    def weight_rows(g, i, j):
        start = jnp.where(j < IN_MAIN_TILES, j * IN_TN, j * IN_TN - gate_shift)
        return (l, pl.multiple_of(start, SUBLANES), 0)

    return pl.pallas_call(
        _in_proj_kernel,
        grid=(groups, t // tm, IN_PAD // IN_TN),
        in_specs=[pl.BlockSpec((None, tm, D_MODEL), lambda g, i, j: (g, i, 0), pipeline_mode=pl.Buffered(1)),
                  pl.BlockSpec((1, D_MODEL), lambda g, i, j: (0, 0)),
                  _mod_spec(mod, tm, 0), _mod_spec(mod, tm, 1),
                  pl.BlockSpec((tm, LANES), lambda g, i, j: (i, 0), pipeline_mode=pl.Buffered(1)),
                  pl.BlockSpec((tm, LANES), lambda g, i, j: (i, 0), pipeline_mode=pl.Buffered(1)),
                  pl.BlockSpec((pl.Element(1), pl.Element(IN_TN), pl.Element(D_MODEL)), weight_rows)],
        out_specs=pl.BlockSpec((None, tm, IN_TN), lambda g, i, j: (g, i, j)),
        out_shape=jax.ShapeDtypeStruct((groups, t, IN_PAD), BF16),
        scratch_shapes=[pltpu.VMEM((tm, D_MODEL), BF16)],
        compiler_params=_cparams("parallel", "parallel", "arbitrary"),
        name="in_proj",
    )(x, g1, mod, mod, cos, sin, w_in_t)


def _rope_tables(pos):
    half = HEAD_DIM // 2
    inv = ROPE_THETA ** (-jnp.arange(half, dtype=F32) * 2.0 / HEAD_DIM)
    ang = pos.astype(F32)[:, None] * inv[None, :]
    cos, sin = jnp.cos(ang), jnp.sin(ang)
    reps = LANES // HEAD_DIM
    cos_t = jnp.concatenate([cos, cos] * reps, axis=1)
    sin_t = jnp.concatenate([-sin, sin] * reps, axis=1)
    return cos_t, sin_t


def _attn_prompt_kernel(sink_ref, q_ref, k_ref, v_ref, kp_ref, vp_ref, o_ref):
    n = pl.program_id(1)
    blk = q_ref.shape[0]
    q = q_ref[...]
    k2 = jnp.concatenate([kp_ref[...], k_ref[...]], axis=0).astype(BF16)
    v2 = jnp.concatenate([vp_ref[...], v_ref[...]], axis=0).astype(BF16)
    qi = lax.broadcasted_iota(jnp.int32, (blk, 2 * blk), 0)
    kc = lax.broadcasted_iota(jnp.int32, (blk, 2 * blk), 1)
    rel = qi - kc + blk
    first_key = jnp.where(n > 0, 0, blk)
    mask = (rel >= 0) & (rel <= WINDOW) & (kc >= first_key)
    outs = []
    for hd in range(N_Q_HEADS):
        h = hd // GROUP
        kh = k2[:, h * HEAD_DIM:(h + 1) * HEAD_DIM]
        vh = v2[:, h * HEAD_DIM:(h + 1) * HEAD_DIM]
        qh = q[:, hd * HEAD_DIM:(hd + 1) * HEAD_DIM].astype(BF16)
        s = lax.dot_general(qh, kh, (((1,), (1,)), ((), ())), preferred_element_type=F32) * (HEAD_DIM ** -0.5)
        s = jnp.where(mask, s, NEG_INF)
        sk = sink_ref[hd]
        m = jnp.maximum(jnp.max(s, axis=1, keepdims=True), sk)
        p = jnp.exp(s - m)
        den = jnp.sum(p, axis=1, keepdims=True) + jnp.exp(sk - m)
        outs.append(jnp.dot(p.astype(BF16), vh, preferred_element_type=F32) / den)
    o_ref[...] = jnp.concatenate(outs, axis=1).astype(o_ref.dtype)


def _attn_prompt(z, sink):
    groups, t, _ = z.shape
    blk = WINDOW
    kcol, vcol = K_OFF // KV_WIDTH, V_OFF // KV_WIDTH
    prev = lambda col: (lambda g, n: (g, jnp.maximum(n - 1, 0), col))
    return pl.pallas_call(
        _attn_prompt_kernel,
        grid=(groups, t // blk),
        in_specs=[pl.BlockSpec(memory_space=pltpu.SMEM),
                  pl.BlockSpec((None, blk, ATTN_WIDTH), lambda g, n: (g, n, 0)),
                  pl.BlockSpec((None, blk, KV_WIDTH), lambda g, n: (g, n, kcol)),
                  pl.BlockSpec((None, blk, KV_WIDTH), lambda g, n: (g, n, vcol)),
                  pl.BlockSpec((None, blk, KV_WIDTH), prev(kcol)),
                  pl.BlockSpec((None, blk, KV_WIDTH), prev(vcol))],
        out_specs=pl.BlockSpec((None, blk, ATTN_WIDTH), lambda g, n: (g, n, 0)),
        out_shape=jax.ShapeDtypeStruct((groups, t, ATTN_WIDTH), BF16),
        compiler_params=_cparams("parallel", "parallel"),
        name="attn_prompt",
    )(sink, z, z, z, z, z)


def _attn_sample_kernel(sink_ref, q_ref, kn_ref, vn_ref, knc_ref, vnc_ref, kt_ref, vt_ref, o_ref, kto_ref, vto_ref):
    bb = q_ref.shape[0]
    win = kt_ref.shape[2]
    row = lax.broadcasted_iota(jnp.int32, (N_Q_HEADS, KV_WIDTH), 0)
    lane = lax.broadcasted_iota(jnp.int32, (N_Q_HEADS, KV_WIDTH), 1)
    own = (row // GROUP) == (lane // HEAD_DIM)
    newest = lax.broadcasted_iota(jnp.int32, (KV_WIDTH, win), 1) == win - 1
    sink = sink_ref[...]
    scale = HEAD_DIM ** -0.5

    rows = range(bb)
    qbd = [jnp.where(own, jnp.concatenate([q_ref[b]] * N_KV_HEADS, axis=1), 0.0) for b in rows]
    s = [jnp.dot(qbd[b].astype(BF16), kt_ref[b].astype(BF16), preferred_element_type=F32) * scale for b in rows]
    s_new = [jnp.sum(qbd[b] * kn_ref[b], axis=1, keepdims=True) * scale for b in rows]
    m = [jnp.maximum(jnp.maximum(jnp.max(s[b], axis=1, keepdims=True), s_new[b]), sink) for b in rows]
    p = [jnp.exp(s[b] - m[b]) for b in rows]
    p_new = [jnp.exp(s_new[b] - m[b]) for b in rows]
    den = [jnp.sum(p[b], axis=1, keepdims=True) + p_new[b] + jnp.exp(sink - m[b]) for b in rows]
    o = [lax.dot_general(p[b].astype(BF16), vt_ref[b].astype(BF16), (((1,), (1,)), ((), ())),
                         preferred_element_type=F32) + p_new[b] * vn_ref[b] for b in rows]
    for b in rows:
        ob = jnp.where(own, o[b], 0.0)
        o64 = ob[:, 0:HEAD_DIM]
        for h in range(1, N_KV_HEADS):
            o64 = o64 + ob[:, h * HEAD_DIM:(h + 1) * HEAD_DIM]
        o_ref[b] = o64 / den[b]
        kto_ref[b] = jnp.where(newest, knc_ref[b], pltpu.roll(kt_ref[b], win - 1, 1))
        vto_ref[b] = jnp.where(newest, vnc_ref[b], pltpu.roll(vt_ref[b], win - 1, 1))


def _attn_sample(q3, k_new, v_new, kt_all, vt_all, l, sink):
    _, nb, _, win = kt_all.shape
    assert win == LANES
    bb = _row_tile(nb, SAMPLE_ATTN_ROWS)
    row = pl.BlockSpec((bb, 1, KV_WIDTH), lambda i: (i, 0, 0))
    col = pl.BlockSpec((bb, KV_WIDTH, 1), lambda i: (i, 0, 0))
    cache_in = pl.BlockSpec((None, bb, KV_WIDTH, win), lambda i: (l, i, 0, 0))
    cache_out = pl.BlockSpec((bb, KV_WIDTH, win), lambda i: (i, 0, 0))
    heads = pl.BlockSpec((bb, N_Q_HEADS, HEAD_DIM), lambda i: (i, 0, 0))
    return pl.pallas_call(
        _attn_sample_kernel,
        grid=(nb // bb,),
        in_specs=[pl.BlockSpec((N_Q_HEADS, 1), lambda i: (0, 0)), heads, row, row, col, col, cache_in, cache_in],
        out_specs=[heads, cache_out, cache_out],
        out_shape=[jax.ShapeDtypeStruct((nb, N_Q_HEADS, HEAD_DIM), F32),
                   jax.ShapeDtypeStruct((nb, KV_WIDTH, win), F32),
                   jax.ShapeDtypeStruct((nb, KV_WIDTH, win), F32)],
        compiler_params=_cparams("parallel"),
        name="attn_sample",
    )(sink.reshape(N_Q_HEADS, 1), q3, k_new.reshape(nb, 1, KV_WIDTH), v_new.reshape(nb, 1, KV_WIDTH),
      k_new.reshape(nb, KV_WIDTH, 1), v_new.reshape(nb, KV_WIDTH, 1), kt_all, vt_all)


POOL_HALO = 16


def _pool_prompt_kernel(u_ref, halo_ref, w_ref, sc_ref, o_ref, ext_ref):
    i = pl.program_id(1)
    tm = u_ref.shape[0]
    ext_ref[0:POOL_HALO, :] = jnp.where(i > 0, halo_ref[...].astype(F32), 0.0)
    ext_ref[POOL_HALO:POOL_HALO + tm, :] = u_ref[...].astype(F32)
    pos = i * tm + lax.broadcasted_iota(jnp.int32, (tm, 1), 0)
    for gi, w in enumerate(POOL_WINDOWS):
        cols = slice(gi * POOL_GROUP_WIDTH, (gi + 1) * POOL_GROUP_WIDTH)
        u = ext_ref[POOL_HALO:POOL_HALO + tm, cols]
        acc = u
        for back in range(1, w):
            acc = acc + ext_ref[POOL_HALO - back:POOL_HALO - back + tm, cols]
        cnt = jnp.minimum(pos + 1, w).astype(F32)
        d = acc / cnt - u
        o_ref[:, cols] = _bdot(d, w_ref[gi]) * sc_ref[:, cols]


def _pool_prompt(z, pool_w, pool_scale, tm):
    groups, t, _ = z.shape
    ucol = U_OFF // POOL_WIDTH
    per = tm // POOL_HALO
    return pl.pallas_call(
        _pool_prompt_kernel,
        grid=(groups, t // tm),
        in_specs=[pl.BlockSpec((None, tm, POOL_WIDTH), lambda g, i: (g, i, ucol)),
                  pl.BlockSpec((None, POOL_HALO, POOL_WIDTH), lambda g, i: (g, jnp.maximum(i * per - 1, 0), ucol)),
                  pl.BlockSpec((len(POOL_WINDOWS), POOL_GROUP_WIDTH, POOL_GROUP_WIDTH), lambda g, i: (0, 0, 0)),
                  pl.BlockSpec((1, POOL_WIDTH), lambda g, i: (0, 0))],
        out_specs=pl.BlockSpec((None, tm, POOL_WIDTH), lambda g, i: (g, i, 0)),
        out_shape=jax.ShapeDtypeStruct((groups, t, POOL_WIDTH), F32),
        scratch_shapes=[pltpu.VMEM((tm + POOL_HALO, POOL_WIDTH), F32)],
        compiler_params=_cparams("parallel", "parallel"),
        name="pool_prompt",
    )(z, z, pool_w, pool_scale)


def _pool_sample_kernel(u_ref, st_ref, w_ref, sc_ref, o_ref):
    for gi, w in enumerate(POOL_WINDOWS):
        cols = slice(gi * POOL_GROUP_WIDTH, (gi + 1) * POOL_GROUP_WIDTH)
        u = u_ref[:, cols].astype(F32)
        acc = u
        for back in range(1, w):
            acc = acc + st_ref[POOL_PAD - back, :, cols]
        cnt = float(min(PAST_LEN + 1, w))
        d = acc / cnt - u
        o_ref[:, cols] = _bdot(d, w_ref[gi]) * sc_ref[:, cols]


def _pool_sample(z, state_t, pool_w, pool_scale):
    rows = z.shape[1]
    ucol = U_OFF // POOL_WIDTH
    return pl.pallas_call(
        _pool_sample_kernel,
        grid=(1,),
        in_specs=[pl.BlockSpec((None, rows, POOL_WIDTH), lambda i: (0, 0, ucol)),
                  pl.BlockSpec((POOL_PAD, rows, POOL_WIDTH), lambda i: (0, 0, 0)),
                  pl.BlockSpec((len(POOL_WINDOWS), POOL_GROUP_WIDTH, POOL_GROUP_WIDTH), lambda i: (0, 0, 0)),
                  pl.BlockSpec((1, POOL_WIDTH), lambda i: (0, 0))],
        out_specs=pl.BlockSpec((None, rows, POOL_WIDTH), lambda i: (0, 0, 0)),
        out_shape=jax.ShapeDtypeStruct((1, rows, POOL_WIDTH), F32),
        compiler_params=_cparams("arbitrary"),
        name="pool_sample",
    )(z, state_t, pool_w, pool_scale)


PREP_HALO = 16


def _rwkv_prep_kernel(*refs, halo, has_vres):
    it = iter(refs)
    pr_ref, prev_ref = next(it), next(it)
    vf_ref = next(it) if has_vres else None
    mu_ref, w0_ref, a0_ref, kk_ref, ka_ref, rk_ref = (next(it) for _ in range(6))
    w2_ref, a2_ref, g2_ref, ones_ref = (next(it) for _ in range(4))
    if has_vres:
        vw1_ref, vw2_ref, vb_ref = next(it), next(it), next(it)
    r_o, w_o, k_o, v_o, a_o, b_o, g_o, bonus_o = (next(it) for _ in range(8))
    ext_ref = next(it) if halo else None

    pr = pr_ref[...].astype(F32)
    tm = pr.shape[0]
    if halo:
        i = pl.program_id(1)
        ext_ref[0:PREP_HALO, :] = jnp.where(i > 0, prev_ref[...].astype(F32), 0.0)
        ext_ref[PREP_HALO:PREP_HALO + tm, :] = pr
        prev = ext_ref[PREP_HALO - 1:PREP_HALO - 1 + tm, :]
    else:
        prev = prev_ref[...]
    xm = pr + (prev - pr) * mu_ref[...]
    wd = RWKV_WIDTH
    r, k, v, lora_in = xm[:, 0:wd], xm[:, wd:2 * wd], xm[:, 2 * wd:3 * wd], xm[:, 3 * wd:4 * wd]
    ones_bd = ones_ref[...]
    lw = _bdot(jnp.tanh(lora_in), w2_ref[...])
    la = _bdot(lora_in, a2_ref[...])
    g = _bdot(_sigmoid(lora_in), g2_ref[...])
    y = -(w0_ref[...] + lw)
    softplus = jnp.maximum(y, 0.0) + jnp.log(1.0 + jnp.exp(-jnp.abs(y)))
    decay = jnp.exp(-jnp.exp(-softplus - 0.5))
    a = _sigmoid(a0_ref[...] + la)
    if has_vres:
        t2 = _bdot(_bdot(v, vw1_ref[...]), vw2_ref[...])
        v = v + (vf_ref[...] - v) * _sigmoid(vb_ref[...] + t2)
    kk = k * kk_ref[...]
    kk = kk / jnp.maximum(jnp.sqrt(_seg_sum(kk * kk, ones_bd)), 1e-12)
    k = k * (1.0 + (a - 1.0) * ka_ref[...])
    r_o[...] = r
    w_o[...] = decay
    k_o[...] = k
    v_o[...] = v
    a_o[...] = -kk
    b_o[...] = kk * a
    g_o[...] = g
    bonus_o[...] = _seg_sum(r * k * rk_ref[...], ones_bd) * v


def _rwkv_prep(z, prev, v_first, rp, tm):
    groups, t, _ = z.shape
    halo = prev is None
    has_vres = v_first is not None
    prcol = PR_OFF // RWKV_PROJ_PAD
    wd = RWKV_WIDTH
    tok = lambda width: pl.BlockSpec((None, tm, width), lambda g, i: (g, i, 0))
    vec = lambda width: pl.BlockSpec((1, width), lambda g, i: (0, 0))
    mat = lambda a, b: pl.BlockSpec((a, b), lambda g, i: (0, 0))
    args = [z]
    in_specs = [pl.BlockSpec((None, tm, RWKV_PROJ_PAD), lambda g, i: (g, i, prcol))]
    if halo:
        per = tm // PREP_HALO
        args.append(z)
        in_specs.append(pl.BlockSpec((None, PREP_HALO, RWKV_PROJ_PAD),
                                     lambda g, i: (g, jnp.maximum(i * per - 1, 0), prcol)))
    else:
        args.append(prev)
        in_specs.append(tok(RWKV_PROJ_PAD))
    if has_vres:
        args.append(v_first)
        in_specs.append(tok(wd))
    args += [rp["mu"], rp["w0"], rp["a0"], rp["k_k"], rp["k_a"], rp["r_k"],
             rp["w2"], rp["a2"], rp["g2"], rp["ones_bd"]]
    in_specs += [vec(RWKV_PROJ_PAD)] + [vec(wd)] * 5 + [mat(wd, wd)] * 4
    if has_vres:
        args += [rp["vw1"], rp["vw2"], rp["vb"]]
        in_specs += [mat(wd, LANES), mat(LANES, wd), vec(wd)]
    return pl.pallas_call(
        functools.partial(_rwkv_prep_kernel, halo=halo, has_vres=has_vres),
        grid=(groups, t // tm),
        in_specs=in_specs,
        out_specs=[tok(wd)] * 8,
        out_shape=[jax.ShapeDtypeStruct((groups, t, wd), F32)] * 8,
        scratch_shapes=[pltpu.VMEM((tm + PREP_HALO, RWKV_PROJ_PAD), F32)] if halo else [],
        compiler_params=_cparams("parallel", "parallel"),
        name="rwkv_prep",
    )(*args)


SCAN_STEPS_PER_TRIP = 64


def _wkv_scan_kernel(r_ref, w_ref, k_ref, v_ref, a_ref, b_ref, y_ref, s_ref):
    nb, tc, _ = r_ref.shape
    pairs = RWKV_HEADS // 2
    n = RWKV_HEAD_DIM

    @pl.when(pl.program_id(0) == 0)
    def _():
        s_ref[...] = jnp.zeros(s_ref.shape, F32)

    row = lax.broadcasted_iota(jnp.int32, (n, 2 * n), 0)
    lane = lax.broadcasted_iota(jnp.int32, (n, 2 * n), 1)
    diag = (lane % n) == row
    chains = [(bi, slice(p * 2 * n, (p + 1) * 2 * n)) for bi in range(nb) for p in range(pairs)]
    nc = len(chains)
    seg_r = lax.broadcasted_iota(jnp.int32, (2 * n, 2 * n), 0) // n
    seg_c = lax.broadcasted_iota(jnp.int32, (2 * n, 2 * n), 1) // n
    ones_bd = (seg_r == seg_c).astype(BF16)
    stack = lambda xs: jnp.concatenate(xs, axis=0)

    def seg_bcast(x):
        return jnp.dot(x.astype(BF16), ones_bd, preferred_element_type=F32)

    span = SCAN_STEPS_PER_TRIP

    def steps(trip, carry):
        base = pl.multiple_of(trip * span, span)
        rows = pl.ds(base, span)
        tiles = [[ref[bi, rows, sl] for ref in (r_ref, w_ref, k_ref, v_ref, a_ref, b_ref)] for bi, sl in chains]
        s = [s_ref[c] for c in range(nc)]
        ys = [[] for _ in chains]

        def y_rows(ycol):
            for c in range(nc):
                ys[c].append(jnp.sum(jnp.where(diag, ycol[c * n:(c + 1) * n], 0.0), axis=0, keepdims=True))

        vcol = seg_bcast(stack([jnp.where(diag, t[3][slice(u, u + 1)], 0.0) for u in range(span) for t in tiles]))
        sr = None
        for u in range(span):
            one = slice(u, u + 1)
            red = seg_bcast(stack([s[c] * tiles[c][4][one] for c in range(nc)] + (sr or [])))
            if sr:
                y_rows(red[nc * n:])
            for c, (r8, w8, k8, v8, a8, b8) in enumerate(tiles):
                s[c] = (s[c] * w8[one] + red[c * n:(c + 1) * n] * b8[one]
                        + vcol[(u * nc + c) * n:(u * nc + c + 1) * n] * k8[one])
            sr = [s[c] * tiles[c][0][one] for c in range(nc)]
        y_rows(seg_bcast(stack(sr)))
        for c, (bi, sl) in enumerate(chains):
            s_ref[c] = s[c]
            y_ref[bi, rows, sl] = jnp.concatenate(ys[c], axis=0)
        return carry

    lax.fori_loop(0, tc // span, steps, 0)


def _wkv_scan(r, w, k, v, a, b, tc):
    nb, t, wd = r.shape
    pairs = RWKV_HEADS // 2
    tok = pl.BlockSpec((nb, tc, wd), lambda i: (0, i, 0))
    return pl.pallas_call(
        _wkv_scan_kernel,
        grid=(t // tc,),
        in_specs=[tok] * 6,
        out_specs=[tok, pl.BlockSpec((nb * pairs, RWKV_HEAD_DIM, 2 * RWKV_HEAD_DIM), lambda i: (0, 0, 0))],
        out_shape=[jax.ShapeDtypeStruct((nb, t, wd), F32),
                   jax.ShapeDtypeStruct((nb * pairs, RWKV_HEAD_DIM, 2 * RWKV_HEAD_DIM), F32)],
        compiler_params=_cparams("arbitrary"),
        name="wkv_scan",
    )(r, w, k, v, a, b)


def _wkv_step_kernel(x_ref, s_ref, y_ref, so_ref):
    n = RWKV_HEAD_DIM
    r, w, k, v, a, b = (x_ref[q] for q in range(6))
    for i0 in range(0, n, SUBLANES):
        rows = range(i0, i0 + SUBLANES)
        s = [s_ref[i] for i in rows]
        sa = [jnp.sum(si * a, axis=0, keepdims=True) for si in s]
        s = [si * w + sai * b + v[i:i + 1] * k for si, sai, i in zip(s, sa, rows)]
        for si, i in zip(s, rows):
            so_ref[i] = si
        y_ref[i0:i0 + SUBLANES, :] = jnp.concatenate([jnp.sum(si * r, axis=0, keepdims=True) for si in s], axis=0)


def _wkv_step(x, state_t, l):
    _, heads, n, nb = x.shape
    return pl.pallas_call(
        _wkv_step_kernel,
        grid=(heads,),
        in_specs=[pl.BlockSpec((6, None, n, nb), lambda h: (0, h, 0, 0)),
                  pl.BlockSpec((None, None, n, n, nb), lambda h: (l, h, 0, 0, 0))],
        out_specs=[pl.BlockSpec((None, n, nb), lambda h: (h, 0, 0)),
                   pl.BlockSpec((None, n, n, nb), lambda h: (h, 0, 0, 0))],
        out_shape=[jax.ShapeDtypeStruct((heads, n, nb), F32), jax.ShapeDtypeStruct((heads, n, n, nb), F32)],
        compiler_params=_cparams("parallel"),
        name="wkv_step",
    )(x, state_t)


def _rwkv_post_kernel(y_ref, bonus_ref, g_ref, lg_ref, lb_ref, ones_ref, o_ref):
    y = y_ref[...]
    ones_bd = ones_ref[...]
    inv = 1.0 / RWKV_HEAD_DIM
    d = y - _seg_sum(y, ones_bd) * inv
    var = _seg_sum(d * d, ones_bd) * inv
    yn = d * lax.rsqrt(var + LNX_EPS) * lg_ref[...] + lb_ref[...]
    o_ref[...] = ((yn + bonus_ref[...]) * g_ref[...]).astype(o_ref.dtype)


def _rwkv_post(y, bonus, g, rp, tm):
    groups, t, wd = y.shape
    tok = pl.BlockSpec((None, tm, wd), lambda gi, i: (gi, i, 0))
    vec = pl.BlockSpec((1, wd), lambda gi, i: (0, 0))
    return pl.pallas_call(
        _rwkv_post_kernel,
        grid=(groups, t // tm),
        in_specs=[tok, tok, tok, vec, vec, pl.BlockSpec((wd, wd), lambda gi, i: (0, 0))],
        out_specs=tok,
        out_shape=jax.ShapeDtypeStruct((groups, t, wd), BF16),
        compiler_params=_cparams("parallel", "parallel"),
        name="rwkv_post",
    )(y, bonus, g, rp["lnx_g"], rp["lnx_b"], rp["ones_bd"])


MERGE_TN = 512


def _merge_kernel(oa_ref, zp_ref, yr_ref, ga_ref, gp_ref, gr_ref, wa_ref, wp_ref, wr_ref, o_ref):
    gate = lambda ref: _sigmoid(ref[...].astype(F32))
    merged = (gate(ga_ref) * _bdot(oa_ref[...], wa_ref[...])
              + gate(gp_ref) * _bdot(zp_ref[...], wp_ref[...])
              + gate(gr_ref) * _bdot(yr_ref[...], wr_ref[...]))
    o_ref[...] = merged.astype(o_ref.dtype)


def _merge(o_attn, z_pool, y_rwkv, z, w_attn_out, w_pool_out, w_rwkv_out, l, tm):
    groups, t, _ = z.shape
    tn = MERGE_TN
    gate = lambda br: pl.BlockSpec((None, tm, tn), lambda g, i, j: (g, i, (GL_OFF + br * D_MODEL) // tn + j))
    tok = lambda width: pl.BlockSpec((None, tm, width), lambda g, i, j: (g, i, 0))
    wt = lambda rows: pl.BlockSpec((None, rows, tn), lambda g, i, j: (l, 0, j))
    return pl.pallas_call(
        _merge_kernel,
        grid=(groups, t // tm, D_MODEL // tn),
        in_specs=[tok(ATTN_WIDTH), tok(POOL_WIDTH), tok(RWKV_WIDTH), gate(0), gate(1), gate(2),
                  wt(ATTN_WIDTH), wt(POOL_WIDTH), wt(RWKV_WIDTH)],
        out_specs=pl.BlockSpec((None, tm, tn), lambda g, i, j: (g, i, j)),
        out_shape=jax.ShapeDtypeStruct((groups, t, D_MODEL), BF16),
        compiler_params=_cparams("parallel", "parallel", "arbitrary"),
        name="merge",
    )(o_attn, z_pool, y_rwkv, z, z, z, w_attn_out, w_pool_out, w_rwkv_out)


def _proj_residual_kernel(a_ref, w_ref, x_ref, gt_ref, o_ref):
    o_ref[...] = x_ref[...] + (1.0 + gt_ref[...]) * _bdot(a_ref[...], w_ref[...])


def _proj_residual(a, w, l, x, mod, chunk, tm, tn):
    groups, t, kdim = a.shape
    per_row = mod.shape[1] != 1
    gate = pl.BlockSpec((None, tm if per_row else 1, tn),
                        lambda g, i, j: (g, i if per_row else 0, chunk * (D_MODEL // tn) + j))
    return pl.pallas_call(
        _proj_residual_kernel,
        grid=(groups, t // tm, D_MODEL // tn),
        in_specs=[pl.BlockSpec((None, tm, kdim), lambda g, i, j: (g, i, 0)),
                  pl.BlockSpec((None, kdim, tn), lambda g, i, j: (l, 0, j)),
                  pl.BlockSpec((None, tm, tn), lambda g, i, j: (g, i, j)),
                  gate],
        out_specs=pl.BlockSpec((None, tm, tn), lambda g, i, j: (g, i, j)),
        out_shape=jax.ShapeDtypeStruct((groups, t, D_MODEL), F32),
        compiler_params=_cparams("parallel", "parallel", "arbitrary"),
        name="proj_residual",
    )(a, w, x, mod)


FFN_TN = 256


def _ffn_up_kernel(x_ref, g_ref, sh_ref, sc_ref, wg_ref, wu_ref, o_ref, h_ref):
    j = pl.program_id(2)
    tm = x_ref.shape[0]

    def swiglu(rows):
        h = h_ref[rows, :]
        gate = jnp.dot(h, wg_ref[...].astype(BF16), preferred_element_type=F32)
        up = jnp.dot(h, wu_ref[...].astype(BF16), preferred_element_type=F32)
        o_ref[rows, :] = (gate * _sigmoid(gate) * up).astype(o_ref.dtype)

    @pl.when(j == 0)
    def _():
        for rows in _row_chunks(tm, NORM_CHUNKS):
            h_ref[rows, :] = _mod_norm(x_ref[rows, :], g_ref[...], _rows_of(sh_ref, rows),
                                       _rows_of(sc_ref, rows)).astype(BF16)
            swiglu(rows)

    @pl.when(j > 0)
    def _():
        swiglu(slice(None))


def _ffn_up(x, g2, mod, w_gate, w_up, l, tm):
    groups, t, _ = x.shape
    tn = FFN_TN
    wt = pl.BlockSpec((None, D_MODEL, tn), lambda g, i, j: (l, 0, j))
    return pl.pallas_call(
        _ffn_up_kernel,
        grid=(groups, t // tm, D_FF // tn),
        in_specs=[pl.BlockSpec((None, tm, D_MODEL), lambda g, i, j: (g, i, 0), pipeline_mode=pl.Buffered(1)),
                  pl.BlockSpec((1, D_MODEL), lambda g, i, j: (0, 0)),
                  _mod_spec(mod, tm, 3), _mod_spec(mod, tm, 4), wt, wt],
        out_specs=pl.BlockSpec((None, tm, tn), lambda g, i, j: (g, i, j)),
        out_shape=jax.ShapeDtypeStruct((groups, t, D_FF), BF16),
        scratch_shapes=[pltpu.VMEM((tm, D_MODEL), BF16)],
        compiler_params=_cparams("parallel", "parallel", "arbitrary"),
        name="ffn_up",
    )(x, g2, mod, mod, w_gate, w_up)


def _final_norm_kernel(x_ref, g_ref, o_ref):
    x = x_ref[...]
    ms = jnp.mean(x * x, axis=-1, keepdims=True)
    o_ref[...] = x * lax.rsqrt(ms + RMS_EPS) * g_ref[...]


def _final_norm(x, g, tm):
    groups, t, _ = x.shape
    tok = pl.BlockSpec((None, tm, D_MODEL), lambda gi, i: (gi, i, 0))
    return pl.pallas_call(
        _final_norm_kernel,
        grid=(groups, t // tm),
        in_specs=[tok, pl.BlockSpec((1, D_MODEL), lambda gi, i: (0, 0))],
        out_specs=tok,
        out_shape=jax.ShapeDtypeStruct(x.shape, F32),
        compiler_params=_cparams("parallel", "parallel"),
        name="final_norm",
    )(x, g)


def _pad_cols(a, width):
    return jnp.pad(a, [(0, 0)] * (a.ndim - 1) + [(0, width - a.shape[-1])])


def _rwkv_params(l, rwkv_mu, rwkv_w0, rwkv_w2, rwkv_a0, rwkv_a2, rwkv_g2, rwkv_k_k, rwkv_k_a, rwkv_r_k,
                 rwkv_lnx_g, rwkv_lnx_b, vres_w1, vres_w2, vres_b):
    wd = RWKV_WIDTH
    row = lambda a: a.reshape(1, -1)
    lora_rows = lambda w, off: jnp.zeros((wd, wd), F32).at[off:off + w.shape[0]].set(w).astype(BF16)
    seg = np.arange(wd) // RWKV_HEAD_DIM
    rp = dict(
        mu=_pad_cols(row(rwkv_mu[l]), RWKV_PROJ_PAD), w0=row(rwkv_w0[l]), a0=row(rwkv_a0[l]),
        k_k=row(rwkv_k_k[l]), k_a=row(rwkv_k_a[l]), r_k=row(rwkv_r_k[l]),
        w2=lora_rows(rwkv_w2[l], 0), a2=lora_rows(rwkv_a2[l], DECAY_LORA),
        g2=lora_rows(rwkv_g2[l], DECAY_LORA + ICLR_LORA),
        ones_bd=jnp.asarray(seg[:, None] == seg[None, :], BF16),
        lnx_g=row(rwkv_lnx_g[l]), lnx_b=row(rwkv_lnx_b[l]))
    if l > 0:
        rp.update(vw1=_pad_cols(vres_w1[l - 1], LANES).astype(BF16),
                  vw2=jnp.pad(vres_w2[l - 1], ((0, LANES - VRES_LORA), (0, 0))).astype(BF16),
                  vb=row(vres_b[l - 1]))
    return rp


def _layer_common(x, mod, z, o_attn, z_pool, y, bonus, g, lw, l, rp, tm):
    y_rwkv = _rwkv_post(y, bonus, g, rp, _row_tile(x.shape[1], ELEMENTWISE_ROWS))
    merged = _merge(o_attn, z_pool, y_rwkv, z, lw["w_attn_out"], lw["w_pool_out"], lw["w_rwkv_out"], l, tm)
    x = _proj_residual(merged, lw["w_o"], l, x, mod, 2, tm, W_O_TN)
    hidden = _ffn_up(x, lw["norm2_g"][l].reshape(1, -1), mod, lw["w_ffn_gate"], lw["w_ffn_up"], l, tm)
    return _proj_residual(hidden, lw["w_ffn_down"], l, x, mod, 5, _row_tile(x.shape[1], NORM_ROWS), FFN_DOWN_TN)


def kernel(x_prompt, x_sample, c_prompt, c_sample, cache_k_win, cache_v_win, state_pool, state_shift, state_wkv, ada_w, ada_b, norm1_g, norm2_g, w_in, attn_sink, w_attn_out, pool_w, pool_scale, w_pool_out, rwkv_mu, rwkv_w0, rwkv_w2, rwkv_a0, rwkv_a2, rwkv_g2, rwkv_k_k, rwkv_k_a, rwkv_r_k, rwkv_lnx_g, rwkv_lnx_b, w_rwkv_out, vres_w1, vres_w2, vres_b, w_o, w_ffn_gate, w_ffn_up, w_ffn_down, final_norm_g):
    depth = ada_w.shape[0]
    nb, t, _ = x_prompt.shape
    ns = x_sample.shape[0]
    assert x_sample.shape[1] == 1 and t % WINDOW == 0
    win = cache_k_win.shape[2]
    assert win <= WINDOW
    heads, n = RWKV_HEADS, RWKV_HEAD_DIM

    pad_rows = -nb % SUBLANES
    c_all = jnp.concatenate([c_prompt, jnp.zeros((pad_rows, D_MODEL), F32), c_sample], axis=0)
    mod_all = _ada(c_all, ada_w, ada_b)

    cos_p, sin_p = _rope_tables(jnp.arange(t))
    cos_s, sin_s = _rope_tables(jnp.full((ns,), PAST_LEN))

    xp = x_prompt
    xs = x_sample.reshape(1, ns, D_MODEL)
    tm_p = _row_tile(t, MATMUL_ROWS)
    tm_s = _row_tile(ns, MATMUL_ROWS)
    vf_p = vf_s = None
    st_p, st_s = [], []
    lw = dict(norm2_g=norm2_g, w_attn_out=w_attn_out, w_pool_out=w_pool_out, w_rwkv_out=w_rwkv_out, w_o=w_o,
              w_ffn_gate=w_ffn_gate, w_ffn_up=w_ffn_up,
              w_ffn_down=w_ffn_down.astype(BF16))
    w_in_t = w_in.transpose(0, 2, 1)
    to_feature_major = lambda c: c.transpose(0, 1, 3, 4, 2).reshape(depth, ns, KV_WIDTH, win)
    from_feature_major = lambda c: c.reshape(ns, N_KV_HEADS, HEAD_DIM, win).transpose(0, 3, 1, 2)
    cache_kt, cache_vt = to_feature_major(cache_k_win), to_feature_major(cache_v_win)
    state_t = state_wkv.transpose(0, 2, 3, 4, 1)
    for l in range(depth):
        rp = _rwkv_params(l, rwkv_mu, rwkv_w0, rwkv_w2, rwkv_a0, rwkv_a2, rwkv_g2, rwkv_k_k, rwkv_k_a, rwkv_r_k,
                          rwkv_lnx_g, rwkv_lnx_b, vres_w1, vres_w2, vres_b)
        g1 = norm1_g[l].reshape(1, -1)
        mod_p = mod_all[l, :nb].reshape(nb, 1, -1)
        mod_s = mod_all[l, nb + pad_rows:].reshape(1, ns, -1)

        z = _in_proj(xp, g1, mod_p, w_in_t, l, cos_p, sin_p, tm_p)
        o_attn = _attn_prompt(z, attn_sink[l])
        z_pool = _pool_prompt(z, pool_w[l], pool_scale[l].reshape(1, -1), _row_tile(t, ELEMENTWISE_ROWS))
        r, w, k, v, a, b, g, bonus = _rwkv_prep(z, None, vf_p, rp, _row_tile(t, RWKV_ROWS))
        if l == 0:
            vf_p = v
        y, s_pair = _wkv_scan(r, w, k, v, a, b, _row_tile(t, RWKV_ROWS))
        wkv_new = s_pair.reshape(nb, heads // 2, n, 2, n).transpose(0, 1, 3, 2, 4).reshape(nb, heads, n, n)
        tail = z[:, t - WINDOW:, :GL_OFF].astype(F32)
        st_p.append((tail[:, :, K_OFF:V_OFF].reshape(nb, WINDOW, N_KV_HEADS, HEAD_DIM),
                     tail[:, :, V_OFF:U_OFF].reshape(nb, WINDOW, N_KV_HEADS, HEAD_DIM),
                     tail[:, WINDOW - POOL_PAD:, U_OFF:PR_OFF],
                     tail[:, WINDOW - 1, PR_OFF:PR_OFF + RWKV_PROJ],
                     wkv_new))
        xp = _layer_common(xp, mod_p, z, o_attn, z_pool, y, bonus, g, lw, l, rp, tm_p)

        z = _in_proj(xs, g1, mod_s, w_in_t, l, cos_s, sin_s, tm_s)
        zs = z[0, :, :GL_OFF].astype(F32)
        k_new, v_new = zs[:, K_OFF:V_OFF], zs[:, V_OFF:U_OFF]
        o3, kt_new, vt_new = _attn_sample(zs[:, :ATTN_WIDTH].reshape(ns, N_Q_HEADS, HEAD_DIM), k_new, v_new,
                                          cache_kt, cache_vt, l, attn_sink[l])
        o_attn = o3.reshape(1, ns, ATTN_WIDTH).astype(BF16)
        z_pool = _pool_sample(z, state_pool[l].transpose(1, 0, 2), pool_w[l], pool_scale[l].reshape(1, -1))
        prev = _pad_cols(state_shift[l], RWKV_PROJ_PAD).reshape(1, ns, RWKV_PROJ_PAD)
        r, w, k, v, a, b, g, bonus = _rwkv_prep(z, prev, vf_s, rp, tm_s)
        if l == 0:
            vf_s = v
        step_in = jnp.stack([r, w, k, v, a, b]).reshape(6, ns, heads, n).transpose(0, 2, 3, 1)
        y_t, wkv_t = _wkv_step(step_in, state_t, l)
        y = y_t.transpose(2, 0, 1).reshape(1, ns, RWKV_WIDTH)
        u_new = zs[:, None, U_OFF:PR_OFF]
        st_s.append((from_feature_major(kt_new), from_feature_major(vt_new),
                     jnp.concatenate([state_pool[l][:, 1:], u_new], axis=1),
                     zs[:, PR_OFF:PR_OFF + RWKV_PROJ],
                     wkv_t.transpose(3, 0, 1, 2)))
        xs = _layer_common(xs, mod_s, z, o_attn, z_pool, y, bonus, g, lw, l, rp, tm_s)

    g_fin = final_norm_g.reshape(1, -1)
    y_prompt = _final_norm(xp, g_fin, _row_tile(t, NORM_ROWS))
    y_sample = _final_norm(xs, g_fin, _row_tile(ns, NORM_ROWS)).reshape(ns, 1, D_MODEL)
    stack = lambda states, i: jnp.stack([s[i] for s in states])
    return (y_prompt, y_sample) + tuple(stack(st_p, i) for i in range(5)) + tuple(stack(st_s, i) for i in range(5))
```

```python
import functools

import numpy as np
import jax
import jax.numpy as jnp
from jax import lax
from jax.experimental import pallas as pl
from jax.experimental.pallas import tpu as pltpu

F32 = jnp.float32
BF16 = jnp.bfloat16

D_MODEL = 2048
HEAD_DIM = 64
N_Q_HEADS = 16
N_KV_HEADS = 4
GROUP = N_Q_HEADS // N_KV_HEADS
ATTN_WIDTH = N_Q_HEADS * HEAD_DIM
KV_WIDTH = N_KV_HEADS * HEAD_DIM
WINDOW = 128
ROPE_THETA = 10000.0
POOL_WIDTH = 512
POOL_WINDOWS = (2, 4, 8, 16)
POOL_GROUP_WIDTH = 128
POOL_PAD = 15
RWKV_HEAD_DIM = 64
RWKV_WIDTH = 512
RWKV_HEADS = 8
DECAY_LORA = 96
ICLR_LORA = 96
GATE_LORA = 256
VRES_LORA = 32
RWKV_PROJ = 3 * RWKV_WIDTH + DECAY_LORA + ICLR_LORA + GATE_LORA
RWKV_PROJ_PAD = 2048
D_FF = 5632
PAST_LEN = 8192
RMS_EPS = 1e-6
LNX_EPS = 64e-5
NEG_INF = -1e30

K_OFF = ATTN_WIDTH
V_OFF = K_OFF + KV_WIDTH
U_OFF = V_OFF + KV_WIDTH
PR_OFF = U_OFF + POOL_WIDTH
GL_OFF = PR_OFF + RWKV_PROJ_PAD
IN_PAD = GL_OFF + 3 * D_MODEL
ROPE_END = V_OFF

LANES = 128
SUBLANES = 8
VMEM_LIMIT = 56 * 2**20

MATMUL_ROWS = 2048
NORM_ROWS = 1024
ELEMENTWISE_ROWS = 512
RWKV_ROWS = 256
ADA_TN = 1024
W_O_TN = 512
FFN_DOWN_TN = 512
SAMPLE_ATTN_ROWS = 8


def _cparams(*sem):
    return pltpu.CompilerParams(dimension_semantics=sem, vmem_limit_bytes=VMEM_LIMIT)


def _row_tile(t, pref):
    tm = min(t, pref)
    while t % tm:
        tm -= SUBLANES
    return tm


def _sigmoid(x):
    return 0.5 + 0.5 * jnp.tanh(0.5 * x)


def _mod_norm(x, g, sh, sc):
    ms = jnp.mean(x * x, axis=-1, keepdims=True)
    return (x * lax.rsqrt(ms + RMS_EPS) * g) * (1.0 + sc) + sh


NORM_CHUNKS = 4


def _row_chunks(tm, n):
    size = tm // n
    return [slice(c * size, (c + 1) * size) for c in range(n)]


def _rows_of(mod_ref, rows):
    return mod_ref[...] if mod_ref.shape[0] == 1 else mod_ref[rows, :]


def _bdot(a, b):
    return jnp.dot(a.astype(BF16), b.astype(BF16), preferred_element_type=F32)


def _seg_sum(x, ones_bd):
    hi = x.astype(BF16)
    lo = (x - hi.astype(F32)).astype(BF16)
    return (jnp.dot(hi, ones_bd, preferred_element_type=F32)
            + jnp.dot(lo, ones_bd, preferred_element_type=F32))


def _mod_spec(mod, tm, chunk):
    per_row = mod.shape[1] != 1
    rows = tm if per_row else 1
    return pl.BlockSpec((None, rows, D_MODEL), lambda g, i, j=0: (g, i if per_row else 0, chunk))


def _ada_kernel(c_ref, w_ref, b_ref, o_ref):
    c = c_ref[...]
    o_ref[...] = _bdot(c * _sigmoid(c), w_ref[...]) + b_ref[...]


def _ada(c_all, ada_w, ada_b):
    depth, _, width = ada_w.shape
    rows = c_all.shape[0]
    tn = ADA_TN
    return pl.pallas_call(
        _ada_kernel,
        grid=(depth, width // tn),
        in_specs=[pl.BlockSpec((rows, D_MODEL), lambda l, j: (0, 0)),
                  pl.BlockSpec((None, D_MODEL, tn), lambda l, j: (l, 0, j)),
                  pl.BlockSpec((None, 1, tn), lambda l, j: (l, 0, j))],
        out_specs=pl.BlockSpec((None, rows, tn), lambda l, j: (l, 0, j)),
        out_shape=jax.ShapeDtypeStruct((depth, rows, width), F32),
        compiler_params=_cparams("parallel", "parallel"),
        name="ada",
    )(c_all, ada_w, ada_b.reshape(depth, 1, width))


IN_TN = 512
IN_MAIN_TILES = GL_OFF // IN_TN


def _in_proj_kernel(x_ref, g_ref, sh_ref, sc_ref, cos_ref, sin_ref, w_ref, o_ref, h_ref):
    j = pl.program_id(2)
    tm = x_ref.shape[0]

    def project(rows):
        return lax.dot_general(h_ref[rows, :], w_ref[0].astype(BF16), (((1,), (1,)), ((), ())),
                               preferred_element_type=F32)

    def rope_tile(rows, whole_tile):
        acc = project(rows)
        reps = IN_TN // LANES
        cos = jnp.concatenate([cos_ref[rows, :]] * reps, axis=1)
        sin = jnp.concatenate([sin_ref[rows, :]] * reps, axis=1)
        lane = lax.broadcasted_iota(jnp.int32, acc.shape, 1)
        first_half = (lane % HEAD_DIM) < (HEAD_DIM // 2)
        rot = jnp.where(first_half, pltpu.roll(acc, IN_TN - HEAD_DIM // 2, 1),
                        pltpu.roll(acc, HEAD_DIM // 2, 1))
        roped = acc * cos + rot * sin
        if not whole_tile:
            roped = jnp.where(j * IN_TN + lane < ROPE_END, roped, acc)
        o_ref[rows, :] = roped.astype(o_ref.dtype)

    assert IN_TN <= ROPE_END

    @pl.when(j == 0)
    def _():
        for rows in _row_chunks(tm, NORM_CHUNKS):
            h_ref[rows, :] = _mod_norm(x_ref[rows, :], g_ref[...], _rows_of(sh_ref, rows),
                                       _rows_of(sc_ref, rows)).astype(BF16)
            rope_tile(rows, True)

    @pl.when((j > 0) & ((j + 1) * IN_TN <= ROPE_END))
    def _():
        rope_tile(slice(None), True)

    @pl.when((j * IN_TN < ROPE_END) & ((j + 1) * IN_TN > ROPE_END))
    def _():
        rope_tile(slice(None), False)

    @pl.when(j * IN_TN >= ROPE_END)
    def _():
        o_ref[...] = project(slice(None)).astype(o_ref.dtype)


def _in_proj(x, g1, mod, w_in_t, l, cos, sin, tm):
    groups, t, _ = x.shape
    gate_shift = RWKV_PROJ_PAD - RWKV_PROJ

    def weight_rows(g, i, j):
        start = jnp.where(j < IN_MAIN_TILES, j * IN_TN, j * IN_TN - gate_shift)
        return (l, pl.multiple_of(start, SUBLANES), 0)

    return pl.pallas_call(
        _in_proj_kernel,
        grid=(groups, t // tm, IN_PAD // IN_TN),
        in_specs=[pl.BlockSpec((None, tm, D_MODEL), lambda g, i, j: (g, i, 0), pipeline_mode=pl.Buffered(1)),
                  pl.BlockSpec((1, D_MODEL), lambda g, i, j: (0, 0)),
                  _mod_spec(mod, tm, 0), _mod_spec(mod, tm, 1),
                  pl.BlockSpec((tm, LANES), lambda g, i, j: (i, 0), pipeline_mode=pl.Buffered(1)),
                  pl.BlockSpec((tm, LANES), lambda g, i, j: (i, 0), pipeline_mode=pl.Buffered(1)),
                  pl.BlockSpec((pl.Element(1), pl.Element(IN_TN), pl.Element(D_MODEL)), weight_rows)],
        out_specs=pl.BlockSpec((None, tm, IN_TN), lambda g, i, j: (g, i, j)),
        out_shape=jax.ShapeDtypeStruct((groups, t, IN_PAD), BF16),
        scratch_shapes=[pltpu.VMEM((tm, D_MODEL), BF16)],
        compiler_params=_cparams("parallel", "parallel", "arbitrary"),
        name="in_proj",
    )(x, g1, mod, mod, cos, sin, w_in_t)


def _rope_tables(pos):
    half = HEAD_DIM // 2
    inv = ROPE_THETA ** (-jnp.arange(half, dtype=F32) * 2.0 / HEAD_DIM)
    ang = pos.astype(F32)[:, None] * inv[None, :]
    cos, sin = jnp.cos(ang), jnp.sin(ang)
    reps = LANES // HEAD_DIM
    cos_t = jnp.concatenate([cos, cos] * reps, axis=1)
    sin_t = jnp.concatenate([-sin, sin] * reps, axis=1)
    return cos_t, sin_t


def _attn_prompt_kernel(sink_ref, q_ref, k_ref, v_ref, kp_ref, vp_ref, o_ref):
    n = pl.program_id(1)
    blk = q_ref.shape[0]
    q = q_ref[...]
    k2 = jnp.concatenate([kp_ref[...], k_ref[...]], axis=0).astype(BF16)
    v2 = jnp.concatenate([vp_ref[...], v_ref[...]], axis=0).astype(BF16)
    qi = lax.broadcasted_iota(jnp.int32, (blk, 2 * blk), 0)
    kc = lax.broadcasted_iota(jnp.int32, (blk, 2 * blk), 1)
    rel = qi - kc + blk
    first_key = jnp.where(n > 0, 0, blk)
    mask = (rel >= 0) & (rel <= WINDOW) & (kc >= first_key)
    outs = []
    for hd in range(N_Q_HEADS):
        h = hd // GROUP
        kh = k2[:, h * HEAD_DIM:(h + 1) * HEAD_DIM]
        vh = v2[:, h * HEAD_DIM:(h + 1) * HEAD_DIM]
        qh = q[:, hd * HEAD_DIM:(hd + 1) * HEAD_DIM].astype(BF16)
        s = lax.dot_general(qh, kh, (((1,), (1,)), ((), ())), preferred_element_type=F32) * (HEAD_DIM ** -0.5)
        s = jnp.where(mask, s, NEG_INF)
        sk = sink_ref[hd]
        m = jnp.maximum(jnp.max(s, axis=1, keepdims=True), sk)
        p = jnp.exp(s - m)
        den = jnp.sum(p, axis=1, keepdims=True) + jnp.exp(sk - m)
        outs.append(jnp.dot(p.astype(BF16), vh, preferred_element_type=F32) / den)
    o_ref[...] = jnp.concatenate(outs, axis=1).astype(o_ref.dtype)


def _attn_prompt(z, sink):
    groups, t, _ = z.shape
    blk = WINDOW
    kcol, vcol = K_OFF // KV_WIDTH, V_OFF // KV_WIDTH
    prev = lambda col: (lambda g, n: (g, jnp.maximum(n - 1, 0), col))
    return pl.pallas_call(
        _attn_prompt_kernel,
        grid=(groups, t // blk),
        in_specs=[pl.BlockSpec(memory_space=pltpu.SMEM),
                  pl.BlockSpec((None, blk, ATTN_WIDTH), lambda g, n: (g, n, 0)),
                  pl.BlockSpec((None, blk, KV_WIDTH), lambda g, n: (g, n, kcol)),
                  pl.BlockSpec((None, blk, KV_WIDTH), lambda g, n: (g, n, vcol)),
                  pl.BlockSpec((None, blk, KV_WIDTH), prev(kcol)),
                  pl.BlockSpec((None, blk, KV_WIDTH), prev(vcol))],
        out_specs=pl.BlockSpec((None, blk, ATTN_WIDTH), lambda g, n: (g, n, 0)),
        out_shape=jax.ShapeDtypeStruct((groups, t, ATTN_WIDTH), BF16),
        compiler_params=_cparams("parallel", "parallel"),
        name="attn_prompt",
    )(sink, z, z, z, z, z)


def _attn_sample_kernel(sink_ref, q_ref, kn_ref, vn_ref, knc_ref, vnc_ref, kt_ref, vt_ref, o_ref, kto_ref, vto_ref):
    bb = q_ref.shape[0]
    win = kt_ref.shape[2]
    row = lax.broadcasted_iota(jnp.int32, (N_Q_HEADS, KV_WIDTH), 0)
    lane = lax.broadcasted_iota(jnp.int32, (N_Q_HEADS, KV_WIDTH), 1)
    own = (row // GROUP) == (lane // HEAD_DIM)
    newest = lax.broadcasted_iota(jnp.int32, (KV_WIDTH, win), 1) == win - 1
    sink = sink_ref[...]
    scale = HEAD_DIM ** -0.5

    rows = range(bb)
    qbd = [jnp.where(own, jnp.concatenate([q_ref[b]] * N_KV_HEADS, axis=1), 0.0) for b in rows]
    s = [jnp.dot(qbd[b].astype(BF16), kt_ref[b].astype(BF16), preferred_element_type=F32) * scale for b in rows]
    s_new = [jnp.sum(qbd[b] * kn_ref[b], axis=1, keepdims=True) * scale for b in rows]
    m = [jnp.maximum(jnp.maximum(jnp.max(s[b], axis=1, keepdims=True), s_new[b]), sink) for b in rows]
    p = [jnp.exp(s[b] - m[b]) for b in rows]
    p_new = [jnp.exp(s_new[b] - m[b]) for b in rows]
    den = [jnp.sum(p[b], axis=1, keepdims=True) + p_new[b] + jnp.exp(sink - m[b]) for b in rows]
    o = [lax.dot_general(p[b].astype(BF16), vt_ref[b].astype(BF16), (((1,), (1,)), ((), ())),
                         preferred_element_type=F32) + p_new[b] * vn_ref[b] for b in rows]
    for b in rows:
        ob = jnp.where(own, o[b], 0.0)
        o64 = ob[:, 0:HEAD_DIM]
        for h in range(1, N_KV_HEADS):
            o64 = o64 + ob[:, h * HEAD_DIM:(h + 1) * HEAD_DIM]
        o_ref[b] = o64 / den[b]
        kto_ref[b] = jnp.where(newest, knc_ref[b], pltpu.roll(kt_ref[b], win - 1, 1))
        vto_ref[b] = jnp.where(newest, vnc_ref[b], pltpu.roll(vt_ref[b], win - 1, 1))


def _attn_sample(q3, k_new, v_new, kt_all, vt_all, l, sink):
    _, nb, _, win = kt_all.shape
    assert win == LANES
    bb = _row_tile(nb, SAMPLE_ATTN_ROWS)
    row = pl.BlockSpec((bb, 1, KV_WIDTH), lambda i: (i, 0, 0))
    col = pl.BlockSpec((bb, KV_WIDTH, 1), lambda i: (i, 0, 0))
    cache_in = pl.BlockSpec((None, bb, KV_WIDTH, win), lambda i: (l, i, 0, 0))
    cache_out = pl.BlockSpec((bb, KV_WIDTH, win), lambda i: (i, 0, 0))
    heads = pl.BlockSpec((bb, N_Q_HEADS, HEAD_DIM), lambda i: (i, 0, 0))
    return pl.pallas_call(
        _attn_sample_kernel,
        grid=(nb // bb,),
        in_specs=[pl.BlockSpec((N_Q_HEADS, 1), lambda i: (0, 0)), heads, row, row, col, col, cache_in, cache_in],
        out_specs=[heads, cache_out, cache_out],
        out_shape=[jax.ShapeDtypeStruct((nb, N_Q_HEADS, HEAD_DIM), F32),
                   jax.ShapeDtypeStruct((nb, KV_WIDTH, win), F32),
                   jax.ShapeDtypeStruct((nb, KV_WIDTH, win), F32)],
        compiler_params=_cparams("parallel"),
        name="attn_sample",
    )(sink.reshape(N_Q_HEADS, 1), q3, k_new.reshape(nb, 1, KV_WIDTH), v_new.reshape(nb, 1, KV_WIDTH),
      k_new.reshape(nb, KV_WIDTH, 1), v_new.reshape(nb, KV_WIDTH, 1), kt_all, vt_all)


POOL_HALO = 16


def _pool_prompt_kernel(u_ref, halo_ref, w_ref, sc_ref, o_ref, ext_ref):
    i = pl.program_id(1)
    tm = u_ref.shape[0]
    ext_ref[0:POOL_HALO, :] = jnp.where(i > 0, halo_ref[...].astype(F32), 0.0)
    ext_ref[POOL_HALO:POOL_HALO + tm, :] = u_ref[...].astype(F32)
    pos = i * tm + lax.broadcasted_iota(jnp.int32, (tm, 1), 0)
    for gi, w in enumerate(POOL_WINDOWS):
        cols = slice(gi * POOL_GROUP_WIDTH, (gi + 1) * POOL_GROUP_WIDTH)
        u = ext_ref[POOL_HALO:POOL_HALO + tm, cols]
        acc = u
        for back in range(1, w):
            acc = acc + ext_ref[POOL_HALO - back:POOL_HALO - back + tm, cols]
        cnt = jnp.minimum(pos + 1, w).astype(F32)
        d = acc / cnt - u
        o_ref[:, cols] = (_bdot(d, w_ref[gi]) * sc_ref[:, cols]).astype(o_ref.dtype)


def _pool_prompt(z, pool_w, pool_scale, tm):
    groups, t, _ = z.shape
    ucol = U_OFF // POOL_WIDTH
    per = tm // POOL_HALO
    return pl.pallas_call(
        _pool_prompt_kernel,
        grid=(groups, t // tm),
        in_specs=[pl.BlockSpec((None, tm, POOL_WIDTH), lambda g, i: (g, i, ucol)),
                  pl.BlockSpec((None, POOL_HALO, POOL_WIDTH), lambda g, i: (g, jnp.maximum(i * per - 1, 0), ucol)),
                  pl.BlockSpec((len(POOL_WINDOWS), POOL_GROUP_WIDTH, POOL_GROUP_WIDTH), lambda g, i: (0, 0, 0)),
                  pl.BlockSpec((1, POOL_WIDTH), lambda g, i: (0, 0))],
        out_specs=pl.BlockSpec((None, tm, POOL_WIDTH), lambda g, i: (g, i, 0)),
        out_shape=jax.ShapeDtypeStruct((groups, t, POOL_WIDTH), BF16),
        scratch_shapes=[pltpu.VMEM((tm + POOL_HALO, POOL_WIDTH), F32)],
        compiler_params=_cparams("parallel", "parallel"),
        name="pool_prompt",
    )(z, z, pool_w, pool_scale)


def _pool_sample_kernel(u_ref, st_ref, w_ref, sc_ref, o_ref):
    for gi, w in enumerate(POOL_WINDOWS):
        cols = slice(gi * POOL_GROUP_WIDTH, (gi + 1) * POOL_GROUP_WIDTH)
        u = u_ref[:, cols].astype(F32)
        acc = u
        for back in range(1, w):
            acc = acc + st_ref[POOL_PAD - back, :, cols]
        cnt = float(min(PAST_LEN + 1, w))
        d = acc / cnt - u
        o_ref[:, cols] = (_bdot(d, w_ref[gi]) * sc_ref[:, cols]).astype(o_ref.dtype)


def _pool_sample(z, state_t, pool_w, pool_scale):
    rows = z.shape[1]
    ucol = U_OFF // POOL_WIDTH
    return pl.pallas_call(
        _pool_sample_kernel,
        grid=(1,),
        in_specs=[pl.BlockSpec((None, rows, POOL_WIDTH), lambda i: (0, 0, ucol)),
                  pl.BlockSpec((POOL_PAD, rows, POOL_WIDTH), lambda i: (0, 0, 0)),
                  pl.BlockSpec((len(POOL_WINDOWS), POOL_GROUP_WIDTH, POOL_GROUP_WIDTH), lambda i: (0, 0, 0)),
                  pl.BlockSpec((1, POOL_WIDTH), lambda i: (0, 0))],
        out_specs=pl.BlockSpec((None, rows, POOL_WIDTH), lambda i: (0, 0, 0)),
        out_shape=jax.ShapeDtypeStruct((1, rows, POOL_WIDTH), BF16),
        compiler_params=_cparams("arbitrary"),
        name="pool_sample",
    )(z, state_t, pool_w, pool_scale)


PREP_HALO = 16


def _rwkv_prep_kernel(*refs, halo, has_vres):
    it = iter(refs)
    pr_ref, prev_ref = next(it), next(it)
    vf_ref = next(it) if has_vres else None
    mu_ref, w0_ref, a0_ref, kk_ref, ka_ref, rk_ref = (next(it) for _ in range(6))
    w2_ref, a2_ref, g2_ref, ones_ref = (next(it) for _ in range(4))
    if has_vres:
        vw1_ref, vw2_ref, vb_ref = next(it), next(it), next(it)
    r_o, w_o, k_o, v_o, a_o, b_o, g_o, bonus_o = (next(it) for _ in range(8))
    ext_ref = next(it) if halo else None

    pr = pr_ref[...].astype(F32)
    tm = pr.shape[0]
    if halo:
        i = pl.program_id(1)
        ext_ref[0:PREP_HALO, :] = jnp.where(i > 0, prev_ref[...].astype(F32), 0.0)
        ext_ref[PREP_HALO:PREP_HALO + tm, :] = pr
        prev = ext_ref[PREP_HALO - 1:PREP_HALO - 1 + tm, :]
    else:
        prev = prev_ref[...]
    xm = pr + (prev - pr) * mu_ref[...]
    wd = RWKV_WIDTH
    r, k, v, lora_in = xm[:, 0:wd], xm[:, wd:2 * wd], xm[:, 2 * wd:3 * wd], xm[:, 3 * wd:4 * wd]
    ones_bd = ones_ref[...]
    lw = _bdot(jnp.tanh(lora_in), w2_ref[...])
    la = _bdot(lora_in, a2_ref[...])
    g = _bdot(_sigmoid(lora_in), g2_ref[...])
    y = -(w0_ref[...] + lw)
    softplus = jnp.maximum(y, 0.0) + jnp.log(1.0 + jnp.exp(-jnp.abs(y)))
    decay = jnp.exp(-jnp.exp(-softplus - 0.5))
    a = _sigmoid(a0_ref[...] + la)
    if has_vres:
        t2 = _bdot(_bdot(v, vw1_ref[...]), vw2_ref[...])
        v = v + (vf_ref[...] - v) * _sigmoid(vb_ref[...] + t2)
    kk = k * kk_ref[...]
    kk = kk / jnp.maximum(jnp.sqrt(_seg_sum(kk * kk, ones_bd)), 1e-12)
    k = k * (1.0 + (a - 1.0) * ka_ref[...])
    r_o[...] = r
    w_o[...] = decay
    k_o[...] = k
    v_o[...] = v
    a_o[...] = -kk
    b_o[...] = kk * a
    g_o[...] = g
    bonus_o[...] = _seg_sum(r * k * rk_ref[...], ones_bd) * v


def _rwkv_prep(z, prev, v_first, rp, tm):
    groups, t, _ = z.shape
    halo = prev is None
    has_vres = v_first is not None
    prcol = PR_OFF // RWKV_PROJ_PAD
    wd = RWKV_WIDTH
    tok = lambda width: pl.BlockSpec((None, tm, width), lambda g, i: (g, i, 0))
    vec = lambda width: pl.BlockSpec((1, width), lambda g, i: (0, 0))
    mat = lambda a, b: pl.BlockSpec((a, b), lambda g, i: (0, 0))
    args = [z]
    in_specs = [pl.BlockSpec((None, tm, RWKV_PROJ_PAD), lambda g, i: (g, i, prcol))]
    if halo:
        per = tm // PREP_HALO
        args.append(z)
        in_specs.append(pl.BlockSpec((None, PREP_HALO, RWKV_PROJ_PAD),
                                     lambda g, i: (g, jnp.maximum(i * per - 1, 0), prcol)))
    else:
        args.append(prev)
        in_specs.append(tok(RWKV_PROJ_PAD))
    if has_vres:
        args.append(v_first)
        in_specs.append(tok(wd))
    args += [rp["mu"], rp["w0"], rp["a0"], rp["k_k"], rp["k_a"], rp["r_k"],
             rp["w2"], rp["a2"], rp["g2"], rp["ones_bd"]]
    in_specs += [vec(RWKV_PROJ_PAD)] + [vec(wd)] * 5 + [mat(wd, wd)] * 4
    if has_vres:
        args += [rp["vw1"], rp["vw2"], rp["vb"]]
        in_specs += [mat(wd, LANES), mat(LANES, wd), vec(wd)]
    return pl.pallas_call(
        functools.partial(_rwkv_prep_kernel, halo=halo, has_vres=has_vres),
        grid=(groups, t // tm),
        in_specs=in_specs,
        out_specs=[tok(wd)] * 8,
        out_shape=[jax.ShapeDtypeStruct((groups, t, wd), F32)] * 8,
        scratch_shapes=[pltpu.VMEM((tm + PREP_HALO, RWKV_PROJ_PAD), F32)] if halo else [],
        compiler_params=_cparams("parallel", "parallel"),
        name="rwkv_prep",
    )(*args)


SCAN_STEPS_PER_TRIP = 128


def _wkv_scan_kernel(r_ref, w_ref, k_ref, v_ref, a_ref, b_ref, y_ref, s_ref):
    nb, tc, _ = r_ref.shape
    pairs = RWKV_HEADS // 2
    n = RWKV_HEAD_DIM

    @pl.when(pl.program_id(0) == 0)
    def _():
        s_ref[...] = jnp.zeros(s_ref.shape, F32)

    row = lax.broadcasted_iota(jnp.int32, (n, 2 * n), 0)
    lane = lax.broadcasted_iota(jnp.int32, (n, 2 * n), 1)
    diag = (lane % n) == row
    chains = [(bi, slice(p * 2 * n, (p + 1) * 2 * n)) for bi in range(nb) for p in range(pairs)]
    nc = len(chains)
    seg_r = lax.broadcasted_iota(jnp.int32, (2 * n, 2 * n), 0) // n
    seg_c = lax.broadcasted_iota(jnp.int32, (2 * n, 2 * n), 1) // n
    ones_bd = (seg_r == seg_c).astype(BF16)
    stack = lambda xs: jnp.concatenate(xs, axis=0)

    def seg_bcast(x):
        return jnp.dot(x.astype(BF16), ones_bd, preferred_element_type=F32)

    span = SCAN_STEPS_PER_TRIP

    def steps(trip, carry):
        base = pl.multiple_of(trip * span, span)
        rows = pl.ds(base, span)
        tiles = [[ref[bi, rows, sl] for ref in (r_ref, w_ref, k_ref, v_ref, a_ref, b_ref)] for bi, sl in chains]
        s = [s_ref[c] for c in range(nc)]
        ys = [[] for _ in chains]

        def y_rows(ycol):
            for c in range(nc):
                ys[c].append(jnp.sum(jnp.where(diag, ycol[c * n:(c + 1) * n], 0.0), axis=0, keepdims=True))

        vcol = seg_bcast(stack([jnp.where(diag, t[3][slice(u, u + 1)], 0.0) for u in range(span) for t in tiles]))
        sr = None
        for u in range(span):
            one = slice(u, u + 1)
            red = seg_bcast(stack([s[c] * tiles[c][4][one] for c in range(nc)] + (sr or [])))
            if sr:
                y_rows(red[nc * n:])
            for c, (r8, w8, k8, v8, a8, b8) in enumerate(tiles):
                s[c] = (s[c] * w8[one] + red[c * n:(c + 1) * n] * b8[one]
                        + vcol[(u * nc + c) * n:(u * nc + c + 1) * n] * k8[one])
            sr = [s[c] * tiles[c][0][one] for c in range(nc)]
        y_rows(seg_bcast(stack(sr)))
        for c, (bi, sl) in enumerate(chains):
            s_ref[c] = s[c]
            y_ref[bi, rows, sl] = jnp.concatenate(ys[c], axis=0)
        return carry

    lax.fori_loop(0, tc // span, steps, 0)


def _wkv_scan(r, w, k, v, a, b, tc):
    nb, t, wd = r.shape
    pairs = RWKV_HEADS // 2
    tok = pl.BlockSpec((nb, tc, wd), lambda i: (0, i, 0))
    return pl.pallas_call(
        _wkv_scan_kernel,
        grid=(t // tc,),
        in_specs=[tok] * 6,
        out_specs=[tok, pl.BlockSpec((nb * pairs, RWKV_HEAD_DIM, 2 * RWKV_HEAD_DIM), lambda i: (0, 0, 0))],
        out_shape=[jax.ShapeDtypeStruct((nb, t, wd), F32),
                   jax.ShapeDtypeStruct((nb * pairs, RWKV_HEAD_DIM, 2 * RWKV_HEAD_DIM), F32)],
        compiler_params=_cparams("arbitrary"),
        name="wkv_scan",
    )(r, w, k, v, a, b)


def _wkv_step_kernel(x_ref, s_ref, y_ref, so_ref):
    n = RWKV_HEAD_DIM
    r, w, k, v, a, b = (x_ref[q] for q in range(6))
    for i0 in range(0, n, SUBLANES):
        rows = range(i0, i0 + SUBLANES)
        s = [s_ref[i] for i in rows]
        sa = [jnp.sum(si * a, axis=0, keepdims=True) for si in s]
        s = [si * w + sai * b + v[i:i + 1] * k for si, sai, i in zip(s, sa, rows)]
        for si, i in zip(s, rows):
            so_ref[i] = si
        y_ref[i0:i0 + SUBLANES, :] = jnp.concatenate([jnp.sum(si * r, axis=0, keepdims=True) for si in s], axis=0)


def _wkv_step(x, state_t, l):
    _, heads, n, nb = x.shape
    return pl.pallas_call(
        _wkv_step_kernel,
        grid=(heads,),
        in_specs=[pl.BlockSpec((6, None, n, nb), lambda h: (0, h, 0, 0)),
                  pl.BlockSpec((None, None, n, n, nb), lambda h: (l, h, 0, 0, 0))],
        out_specs=[pl.BlockSpec((None, n, nb), lambda h: (h, 0, 0)),
                   pl.BlockSpec((None, n, n, nb), lambda h: (h, 0, 0, 0))],
        out_shape=[jax.ShapeDtypeStruct((heads, n, nb), F32), jax.ShapeDtypeStruct((heads, n, n, nb), F32)],
        compiler_params=_cparams("parallel"),
        name="wkv_step",
    )(x, state_t)


def _rwkv_post_kernel(y_ref, bonus_ref, g_ref, lg_ref, lb_ref, ones_ref, o_ref):
    y = y_ref[...]
    ones_bd = ones_ref[...]
    inv = 1.0 / RWKV_HEAD_DIM
    d = y - _seg_sum(y, ones_bd) * inv
    var = _seg_sum(d * d, ones_bd) * inv
    yn = d * lax.rsqrt(var + LNX_EPS) * lg_ref[...] + lb_ref[...]
    o_ref[...] = ((yn + bonus_ref[...]) * g_ref[...]).astype(o_ref.dtype)


def _rwkv_post(y, bonus, g, rp, tm):
    groups, t, wd = y.shape
    tok = pl.BlockSpec((None, tm, wd), lambda gi, i: (gi, i, 0))
    vec = pl.BlockSpec((1, wd), lambda gi, i: (0, 0))
    return pl.pallas_call(
        _rwkv_post_kernel,
        grid=(groups, t // tm),
        in_specs=[tok, tok, tok, vec, vec, pl.BlockSpec((wd, wd), lambda gi, i: (0, 0))],
        out_specs=tok,
        out_shape=jax.ShapeDtypeStruct((groups, t, wd), BF16),
        compiler_params=_cparams("parallel", "parallel"),
        name="rwkv_post",
    )(y, bonus, g, rp["lnx_g"], rp["lnx_b"], rp["ones_bd"])


MERGE_TN = 512


def _merge_kernel(oa_ref, zp_ref, yr_ref, ga_ref, gp_ref, gr_ref, wa_ref, wp_ref, wr_ref, o_ref):
    def gated(g_ref, a_ref, w_ref):
        d = _bdot(a_ref[...], w_ref[...])
        return d + d * jnp.tanh(0.5 * g_ref[...].astype(F32))

    merged = 0.5 * (gated(ga_ref, oa_ref, wa_ref) + gated(gp_ref, zp_ref, wp_ref) + gated(gr_ref, yr_ref, wr_ref))
    o_ref[...] = merged.astype(o_ref.dtype)


def _merge(o_attn, z_pool, y_rwkv, z, w_attn_out, w_pool_out, w_rwkv_out, l, tm):
    groups, t, _ = z.shape
    tn = MERGE_TN
    gate = lambda br: pl.BlockSpec((None, tm, tn), lambda g, i, j: (g, i, (GL_OFF + br * D_MODEL) // tn + j))
    tok = lambda width: pl.BlockSpec((None, tm, width), lambda g, i, j: (g, i, 0))
    wt = lambda rows: pl.BlockSpec((None, rows, tn), lambda g, i, j: (l, 0, j))
    return pl.pallas_call(
        _merge_kernel,
        grid=(groups, t // tm, D_MODEL // tn),
        in_specs=[tok(ATTN_WIDTH), tok(POOL_WIDTH), tok(RWKV_WIDTH), gate(0), gate(1), gate(2),
                  wt(ATTN_WIDTH), wt(POOL_WIDTH), wt(RWKV_WIDTH)],
        out_specs=pl.BlockSpec((None, tm, tn), lambda g, i, j: (g, i, j)),
        out_shape=jax.ShapeDtypeStruct((groups, t, D_MODEL), BF16),
        compiler_params=_cparams("parallel", "parallel", "arbitrary"),
        name="merge",
    )(o_attn, z_pool, y_rwkv, z, z, z, w_attn_out, w_pool_out, w_rwkv_out)


def _proj_residual_kernel(a_ref, w_ref, x_ref, gt_ref, o_ref):
    o_ref[...] = x_ref[...] + (1.0 + gt_ref[...]) * _bdot(a_ref[...], w_ref[...])


def _proj_residual(a, w, l, x, mod, chunk, tm, tn):
    groups, t, kdim = a.shape
    per_row = mod.shape[1] != 1
    gate = pl.BlockSpec((None, tm if per_row else 1, tn),
                        lambda g, i, j: (g, i if per_row else 0, chunk * (D_MODEL // tn) + j))
    return pl.pallas_call(
        _proj_residual_kernel,
        grid=(groups, t // tm, D_MODEL // tn),
        in_specs=[pl.BlockSpec((None, tm, kdim), lambda g, i, j: (g, i, 0)),
                  pl.BlockSpec((None, kdim, tn), lambda g, i, j: (l, 0, j)),
                  pl.BlockSpec((None, tm, tn), lambda g, i, j: (g, i, j)),
                  gate],
        out_specs=pl.BlockSpec((None, tm, tn), lambda g, i, j: (g, i, j)),
        out_shape=jax.ShapeDtypeStruct((groups, t, D_MODEL), F32),
        compiler_params=_cparams("parallel", "parallel", "arbitrary"),
        name="proj_residual",
    )(a, w, x, mod)


FFN_TN = 256


def _ffn_up_kernel(x_ref, g_ref, sh_ref, sc_ref, wg_ref, wu_ref, o_ref, h_ref):
    j = pl.program_id(2)
    tm = x_ref.shape[0]

    def swiglu(rows):
        h = h_ref[rows, :]
        gate = jnp.dot(h, wg_ref[...].astype(BF16), preferred_element_type=F32)
        up = jnp.dot(h, wu_ref[...].astype(BF16), preferred_element_type=F32)
        o_ref[rows, :] = (gate * _sigmoid(gate) * up).astype(o_ref.dtype)

    @pl.when(j == 0)
    def _():
        for rows in _row_chunks(tm, NORM_CHUNKS):
            h_ref[rows, :] = _mod_norm(x_ref[rows, :], g_ref[...], _rows_of(sh_ref, rows),
                                       _rows_of(sc_ref, rows)).astype(BF16)
            swiglu(rows)

    @pl.when(j > 0)
    def _():
        swiglu(slice(None))


def _ffn_up(x, g2, mod, w_gate, w_up, l, tm):
    groups, t, _ = x.shape
    tn = FFN_TN
    wt = pl.BlockSpec((None, D_MODEL, tn), lambda g, i, j: (l, 0, j))
    return pl.pallas_call(
        _ffn_up_kernel,
        grid=(groups, t // tm, D_FF // tn),
        in_specs=[pl.BlockSpec((None, tm, D_MODEL), lambda g, i, j: (g, i, 0), pipeline_mode=pl.Buffered(1)),
                  pl.BlockSpec((1, D_MODEL), lambda g, i, j: (0, 0)),
                  _mod_spec(mod, tm, 3), _mod_spec(mod, tm, 4), wt, wt],
        out_specs=pl.BlockSpec((None, tm, tn), lambda g, i, j: (g, i, j)),
        out_shape=jax.ShapeDtypeStruct((groups, t, D_FF), BF16),
        scratch_shapes=[pltpu.VMEM((tm, D_MODEL), BF16)],
        compiler_params=_cparams("parallel", "parallel", "arbitrary"),
        name="ffn_up",
    )(x, g2, mod, mod, w_gate, w_up)


def _final_norm_kernel(x_ref, g_ref, o_ref):
    x = x_ref[...]
    ms = jnp.mean(x * x, axis=-1, keepdims=True)
    o_ref[...] = x * lax.rsqrt(ms + RMS_EPS) * g_ref[...]


def _final_norm(x, g, tm):
    groups, t, _ = x.shape
    tok = pl.BlockSpec((None, tm, D_MODEL), lambda gi, i: (gi, i, 0))
    return pl.pallas_call(
        _final_norm_kernel,
        grid=(groups, t // tm),
        in_specs=[tok, pl.BlockSpec((1, D_MODEL), lambda gi, i: (0, 0))],
        out_specs=tok,
        out_shape=jax.ShapeDtypeStruct(x.shape, F32),
        compiler_params=_cparams("parallel", "parallel"),
        name="final_norm",
    )(x, g)


def _pad_cols(a, width):
    return jnp.pad(a, [(0, 0)] * (a.ndim - 1) + [(0, width - a.shape[-1])])


def _rwkv_params(l, rwkv_mu, rwkv_w0, rwkv_w2, rwkv_a0, rwkv_a2, rwkv_g2, rwkv_k_k, rwkv_k_a, rwkv_r_k,
                 rwkv_lnx_g, rwkv_lnx_b, vres_w1, vres_w2, vres_b):
    wd = RWKV_WIDTH
    row = lambda a: a.reshape(1, -1)
    lora_rows = lambda w, off: jnp.zeros((wd, wd), F32).at[off:off + w.shape[0]].set(w).astype(BF16)
    seg = np.arange(wd) // RWKV_HEAD_DIM
    rp = dict(
        mu=_pad_cols(row(rwkv_mu[l]), RWKV_PROJ_PAD), w0=row(rwkv_w0[l]), a0=row(rwkv_a0[l]),
        k_k=row(rwkv_k_k[l]), k_a=row(rwkv_k_a[l]), r_k=row(rwkv_r_k[l]),
        w2=lora_rows(rwkv_w2[l], 0), a2=lora_rows(rwkv_a2[l], DECAY_LORA),
        g2=lora_rows(rwkv_g2[l], DECAY_LORA + ICLR_LORA),
        ones_bd=jnp.asarray(seg[:, None] == seg[None, :], BF16),
        lnx_g=row(rwkv_lnx_g[l]), lnx_b=row(rwkv_lnx_b[l]))
    if l > 0:
        rp.update(vw1=_pad_cols(vres_w1[l - 1], LANES).astype(BF16),
                  vw2=jnp.pad(vres_w2[l - 1], ((0, LANES - VRES_LORA), (0, 0))).astype(BF16),
                  vb=row(vres_b[l - 1]))
    return rp


def _layer_common(x, mod, z, o_attn, z_pool, y, bonus, g, lw, l, rp, tm):
    y_rwkv = _rwkv_post(y, bonus, g, rp, _row_tile(x.shape[1], ELEMENTWISE_ROWS))
    merged = _merge(o_attn, z_pool, y_rwkv, z, lw["w_attn_out"], lw["w_pool_out"], lw["w_rwkv_out"], l, tm)
    x = _proj_residual(merged, lw["w_o"], l, x, mod, 2, tm, W_O_TN)
    hidden = _ffn_up(x, lw["norm2_g"][l].reshape(1, -1), mod, lw["w_ffn_gate"], lw["w_ffn_up"], l, tm)
    return _proj_residual(hidden, lw["w_ffn_down"], l, x, mod, 5, _row_tile(x.shape[1], NORM_ROWS), FFN_DOWN_TN)


def kernel(x_prompt, x_sample, c_prompt, c_sample, cache_k_win, cache_v_win, state_pool, state_shift, state_wkv, ada_w, ada_b, norm1_g, norm2_g, w_in, attn_sink, w_attn_out, pool_w, pool_scale, w_pool_out, rwkv_mu, rwkv_w0, rwkv_w2, rwkv_a0, rwkv_a2, rwkv_g2, rwkv_k_k, rwkv_k_a, rwkv_r_k, rwkv_lnx_g, rwkv_lnx_b, w_rwkv_out, vres_w1, vres_w2, vres_b, w_o, w_ffn_gate, w_ffn_up, w_ffn_down, final_norm_g):
    depth = ada_w.shape[0]
    nb, t, _ = x_prompt.shape
    ns = x_sample.shape[0]
    assert x_sample.shape[1] == 1 and t % WINDOW == 0
    win = cache_k_win.shape[2]
    assert win <= WINDOW
    heads, n = RWKV_HEADS, RWKV_HEAD_DIM

    pad_rows = -nb % SUBLANES
    c_all = jnp.concatenate([c_prompt, jnp.zeros((pad_rows, D_MODEL), F32), c_sample], axis=0)
    mod_all = _ada(c_all, ada_w, ada_b)

    cos_p, sin_p = _rope_tables(jnp.arange(t))
    cos_s, sin_s = _rope_tables(jnp.full((ns,), PAST_LEN))

    xp = x_prompt
    xs = x_sample.reshape(1, ns, D_MODEL)
    tm_p = _row_tile(t, MATMUL_ROWS)
    tm_s = _row_tile(ns, MATMUL_ROWS)
    vf_p = vf_s = None
    st_p, st_s = [], []
    lw = dict(norm2_g=norm2_g, w_attn_out=w_attn_out, w_pool_out=w_pool_out, w_rwkv_out=w_rwkv_out, w_o=w_o,
              w_ffn_gate=w_ffn_gate, w_ffn_up=w_ffn_up,
              w_ffn_down=w_ffn_down.astype(BF16))
    w_in_t = w_in.transpose(0, 2, 1)
    to_feature_major = lambda c: c.transpose(0, 1, 3, 4, 2).reshape(depth, ns, KV_WIDTH, win)
    from_feature_major = lambda c: c.reshape(ns, N_KV_HEADS, HEAD_DIM, win).transpose(0, 3, 1, 2)
    cache_kt, cache_vt = to_feature_major(cache_k_win), to_feature_major(cache_v_win)
    state_t = state_wkv.transpose(0, 2, 3, 4, 1)
    for l in range(depth):
        rp = _rwkv_params(l, rwkv_mu, rwkv_w0, rwkv_w2, rwkv_a0, rwkv_a2, rwkv_g2, rwkv_k_k, rwkv_k_a, rwkv_r_k,
                          rwkv_lnx_g, rwkv_lnx_b, vres_w1, vres_w2, vres_b)
        g1 = norm1_g[l].reshape(1, -1)
        mod_p = mod_all[l, :nb].reshape(nb, 1, -1)
        mod_s = mod_all[l, nb + pad_rows:].reshape(1, ns, -1)

        z = _in_proj(xp, g1, mod_p, w_in_t, l, cos_p, sin_p, tm_p)
        o_attn = _attn_prompt(z, attn_sink[l])
        z_pool = _pool_prompt(z, pool_w[l], pool_scale[l].reshape(1, -1), _row_tile(t, ELEMENTWISE_ROWS))
        r, w, k, v, a, b, g, bonus = _rwkv_prep(z, None, vf_p, rp, _row_tile(t, RWKV_ROWS))
        if l == 0:
            vf_p = v
        y, s_pair = _wkv_scan(r, w, k, v, a, b, _row_tile(t, RWKV_ROWS))
        wkv_new = s_pair.reshape(nb, heads // 2, n, 2, n).transpose(0, 1, 3, 2, 4).reshape(nb, heads, n, n)
        tail = z[:, t - WINDOW:, :GL_OFF].astype(F32)
        st_p.append((tail[:, :, K_OFF:V_OFF].reshape(nb, WINDOW, N_KV_HEADS, HEAD_DIM),
                     tail[:, :, V_OFF:U_OFF].reshape(nb, WINDOW, N_KV_HEADS, HEAD_DIM),
                     tail[:, WINDOW - POOL_PAD:, U_OFF:PR_OFF],
                     tail[:, WINDOW - 1, PR_OFF:PR_OFF + RWKV_PROJ],
                     wkv_new))
        xp = _layer_common(xp, mod_p, z, o_attn, z_pool, y, bonus, g, lw, l, rp, tm_p)

        z = _in_proj(xs, g1, mod_s, w_in_t, l, cos_s, sin_s, tm_s)
        zs = z[0, :, :GL_OFF].astype(F32)
        k_new, v_new = zs[:, K_OFF:V_OFF], zs[:, V_OFF:U_OFF]
        o3, kt_new, vt_new = _attn_sample(zs[:, :ATTN_WIDTH].reshape(ns, N_Q_HEADS, HEAD_DIM), k_new, v_new,
                                          cache_kt, cache_vt, l, attn_sink[l])
        o_attn = o3.reshape(1, ns, ATTN_WIDTH).astype(BF16)
        z_pool = _pool_sample(z, state_pool[l].transpose(1, 0, 2), pool_w[l], pool_scale[l].reshape(1, -1))
        prev = _pad_cols(state_shift[l], RWKV_PROJ_PAD).reshape(1, ns, RWKV_PROJ_PAD)
        r, w, k, v, a, b, g, bonus = _rwkv_prep(z, prev, vf_s, rp, tm_s)
        if l == 0:
            vf_s = v
        step_in = jnp.stack([r, w, k, v, a, b]).reshape(6, ns, heads, n).transpose(0, 2, 3, 1)
        y_t, wkv_t = _wkv_step(step_in, state_t, l)
        y = y_t.transpose(2, 0, 1).reshape(1, ns, RWKV_WIDTH)
        u_new = zs[:, None, U_OFF:PR_OFF]
        st_s.append((from_feature_major(kt_new), from_feature_major(vt_new),
                     jnp.concatenate([state_pool[l][:, 1:], u_new], axis=1),
                     zs[:, PR_OFF:PR_OFF + RWKV_PROJ],
                     wkv_t.transpose(3, 0, 1, 2)))
        xs = _layer_common(xs, mod_s, z, o_attn, z_pool, y, bonus, g, lw, l, rp, tm_s)

    g_fin = final_norm_g.reshape(1, -1)
    y_prompt = _final_norm(xp, g_fin, _row_tile(t, NORM_ROWS))
    y_sample = _final_norm(xs, g_fin, _row_tile(ns, NORM_ROWS)).reshape(ns, 1, D_MODEL)
    stack = lambda states, i: jnp.stack([s[i] for s in states])
    return (y_prompt, y_sample) + tuple(stack(st_p, i) for i in range(5)) + tuple(stack(st_s, i) for i in range(5))
```
